```python
import math
import jax, jax.numpy as jnp
from jax import lax
import numpy as np

D_MODEL = 1024
BATCH = 1
SEQ = 16384
DEPTH = 1
DEC_BATCH = 8
DEC_SEQ = 4096
PAST_LEN = 128

M_WIDTH = 1024
M_HEADS = 4
M_HEAD_DIM = M_WIDTH // M_HEADS
M_CHUNK = 128
M_CONV = 5
A_HEADS = 8
A_NOPE = 128
A_ROPE = 64
A_V = 128
A_Q_RANK = 256
A_KV_RANK = 256
ROPE_BASE = 10000.0
Q_BLOCK = 128
X_HEADS = 4
X_HEAD_DIM = D_MODEL // X_HEADS
N_MEM = 256
N_EXPERTS = 16
EC_FACTOR = 2
E_FF = 1024
NORM_EPS = 1e-6
IN_SPLITS = (M_WIDTH, M_WIDTH, M_WIDTH, M_WIDTH, 4 * M_HEADS, A_Q_RANK, A_KV_RANK, A_ROPE, D_MODEL, D_MODEL)
IN_WIDTH = 4 * M_WIDTH + 4 * M_HEADS + A_Q_RANK + A_KV_RANK + A_ROPE + 2 * D_MODEL

kernel_name = 'hybrid_mlstm_mla_ec_encoder'


def rmsnorm(x, g):
    xf = x.astype(jnp.float32)
    y = xf * lax.rsqrt(jnp.mean(xf * xf, axis=-1, keepdims=True) + NORM_EPS)
    return (y * g.astype(jnp.float32)).astype(x.dtype)


def split_cols(a, sizes):
    outs, off = [], 0
    for s in sizes:
        outs.append(a[..., off:off + s])
        off += s
    return outs


def rope_tables(S):
    pos = jnp.arange(S, dtype=jnp.float32)
    inv = ROPE_BASE ** (-jnp.arange(0, A_ROPE, 2, dtype=jnp.float32) / A_ROPE)
    ang = pos[:, None] * inv[None, :]
    return jnp.cos(ang), jnp.sin(ang)


def apply_rope(x, cos, sin):
    xf = x.astype(jnp.float32)
    x1, x2 = jnp.split(xf, 2, axis=-1)
    return jnp.concatenate([x1 * cos - x2 * sin, x1 * sin + x2 * cos], axis=-1).astype(x.dtype)


def centred_depthwise_conv(x, w, b):
    C = x.shape[-1]
    y = lax.conv_general_dilated(x, w[:, None, :], window_strides=(1,),
                                 padding=[(M_CONV // 2, M_CONV // 2)],
                                 dimension_numbers=('NWC', 'WIO', 'NWC'),
                                 feature_group_count=C)
    return y + b


def mlstm_chunkwise(q, k, v, i_pre, f_pre):
    B, S, H, d = q.shape
    L = M_CHUNK
    NC = S // L
    f32 = jnp.float32

    def chunks4(a):
        return a.astype(f32).reshape(B, NC, L, H, d).transpose(1, 0, 3, 2, 4)

    def chunks3(a):
        return a.astype(f32).reshape(B, NC, L, H).transpose(1, 0, 3, 2)

    qc = chunks4(q) * (d ** -0.5)
    kc = chunks4(k)
    vc = chunks4(v)
    ic = chunks3(i_pre)
    fc = chunks3(jax.nn.log_sigmoid(f_pre.astype(f32)))
    tril = jnp.tril(jnp.ones((L, L), dtype=bool))

    def step(carry, xs):
        C, n, m = carry
        qb, kb, vb, ib, fb = xs
        b = jnp.cumsum(fb, axis=-1)
        a = b + m[..., None]
        D = jnp.where(tril, b[..., :, None] - b[..., None, :] + ib[..., None, :], -jnp.inf)
        m_t = jnp.maximum(a, jnp.max(D, axis=-1))
        w_inter = jnp.exp(a - m_t)
        s = jnp.einsum('bhtd,bhsd->bhts', qb, kb) * jnp.exp(D - m_t[..., None])
        num = jnp.einsum('bhts,bhsd->bhtd', s, vb) + w_inter[..., None] * jnp.einsum('bhtd,bhde->bhte', qb, C)
        den = jnp.sum(s, axis=-1) + w_inter * jnp.einsum('bhtd,bhd->bht', qb, n)
        h = num / jnp.maximum(jnp.abs(den), jnp.exp(-m_t))[..., None]
        bL = b[..., -1]
        g = bL[..., None] - b + ib
        m_new = jnp.maximum(bL + m, jnp.max(g, axis=-1))
        decay = jnp.exp(bL + m - m_new)
        wk = jnp.exp(g - m_new[..., None])
        C_new = decay[..., None, None] * C + jnp.einsum('bhs,bhsd,bhse->bhde', wk, kb, vb)
        n_new = decay[..., None] * n + jnp.einsum('bhs,bhsd->bhd', wk, kb)
        return (C_new, n_new, m_new), h

    init = (jnp.zeros((B, H, d, d), f32), jnp.zeros((B, H, d), f32), jnp.zeros((B, H), f32))
    _, h = lax.scan(step, init, (qc, kc, vc, ic, fc))
    return h.transpose(1, 0, 3, 2, 4).reshape(B, S, H, d)


def mla_attention(q_nope, q_rope, k_nope, k_rope, v):
    B, S, H, _ = q_nope.shape
    NB = S // Q_BLOCK
    scale = (A_NOPE + A_ROPE) ** -0.5

    def blocks(a):
        return a.reshape((B, NB, Q_BLOCK) + a.shape[2:]).swapaxes(0, 1)

    def attend(qs):
        qn, qr = qs
        s = jnp.einsum('bqhd,bkhd->bhqk', qn, k_nope) + jnp.einsum('bqhd,bkd->bhqk', qr, k_rope)
        p = jax.nn.softmax(s.astype(jnp.float32) * scale, axis=-1).astype(v.dtype)
        return jnp.einsum('bhqk,bkhd->bqhd', p, v)

    o = lax.map(attend, (blocks(q_nope), blocks(q_rope)))
    return o.swapaxes(0, 1).reshape(B, S, H * A_V)


def token_mixer(xn, w_in, b_gates, conv_w, conv_b, mh_norm_g, g_cq, g_ckv, w_uq, w_ukv, w_br_a, w_br_b, w_out):
    B, S, _ = xn.shape
    z = xn @ w_in
    qm, km, vm, om, gates, cq, ckv, kr, ga, gb = split_cols(z, IN_SPLITS)

    qk = jax.nn.silu(centred_depthwise_conv(jnp.concatenate([qm, km], axis=-1), conv_w, conv_b))
    qm, km = qk[..., :M_WIDTH], qk[..., M_WIDTH:]
    gates = (gates.astype(jnp.float32) + b_gates.astype(jnp.float32)).reshape(B, S, 4, M_HEADS)
    i_f, f_f, i_b, f_b = gates[:, :, 0], gates[:, :, 1], gates[:, :, 2], gates[:, :, 3]
    q = qm.reshape(B, S, M_HEADS, M_HEAD_DIM)
    k = km.reshape(B, S, M_HEADS, M_HEAD_DIM)
    v = vm.reshape(B, S, M_HEADS, M_HEAD_DIM)
    h_fwd = mlstm_chunkwise(q, k, v, i_f, f_f)
    h_bwd = jnp.flip(mlstm_chunkwise(jnp.flip(q, 1), jnp.flip(k, 1), jnp.flip(v, 1),
                                     jnp.flip(i_b, 1), jnp.flip(f_b, 1)), 1)
    h = rmsnorm(h_fwd + h_bwd, mh_norm_g.reshape(M_HEADS, M_HEAD_DIM)).astype(xn.dtype)
    h = h.reshape(B, S, M_WIDTH) * jax.nn.sigmoid(om)
    y_a = h @ w_br_a

    cos, sin = rope_tables(S)
    qa = (rmsnorm(cq, g_cq) @ w_uq).reshape(B, S, A_HEADS, A_NOPE + A_ROPE)
    q_nope, q_rope = qa[..., :A_NOPE], qa[..., A_NOPE:]
    q_rope = apply_rope(q_rope, cos[:, None, :], sin[:, None, :])
    kv = (rmsnorm(ckv, g_ckv) @ w_ukv).reshape(B, S, A_HEADS, A_NOPE + A_V)
    k_nope, v_a = kv[..., :A_NOPE], kv[..., A_NOPE:]
    k_rope = apply_rope(kr, cos, sin)
    y_b = mla_attention(q_nope, q_rope, k_nope, k_rope, v_a) @ w_br_b

    merged = jax.nn.sigmoid(ga) * y_a + jax.nn.sigmoid(gb) * y_b
    return merged @ w_out


def cross_attention(xn, memn, w_xq, w_xk, w_xv, w_xo):
    B, S, _ = xn.shape
    M = memn.shape[1]
    q = (xn @ w_xq).reshape(B, S, X_HEADS, X_HEAD_DIM)
    k = (memn @ w_xk).reshape(B, M, X_HEADS, X_HEAD_DIM)
    v = (memn @ w_xv).reshape(B, M, X_HEADS, X_HEAD_DIM)
    s = jnp.einsum('bqhd,bkhd->bhqk', q, k).astype(jnp.float32) * (X_HEAD_DIM ** -0.5)
    p = jax.nn.softmax(s, axis=-1).astype(v.dtype)
    o = jnp.einsum('bhqk,bkhd->bqhd', p, v).reshape(B, S, X_HEADS * X_HEAD_DIM)
    return o @ w_xo


def ec_moe(xn, w_router, b_router, w_e_gate, w_e_up, w_e_down):
    B, S, D = xn.shape
    T = B * S
    xt = xn.reshape(T, D)
    aff = jax.nn.softmax((xt @ w_router + b_router).astype(jnp.float32), axis=-1)
    cap = max(1, EC_FACTOR * T // N_EXPERTS)
    g, idx = lax.top_k(aff.T, cap)
    xe = xt[idx]
    hdn = jax.nn.silu(jnp.einsum('ecd,edf->ecf', xe, w_e_gate)) * jnp.einsum('ecd,edf->ecf', xe, w_e_up)
    ye = jnp.einsum('ecf,efd->ecd', hdn, w_e_down) * g[..., None].astype(xn.dtype)
    y = jnp.zeros_like(xt).at[idx.reshape(-1)].add(ye.reshape(-1, D))
    return y.reshape(B, S, D)


def trunk(x, mem, norm_mix_g, w_in, b_gates, conv_w, conv_b, mh_norm_g, g_cq, g_ckv, w_uq, w_ukv,
          w_br_a, w_br_b, w_out, norm_x_g, norm_mem_g, w_xq, w_xk, w_xv, w_xo, norm_ffn_g,
          w_router, b_router, w_e_gate, w_e_up, w_e_down, final_norm_g):
    for l in range(DEPTH):
        x = x + token_mixer(rmsnorm(x, norm_mix_g[l]), w_in[l], b_gates[l], conv_w[l], conv_b[l],
                            mh_norm_g[l], g_cq[l], g_ckv[l], w_uq[l], w_ukv[l], w_br_a[l], w_br_b[l], w_out[l])
        x = x + cross_attention(rmsnorm(x, norm_x_g[l]), rmsnorm(mem, norm_mem_g[l]),
                                w_xq[l], w_xk[l], w_xv[l], w_xo[l])
        x = x + ec_moe(rmsnorm(x, norm_ffn_g[l]), w_router[l], b_router[l],
                       w_e_gate[l], w_e_up[l], w_e_down[l])
    return rmsnorm(x, final_norm_g)


def setup_inputs(seed: int = 0) -> dict:
    key = jax.random.key(seed)
    ks = jax.random.split(key, 40)
    f32 = jnp.float32

    def nrm(k, shape, fan_in):
        return jax.random.normal(k, shape, f32) * (fan_in ** -0.5)

    def gain(k, shape):
        return 1.0 + 0.02 * jax.random.normal(k, shape, f32)

    gk = jax.random.split(ks[6], 4)
    b_gates = jnp.concatenate([
        0.1 * jax.random.normal(gk[0], (DEPTH, 1, M_HEADS), f32),
        3.0 + 3.0 * jax.random.uniform(gk[1], (DEPTH, 1, M_HEADS), f32),
        0.1 * jax.random.normal(gk[2], (DEPTH, 1, M_HEADS), f32),
        3.0 + 3.0 * jax.random.uniform(gk[3], (DEPTH, 1, M_HEADS), f32),
    ], axis=1).reshape(DEPTH, 4 * M_HEADS)

    return {
        'x_prompt': jax.random.normal(ks[0], (BATCH, SEQ, D_MODEL), f32),
        'x_sample': jax.random.normal(ks[1], (DEC_BATCH, DEC_SEQ, D_MODEL), f32),
        'mem_prompt': jax.random.normal(ks[2], (BATCH, N_MEM, D_MODEL), f32),
        'mem_sample': jax.random.normal(ks[3], (DEC_BATCH, N_MEM, D_MODEL), f32),
        'norm_mix_g': gain(ks[4], (DEPTH, D_MODEL)),
        'w_in': nrm(ks[5], (DEPTH, D_MODEL, IN_WIDTH), D_MODEL),
        'b_gates': b_gates,
        'conv_w': nrm(ks[7], (DEPTH, M_CONV, 2 * M_WIDTH), M_CONV),
        'conv_b': 0.02 * jax.random.normal(ks[8], (DEPTH, 2 * M_WIDTH), f32),
        'mh_norm_g': gain(ks[9], (DEPTH, M_WIDTH)),
        'g_cq': gain(ks[10], (DEPTH, A_Q_RANK)),
        'g_ckv': gain(ks[11], (DEPTH, A_KV_RANK)),
        'w_uq': nrm(ks[12], (DEPTH, A_Q_RANK, A_HEADS * (A_NOPE + A_ROPE)), A_Q_RANK),
        'w_ukv': nrm(ks[13], (DEPTH, A_KV_RANK, A_HEADS * (A_NOPE + A_V)), A_KV_RANK),
        'w_br_a': nrm(ks[14], (DEPTH, M_WIDTH, D_MODEL), M_WIDTH),
        'w_br_b': nrm(ks[15], (DEPTH, A_HEADS * A_V, D_MODEL), A_HEADS * A_V),
        'w_out': nrm(ks[16], (DEPTH, D_MODEL, D_MODEL), D_MODEL),
        'norm_x_g': gain(ks[17], (DEPTH, D_MODEL)),
        'norm_mem_g': gain(ks[18], (DEPTH, D_MODEL)),
        'w_xq': nrm(ks[19], (DEPTH, D_MODEL, X_HEADS * X_HEAD_DIM), D_MODEL),
        'w_xk': nrm(ks[20], (DEPTH, D_MODEL, X_HEADS * X_HEAD_DIM), D_MODEL),
        'w_xv': nrm(ks[21], (DEPTH, D_MODEL, X_HEADS * X_HEAD_DIM), D_MODEL),
        'w_xo': nrm(ks[22], (DEPTH, X_HEADS * X_HEAD_DIM, D_MODEL), X_HEADS * X_HEAD_DIM),
        'norm_ffn_g': gain(ks[23], (DEPTH, D_MODEL)),
        'w_router': nrm(ks[24], (DEPTH, D_MODEL, N_EXPERTS), D_MODEL),
        'b_router': 0.01 * jax.random.normal(ks[25], (DEPTH, N_EXPERTS), f32),
        'w_e_gate': nrm(ks[26], (DEPTH, N_EXPERTS, D_MODEL, E_FF), D_MODEL),
        'w_e_up': nrm(ks[27], (DEPTH, N_EXPERTS, D_MODEL, E_FF), D_MODEL),
        'w_e_down': nrm(ks[28], (DEPTH, N_EXPERTS, E_FF, D_MODEL), E_FF),
        'final_norm_g': gain(ks[29], (D_MODEL,)),
    }


def reference(x_prompt, x_sample, mem_prompt, mem_sample, norm_mix_g, w_in, b_gates, conv_w, conv_b,
              mh_norm_g, g_cq, g_ckv, w_uq, w_ukv, w_br_a, w_br_b, w_out, norm_x_g, norm_mem_g,
              w_xq, w_xk, w_xv, w_xo, norm_ffn_g, w_router, b_router, w_e_gate, w_e_up, w_e_down,
              final_norm_g):
    weights = (norm_mix_g, w_in, b_gates, conv_w, conv_b, mh_norm_g, g_cq, g_ckv, w_uq, w_ukv,
               w_br_a, w_br_b, w_out, norm_x_g, norm_mem_g, w_xq, w_xk, w_xv, w_xo, norm_ffn_g,
               w_router, b_router, w_e_gate, w_e_up, w_e_down, final_norm_g)
    y_prompt = trunk(x_prompt, mem_prompt, *weights)
    y_sample = trunk(x_sample, mem_sample, *weights)
    return (y_prompt, y_sample)
```

```python
import functools

import jax
import jax.numpy as jnp
from jax import lax
from jax.experimental import pallas as pl
from jax.experimental.pallas import tpu as pltpu

F32 = jnp.float32
BF16 = jnp.bfloat16
I32 = jnp.int32

D_MODEL = 1024
M_WIDTH = 1024
M_HEADS = 4
M_HEAD_DIM = M_WIDTH // M_HEADS
M_CHUNK = 128
M_CONV = 5
A_HEADS = 8
A_NOPE = 128
A_ROPE = 64
A_V = 128
A_Q_RANK = 256
A_KV_RANK = 256
A_QK_PAD = 256
ROPE_BASE = 10000.0
X_HEADS = 4
X_HEAD_DIM = D_MODEL // X_HEADS
N_EXPERTS = 16
EC_FACTOR = 2
E_FF = 1024
NORM_EPS = 1e-6
LANES = 128
BIG_COLS = 6 * 1024
VMEM_LIMIT = 56 * 1024 * 1024
MOE_TOKEN_TILE = 1024
MOE_SLOT_TILE = 256


def _cparams(sem):
    return pltpu.CompilerParams(dimension_semantics=sem, vmem_limit_bytes=VMEM_LIMIT)


def _dot(a, b):
    return jnp.dot(a, b, preferred_element_type=F32)


def _dot_nt(a, b):
    return lax.dot_general(a, b, (((1,), (1,)), ((), ())), preferred_element_type=F32)


def _dot_tn(a, b):
    return lax.dot_general(a, b, (((0,), (0,)), ((), ())), preferred_element_type=F32)


def _rms(x, g):
    return x * lax.rsqrt(jnp.mean(x * x, axis=-1, keepdims=True) + NORM_EPS) * g


def _sigmoid(x):
    return 1.0 / (1.0 + jnp.exp(-x))


def _split3(x):
    hi = x.astype(BF16)
    r = x - hi.astype(F32)
    mid = r.astype(BF16)
    lo = (r - mid.astype(F32)).astype(BF16)
    return hi, mid, lo


def _inproj_kernel(x_ref, g_ref, wbig_ref, wc_ref, wkr_ref, wg_ref, wgt_ref, bg_ref, bgt_ref,
                   cos_ref, sin_ref,
                   z_ref, c_ref, kr_ref, gate_ref, gatet_ref, xn_scr):
    j = pl.program_id(1)

    @pl.when(j == 0)
    def _():
        xn = _rms(x_ref[...], g_ref[...]).astype(BF16)
        xn_scr[...] = xn
        c_ref[...] = _dot(xn, wc_ref[...]).astype(BF16)
        kr = _dot(xn, wkr_ref[...])
        kr_ref[...] = (kr[:, :LANES] * cos_ref[...] + kr[:, LANES:] * sin_ref[...]).astype(BF16)
        gate_ref[...] = _dot(xn, wg_ref[...]) + bg_ref[...]
        gatet_ref[...] = _dot_nt(wgt_ref[...], xn) + bgt_ref[...]

    z_ref[...] = _dot(xn_scr[...], wbig_ref[...]).astype(BF16)


def _inproj(x, S, w):
    T = x.shape[0]
    tm = min(1024, S)
    tn = 1024
    nseq = S // tm
    row = lambda i, j: (i, 0)
    const = lambda i, j: (0, 0)
    return pl.pallas_call(
        _inproj_kernel,
        grid=(T // tm, BIG_COLS // tn),
        in_specs=[
            pl.BlockSpec((tm, D_MODEL), row),
            pl.BlockSpec((1, D_MODEL), const),
            pl.BlockSpec((D_MODEL, tn), lambda i, j: (0, j)),
            pl.BlockSpec((D_MODEL, 512), const),
            pl.BlockSpec((D_MODEL, 256), const),
            pl.BlockSpec((D_MODEL, 16), const),
            pl.BlockSpec((16, D_MODEL), const),
            pl.BlockSpec((1, 16), const),
            pl.BlockSpec((16, 1), const),
            pl.BlockSpec((tm, LANES), lambda i, j: (i % nseq, 0)),
            pl.BlockSpec((tm, LANES), lambda i, j: (i % nseq, 0)),
        ],
        out_specs=[
            pl.BlockSpec((tm, tn), lambda i, j: (i, j)),
            pl.BlockSpec((tm, 512), row),
            pl.BlockSpec((tm, LANES), row),
            pl.BlockSpec((tm, 16), row),
            pl.BlockSpec((16, tm), lambda i, j: (0, i)),
        ],
        out_shape=[
            jax.ShapeDtypeStruct((T, BIG_COLS), BF16),
            jax.ShapeDtypeStruct((T, 512), BF16),
            jax.ShapeDtypeStruct((T, LANES), BF16),
            jax.ShapeDtypeStruct((T, 16), F32),
            jax.ShapeDtypeStruct((16, T), F32),
        ],
        scratch_shapes=[pltpu.VMEM((tm, D_MODEL), BF16)],
        compiler_params=_cparams(("parallel", "arbitrary")),
        name="inproj",
    )(x, w["norm_mix_g"], w["w_big"], w["w_c"], w["w_kr"], w["w_g"], w["w_gt"], w["b_g"], w["b_gt"],
      w["cos"], w["sin"])


CONV_HALO = 16


def _conv_kernel(z_ref, zp_ref, zn_ref, w_ref, b_ref, o_ref, scr, *, tr, tiles_per_seq, q_tiles):
    i = pl.program_id(0)
    j = pl.program_id(1)
    it = i % tiles_per_seq
    keep_prev = jnp.where(it == 0, 0.0, 1.0)
    keep_next = jnp.where(it == tiles_per_seq - 1, 0.0, 1.0)
    scr[0:8, :] = zp_ref[...].astype(F32)[8:16, :] * keep_prev
    scr[8:8 + tr, :] = z_ref[...].astype(F32)
    scr[8 + tr:16 + tr, :] = zn_ref[...].astype(F32)[0:8, :] * keep_next
    acc = jnp.zeros(o_ref.shape, F32) + b_ref[...]
    for k in range(M_CONV):
        off = 8 - M_CONV // 2 + k
        acc = acc + w_ref[k:k + 1, :] * scr[off:off + tr, :]
    y = acc * _sigmoid(acc)
    scale = jnp.where(j < q_tiles, M_HEAD_DIM ** -0.5, 1.0)
    o_ref[...] = (y * scale).astype(BF16)


def _conv(z, S, w):
    T = z.shape[0]
    tr = min(512, S)
    tcw = 512
    tiles_per_seq = S // tr
    hb = tr // CONV_HALO
    nhalo = T // CONV_HALO
    kern = functools.partial(_conv_kernel, tr=tr, tiles_per_seq=tiles_per_seq, q_tiles=M_WIDTH // tcw)
    return pl.pallas_call(
        kern,
        grid=(T // tr, 2 * M_WIDTH // tcw),
        in_specs=[
            pl.BlockSpec((tr, tcw), lambda i, j: (i, j)),
            pl.BlockSpec((CONV_HALO, tcw), lambda i, j: (jnp.maximum(i * hb - 1, 0), j)),
            pl.BlockSpec((CONV_HALO, tcw), lambda i, j: (jnp.minimum((i + 1) * hb, nhalo - 1), j)),
            pl.BlockSpec((M_CONV, tcw), lambda i, j: (0, j)),
            pl.BlockSpec((1, tcw), lambda i, j: (0, j)),
        ],
        out_specs=pl.BlockSpec((tr, tcw), lambda i, j: (i, j)),
        out_shape=jax.ShapeDtypeStruct((T, 2 * M_WIDTH), BF16),
        scratch_shapes=[pltpu.VMEM((tr + 16, tcw), F32)],
        compiler_params=_cparams(("parallel", "parallel")),
        name="conv_silu",
    )(z, z, z, w["conv_w"], w["conv_b"])


def _log_sigmoid(x):
    return -(jnp.maximum(-x, 0.0) + jnp.log1p(jnp.exp(-jnp.abs(x))))


def _mlstm_dir(d, q_ref, k_ref, v_ref, g_ref, gt_ref, o_ref, c_scr, n_scr, m_scr):
    L = M_CHUNK
    r = lax.broadcasted_iota(I32, (L, L), 0)
    c = lax.broadcasted_iota(I32, (L, L), 1)
    if d == 0:
        mask = c <= r
    else:
        mask = c >= r
    tri_col = jnp.where(mask, 1.0, 0.0).astype(BF16)
    tri_row = jnp.where(r <= c if d == 0 else r >= c, 1.0, 0.0).astype(BF16)
    g = g_ref[...]
    gt = gt_ref[...]
    lf = _log_sigmoid(g)
    lft = _log_sigmoid(gt)
    b_col_all = sum(_dot(tri_col, p) for p in _split3(lf))
    b_row_all = sum(_dot(p, tri_row) for p in _split3(lft))
    edge = L - 1 if d == 0 else 0
    for h in range(M_HEADS):
        idx = d * M_HEADS + h
        ci = d * 2 * M_HEADS + h
        cf = ci + M_HEADS
        i_col = g[:, ci:ci + 1]
        i_row = gt[ci:ci + 1, :]
        b_col = b_col_all[:, cf:cf + 1]
        b_row = b_row_all[cf:cf + 1, :]
        b_last = b_col[edge:edge + 1, :]
        m_prev = m_scr[idx][0:1, 0:1]
        a = b_col + m_prev
        dm = jnp.where(mask, b_col - b_row + i_row, -jnp.inf)
        m_t = jnp.maximum(a, jnp.max(dm, axis=1, keepdims=True))
        w_inter = jnp.exp(a - m_t)
        p = jnp.exp(dm - m_t)
        sl = slice(h * M_HEAD_DIM, (h + 1) * M_HEAD_DIM)
        q = q_ref[:, sl]
        k = k_ref[:, sl]
        v = v_ref[:, sl]
        s = _dot_nt(q, k) * p
        cst = c_scr[idx]
        nst = n_scr[idx]
        num = _dot(s.astype(BF16), v) + w_inter * _dot(q, cst.astype(BF16))
        qn = jnp.sum(q.astype(F32) * nst, axis=1, keepdims=True)
        den = jnp.sum(s, axis=1, keepdims=True) + w_inter * qn
        o_ref[:, sl] = num / jnp.maximum(jnp.abs(den), jnp.exp(-m_t))
        gk = b_last - b_col + i_col
        m_new = jnp.maximum(b_last + m_prev, jnp.max(gk, axis=0, keepdims=True))
        decay = jnp.exp(b_last + m_prev - m_new)
        wk = jnp.exp(gk - m_new)
        wv = (wk * v.astype(F32)).astype(BF16)
        c_scr[idx] = decay * cst + _dot_tn(k, wv)
        n_scr[idx] = decay * nst + jnp.sum(wk * k.astype(F32), axis=0, keepdims=True)
        m_scr[idx] = jnp.broadcast_to(m_new, (8, LANES))


def _mlstm_kernel(qf, kf, vf, gf, gtf, qb, kb, vb, gb, gtb, of, ob, c_scr, n_scr, m_scr):
    @pl.when(pl.program_id(1) == 0)
    def _():
        c_scr[...] = jnp.zeros(c_scr.shape, F32)
        n_scr[...] = jnp.zeros(n_scr.shape, F32)
        m_scr[...] = jnp.zeros(m_scr.shape, F32)

    _mlstm_dir(0, qf, kf, vf, gf, gtf, of, c_scr, n_scr, m_scr)
    _mlstm_dir(1, qb, kb, vb, gb, gtb, ob, c_scr, n_scr, m_scr)


def _mlstm(qk, z, gate, gatet, B, S):
    T = B * S
    L = M_CHUNK
    nc = S // L
    fwd = lambda b, c: b * nc + c
    bwd = lambda b, c: b * nc + nc - 1 - c

    def specs(pos):
        return [
            pl.BlockSpec((L, M_WIDTH), lambda b, c: (pos(b, c), 0)),
            pl.BlockSpec((L, M_WIDTH), lambda b, c: (pos(b, c), 1)),
            pl.BlockSpec((L, M_WIDTH), lambda b, c: (pos(b, c), 2)),
            pl.BlockSpec((L, 16), lambda b, c: (pos(b, c), 0)),
            pl.BlockSpec((16, L), lambda b, c: (0, pos(b, c))),
        ]

    nstate = 2 * M_HEADS
    return pl.pallas_call(
        _mlstm_kernel,
        grid=(B, nc),
        in_specs=specs(fwd) + specs(bwd),
        out_specs=[
            pl.BlockSpec((L, M_WIDTH), lambda b, c: (fwd(b, c), 0)),
            pl.BlockSpec((L, M_WIDTH), lambda b, c: (bwd(b, c), 0)),
        ],
        out_shape=[jax.ShapeDtypeStruct((T, M_WIDTH), F32)] * 2,
        scratch_shapes=[
            pltpu.VMEM((nstate, M_HEAD_DIM, M_HEAD_DIM), F32),
            pltpu.VMEM((nstate, 1, M_HEAD_DIM), F32),
            pltpu.VMEM((nstate, 8, LANES), F32),
        ],
        compiler_params=_cparams(("parallel", "arbitrary")),
        name="mlstm",
    )(qk, qk, z, gate, gatet, qk, qk, z, gate, gatet)


def _mla_proj_kernel(c_ref, kr_ref, cos_ref, sin_ref, gq_ref, gkv_ref, wqa_ref, wqr_ref, wkv_ref,
                     q_ref, k_ref, v_ref):
    cq = c_ref[:, :A_Q_RANK].astype(F32)
    ckv = c_ref[:, A_Q_RANK:].astype(F32)
    cqn = _rms(cq, gq_ref[...]).astype(BF16)
    ckvn = _rms(ckv, gkv_ref[...]).astype(BF16)
    qa = _dot(cqn, wqa_ref[...])
    qr = _dot(cqn, wqr_ref[...])
    kv = _dot(ckvn, wkv_ref[...])
    cos = cos_ref[...]
    sin = sin_ref[...]
    kr = kr_ref[...]
    for h in range(A_HEADS):
        o = h * A_QK_PAD
        q_ref[0, h, :, :LANES] = qa[:, o:o + LANES].astype(BF16)
        q_ref[0, h, :, LANES:] = (qa[:, o + LANES:o + 2 * LANES] * cos
                                  + qr[:, h * LANES:(h + 1) * LANES] * sin).astype(BF16)
        k_ref[0, h, :, :LANES] = kv[:, o:o + LANES].astype(BF16)
        k_ref[0, h, :, LANES:] = kr
        v_ref[0, h] = kv[:, o + LANES:o + 2 * LANES].astype(BF16)


def _mla_proj(c, kr, B, S, w):
    tm = min(512, S)
    nseq = S // tm
    row = lambda i: (i, 0)
    const = lambda i: (0, 0)
    seq = lambda i: (i % nseq, 0)
    head_blk = lambda i: (i // nseq, 0, i % nseq, 0)
    return pl.pallas_call(
        _mla_proj_kernel,
        grid=(B * S // tm,),
        in_specs=[
            pl.BlockSpec((tm, 512), row),
            pl.BlockSpec((tm, LANES), row),
            pl.BlockSpec((tm, LANES), seq),
            pl.BlockSpec((tm, LANES), seq),
            pl.BlockSpec((1, A_Q_RANK), const),
            pl.BlockSpec((1, A_KV_RANK), const),
            pl.BlockSpec((A_Q_RANK, A_HEADS * A_QK_PAD), const),
            pl.BlockSpec((A_Q_RANK, A_HEADS * LANES), const),
            pl.BlockSpec((A_KV_RANK, A_HEADS * (A_NOPE + A_V)), const),
        ],
        out_specs=[
            pl.BlockSpec((1, A_HEADS, tm, A_QK_PAD), head_blk),
            pl.BlockSpec((1, A_HEADS, tm, A_QK_PAD), head_blk),
            pl.BlockSpec((1, A_HEADS, tm, A_V), head_blk),
        ],
        out_shape=[
            jax.ShapeDtypeStruct((B, A_HEADS, S, A_QK_PAD), BF16),
            jax.ShapeDtypeStruct((B, A_HEADS, S, A_QK_PAD), BF16),
            jax.ShapeDtypeStruct((B, A_HEADS, S, A_V), BF16),
        ],
        compiler_params=_cparams(("parallel",)),
        name="mla_proj",
    )(c, kr, w["cos"], w["sin"], w["g_cq"], w["g_ckv"], w["w_qa"], w["w_qr"], w["w_ukv"])


def _flash_kernel(q_ref, k_ref, v_ref, o_ref, *, tk, nk):
    q = q_ref[0, 0]
    tq = q.shape[0]
    scale = (A_NOPE + A_ROPE) ** -0.5

    def body(j, carry):
        m, l, acc = carry
        start = pl.multiple_of(j * tk, tk)
        k = k_ref[0, 0, pl.ds(start, tk), :]
        v = v_ref[0, 0, pl.ds(start, tk), :]
        s = _dot_nt(q, k)
        m_new = jnp.maximum(m, jnp.max(s, axis=1, keepdims=True))
        p = jnp.exp((s - m_new) * scale)
        alpha = jnp.exp((m - m_new) * scale)
        l = alpha * l + jnp.sum(p, axis=1, keepdims=True)
        acc = alpha * acc + _dot(p.astype(BF16), v)
        return m_new, l, acc

    init = (jnp.full((tq, 1), -jnp.inf, F32), jnp.zeros((tq, 1), F32), jnp.zeros((tq, A_V), F32))
    _, l, acc = lax.fori_loop(0, nk, body, init)
    o_ref[...] = (acc / l).astype(BF16)


def _flash(q, k, v):
    B, H, S, _ = q.shape
    tq = min(512, S)
    tk = min(512, S)
    nq = S // tq
    kern = functools.partial(_flash_kernel, tk=tk, nk=S // tk)
    return pl.pallas_call(
        kern,
        grid=(B, H, nq),
        in_specs=[
            pl.BlockSpec((1, 1, tq, A_QK_PAD), lambda b, h, i: (b, h, i, 0)),
            pl.BlockSpec((1, 1, S, A_QK_PAD), lambda b, h, i: (b, h, 0, 0)),
            pl.BlockSpec((1, 1, S, A_V), lambda b, h, i: (b, h, 0, 0)),
        ],
        out_specs=pl.BlockSpec((tq, A_V), lambda b, h, i: (b * nq + i, h)),
        out_shape=jax.ShapeDtypeStruct((B * S, A_HEADS * A_V), BF16),
        compiler_params=_cparams(("parallel", "parallel", "arbitrary")),
        name="mla_flash",
    )(q, k, v)


def _mixer_out_kernel(hf_ref, hb_ref, om_ref, ga_ref, gb_ref, att_ref, x_ref, mg_ref,
                      wa_ref, wb_ref, wo_ref, o_ref):
    hs = hf_ref[...] + hb_ref[...]
    mg = mg_ref[...]
    parts = []
    for h in range(M_HEADS):
        sl = slice(h * M_HEAD_DIM, (h + 1) * M_HEAD_DIM)
        parts.append(_rms(hs[:, sl], mg[:, sl]))
    hn = jnp.concatenate(parts, axis=1) * _sigmoid(om_ref[...].astype(F32))
    y_a = _dot(hn.astype(BF16), wa_ref[...])
    y_b = _dot(att_ref[...], wb_ref[...])
    merged = _sigmoid(ga_ref[...].astype(F32)) * y_a + _sigmoid(gb_ref[...].astype(F32)) * y_b
    o_ref[...] = x_ref[...] + _dot(merged.astype(BF16), wo_ref[...])


def _mixer_out(hf, hb, z, att, x, w):
    T = x.shape[0]
    tm = min(512, T)
    row = lambda i: (i, 0)
    const = lambda i: (0, 0)
    wspec = pl.BlockSpec((D_MODEL, D_MODEL), const)
    return pl.pallas_call(
        _mixer_out_kernel,
        grid=(T // tm,),
        in_specs=[
            pl.BlockSpec((tm, M_WIDTH), row),
            pl.BlockSpec((tm, M_WIDTH), row),
            pl.BlockSpec((tm, M_WIDTH), lambda i: (i, 3)),
            pl.BlockSpec((tm, D_MODEL), lambda i: (i, 4)),
            pl.BlockSpec((tm, D_MODEL), lambda i: (i, 5)),
            pl.BlockSpec((tm, D_MODEL), row),
            pl.BlockSpec((tm, D_MODEL), row),
            pl.BlockSpec((1, M_WIDTH), const),
            wspec, wspec, wspec,
        ],
        out_specs=pl.BlockSpec((tm, D_MODEL), row),
        out_shape=jax.ShapeDtypeStruct((T, D_MODEL), F32),
        compiler_params=_cparams(("parallel",)),
        name="mixer_out",
    )(hf, hb, z, z, z, att, x, w["mh_norm_g"], w["w_br_a"], w["w_br_b"], w["w_out"])


def _mem_kernel(m_ref, g_ref, wk_ref, wv_ref, k_ref, v_ref):
    mn = _rms(m_ref[...], g_ref[...]).astype(BF16)
    k_ref[...] = _dot(mn, wk_ref[...]).astype(BF16)
    v_ref[...] = _dot(mn, wv_ref[...]).astype(BF16)


def _mem_proj(mem, w):
    R = mem.shape[0]
    tm = 256
    row = lambda i: (i, 0)
    const = lambda i: (0, 0)
    wspec = pl.BlockSpec((D_MODEL, D_MODEL), const)
    return pl.pallas_call(
        _mem_kernel,
        grid=(R // tm,),
        in_specs=[pl.BlockSpec((tm, D_MODEL), row), pl.BlockSpec((1, D_MODEL), const), wspec, wspec],
        out_specs=[pl.BlockSpec((tm, D_MODEL), row)] * 2,
        out_shape=[jax.ShapeDtypeStruct((R, D_MODEL), BF16)] * 2,
        compiler_params=_cparams(("parallel",)),
        name="mem_proj",
    )(mem, w["norm_mem_g"], w["w_xk"], w["w_xv"])


def _cross_router_kernel(x_ref, kx_ref, vx_ref, gx_ref, gf_ref, wq_ref, wo_ref, wr_ref, wrt_ref,
                         br_ref, brt_ref, x2_ref, xn_ref, aff_ref, afft_ref):
    x1 = x_ref[...]
    xn = _rms(x1, gx_ref[...]).astype(BF16)
    q = _dot(xn, wq_ref[...])
    outs = []
    for h in range(X_HEADS):
        sl = slice(h * X_HEAD_DIM, (h + 1) * X_HEAD_DIM)
        s = _dot_nt(q[:, sl].astype(BF16), kx_ref[:, sl]) * (X_HEAD_DIM ** -0.5)
        e = jnp.exp(s - jnp.max(s, axis=1, keepdims=True))
        p = e / jnp.sum(e, axis=1, keepdims=True)
        outs.append(_dot(p.astype(BF16), vx_ref[:, sl]))
    o = jnp.concatenate(outs, axis=1)
    x2 = x1 + _dot(o.astype(BF16), wo_ref[...])
    x2_ref[...] = x2
    xf = _rms(x2, gf_ref[...])
    hi = xf.astype(BF16)
    xn_ref[...] = hi
    lo = (xf - hi.astype(F32)).astype(BF16)
    wr = wr_ref[...]
    wr_hi = wr.astype(BF16)
    wr_lo = (wr - wr_hi.astype(F32)).astype(BF16)
    logit = _dot(hi, wr_hi) + _dot(lo, wr_hi) + _dot(hi, wr_lo) + br_ref[...]
    e = jnp.exp(logit - jnp.max(logit, axis=1, keepdims=True))
    aff_ref[...] = e / jnp.sum(e, axis=1, keepdims=True)
    wrt = wrt_ref[...]
    wrt_hi = wrt.astype(BF16)
    wrt_lo = (wrt - wrt_hi.astype(F32)).astype(BF16)
    logit_t = _dot_nt(wrt_hi, hi) + _dot_nt(wrt_hi, lo) + _dot_nt(wrt_lo, hi) + brt_ref[...]
    et = jnp.exp(logit_t - jnp.max(logit_t, axis=0, keepdims=True))
    afft_ref[...] = et / jnp.sum(et, axis=0, keepdims=True)


def _cross_router(x1, kx, vx, S, n_mem, w):
    T = x1.shape[0]
    tm = min(512, S)
    nseq = S // tm
    row = lambda i: (i, 0)
    const = lambda i: (0, 0)
    wspec = pl.BlockSpec((D_MODEL, D_MODEL), const)
    memspec = pl.BlockSpec((n_mem, D_MODEL), lambda i: (i // nseq, 0))
    return pl.pallas_call(
        _cross_router_kernel,
        grid=(T // tm,),
        in_specs=[
            pl.BlockSpec((tm, D_MODEL), row), memspec, memspec,
            pl.BlockSpec((1, D_MODEL), const), pl.BlockSpec((1, D_MODEL), const),
            wspec, wspec,
            pl.BlockSpec((D_MODEL, N_EXPERTS), const), pl.BlockSpec((N_EXPERTS, D_MODEL), const),
            pl.BlockSpec((1, N_EXPERTS), const), pl.BlockSpec((N_EXPERTS, 1), const),
        ],
        out_specs=[
            pl.BlockSpec((tm, D_MODEL), row),
            pl.BlockSpec((tm, D_MODEL), row),
            pl.BlockSpec((tm, N_EXPERTS), row),
            pl.BlockSpec((N_EXPERTS, tm), lambda i: (0, i)),
        ],
        out_shape=[
            jax.ShapeDtypeStruct((T, D_MODEL), F32),
            jax.ShapeDtypeStruct((T, D_MODEL), BF16),
            jax.ShapeDtypeStruct((T, N_EXPERTS), F32),
            jax.ShapeDtypeStruct((N_EXPERTS, T), F32),
        ],
        compiler_params=_cparams(("parallel",)),
        name="cross_router",
    )(x1, kx, vx, w["norm_x_g"], w["norm_ffn_g"], w["w_xq"], w["w_xo"], w["w_router"], w["w_router_t"],
      w["b_router"], w["b_router_t"])


def _excl_cumsum(mask_f, strict_lane, strict_blk):
    nb = mask_f.shape[0]
    within = _dot(mask_f.astype(BF16), strict_lane)
    tot = jnp.sum(mask_f, axis=1, keepdims=True)
    bstart = _dot(strict_blk, jnp.broadcast_to(tot, (nb, LANES)).astype(BF16))
    return within + bstart, bstart


def _select_kernel(aff_ref, pos_ref, bst_ref, *, cap):
    a = aff_ref[0]
    nb = a.shape[0]
    bits = pltpu.bitcast(a, I32)

    def radix(i, prefix):
        cand = prefix | jnp.left_shift(jnp.int32(1), 30 - i)
        cnt = jnp.sum(jnp.where(bits >= cand, 1.0, 0.0), axis=(0, 1), keepdims=True)
        return jnp.where(cnt >= cap, cand, prefix)

    thr = lax.fori_loop(0, 31, radix, jnp.zeros((1, 1), I32))
    gt = bits > thr
    eq = bits == thr
    need = cap - jnp.sum(jnp.where(gt, 1.0, 0.0), axis=(0, 1), keepdims=True)
    li = lax.broadcasted_iota(I32, (LANES, LANES), 0)
    lj = lax.broadcasted_iota(I32, (LANES, LANES), 1)
    strict_lane = jnp.where(li < lj, 1.0, 0.0).astype(BF16)
    bi = lax.broadcasted_iota(I32, (nb, nb), 0)
    bj = lax.broadcasted_iota(I32, (nb, nb), 1)
    strict_blk = jnp.where(bj < bi, 1.0, 0.0).astype(BF16)
    rank, _ = _excl_cumsum(jnp.where(eq, 1.0, 0.0), strict_lane, strict_blk)
    sel = gt | (eq & (rank < need))
    pos, bstart = _excl_cumsum(jnp.where(sel, 1.0, 0.0), strict_lane, strict_blk)
    pos_ref[0] = jnp.where(sel, pos.astype(I32), -1)
    col = jnp.broadcast_to(bstart[:, 0:1], (nb, nb))
    bst_ref[0] = jnp.sum(jnp.where(bi == bj, col, 0.0), axis=0, keepdims=True).astype(I32)


def _select(aff_t, cap):
    E, T = aff_t.shape
    nb = T // LANES
    kern = functools.partial(_select_kernel, cap=cap)
    return pl.pallas_call(
        kern,
        grid=(E,),
        in_specs=[pl.BlockSpec((1, nb, LANES), lambda e: (e, 0, 0))],
        out_specs=[
            pl.BlockSpec((1, nb, LANES), lambda e: (e, 0, 0)),
            pl.BlockSpec((1, 1, nb), lambda e: (e, 0, 0)),
        ],
        out_shape=[
            jax.ShapeDtypeStruct((E, nb, LANES), I32),
            jax.ShapeDtypeStruct((E, 1, nb), I32),
        ],
        compiler_params=_cparams(("parallel",)),
        name="ec_select",
    )(aff_t.reshape(E, nb, LANES))


def _moe_tables(bst, cap, T, tu, tc):
    E = bst.shape[0]
    nu = T // tu
    ns = cap // tc
    start = bst[:, ::tu // LANES]
    end = jnp.concatenate([start[:, 1:], jnp.full((E, 1), cap, I32)], axis=1)
    cnt = end - start
    s_lo = jnp.minimum(start // tc, ns - 1)
    s_hi = jnp.where(cnt > 0, (end - 1) // tc, s_lo)
    ncell = jnp.where(cnt > 0, s_hi - s_lo + 1, 0)

    def enumerate_cells(nc_flat, slo_flat, steps):
        off_end = jnp.cumsum(nc_flat, axis=-1)
        off = off_end - nc_flat
        total = off_end[..., -1:]
        k = jnp.arange(steps, dtype=I32)
        kk = jnp.minimum(k, total - 1)
        grp = jnp.sum((off_end[..., None, :] <= kk[..., :, None]).astype(I32), axis=-1)
        s = jnp.take_along_axis(slo_flat, grp, axis=-1) + kk - jnp.take_along_axis(off, grp, axis=-1)
        valid = (k < total).astype(I32)
        return grp, s, valid

    pg = nu + ns
    g_u, g_s, g_valid = enumerate_cells(ncell, s_lo, pg)
    prev_s = jnp.concatenate([jnp.full((E, 1), -1, I32), g_s[:, :-1]], axis=1)
    g_first = g_valid * (g_s != prev_s).astype(I32)
    gather_tabs = tuple(t.reshape(-1) for t in (g_u, g_s, g_valid, g_first))

    ncell_c = ncell.at[0].set(jnp.maximum(ncell[0], 1))
    pc = E * (nu + ns)
    c_grp, c_s, c_valid = enumerate_cells(ncell_c.T.reshape(-1), s_lo.T.reshape(-1), pc)
    c_u = c_grp // E
    c_e = c_grp % E
    prev_u = jnp.concatenate([jnp.full((1,), -1, I32), c_u[:-1]])
    next_u = jnp.concatenate([c_u[1:], jnp.full((1,), -1, I32)])
    next_valid = jnp.concatenate([c_valid[1:], jnp.zeros((1,), I32)])
    c_first = c_valid * (c_u != prev_u).astype(I32)
    c_last = c_valid * jnp.maximum((c_u != next_u).astype(I32), 1 - next_valid)
    combine_tabs = (c_u, c_e, c_s, c_valid, c_first, c_last)
    return gather_tabs, combine_tabs


def _gather_kernel(u_tab, s_tab, valid_tab, first_tab, pos_ref, x_ref, o_ref, *, steps, tc):
    step = pl.program_id(0) * steps + pl.program_id(1)
    s = s_tab[step]
    tu = pos_ref.shape[-1]
    slot = lax.broadcasted_iota(I32, (tc, tu), 0) + s * tc
    onehot = jnp.where(pos_ref[0] == slot, 1.0, 0.0).astype(BF16)

    @pl.when(first_tab[step] == 1)
    def _():
        o_ref[0] = _dot(onehot, x_ref[...]).astype(BF16)

    @pl.when((valid_tab[step] == 1) & (first_tab[step] == 0))
    def _():
        o_ref[0] = o_ref[0] + _dot(onehot, x_ref[...]).astype(BF16)


def _moe_gather(xn, pos_row, tabs, cap, tu, tc):
    T = xn.shape[0]
    E = N_EXPERTS
    steps = T // tu + cap // tc
    kern = functools.partial(_gather_kernel, steps=steps, tc=tc)
    grid_spec = pltpu.PrefetchScalarGridSpec(
        num_scalar_prefetch=4,
        grid=(E, steps),
        in_specs=[
            pl.BlockSpec((1, 1, tu), lambda e, k, u, s, v, f: (e, 0, u[e * steps + k])),
            pl.BlockSpec((tu, D_MODEL), lambda e, k, u, s, v, f: (u[e * steps + k], 0)),
        ],
        out_specs=pl.BlockSpec((1, tc, D_MODEL), lambda e, k, u, s, v, f: (e, s[e * steps + k], 0)),
    )
    return pl.pallas_call(
        kern,
        grid_spec=grid_spec,
        out_shape=jax.ShapeDtypeStruct((E, cap, D_MODEL), BF16),
        compiler_params=_cparams(("parallel", "arbitrary")),
        name="moe_gather",
    )(*tabs, pos_row, xn)


def _ffn_kernel(x_ref, wg_ref, wu_ref, wd_ref, o_ref):
    x = x_ref[0]
    g = _dot(x, wg_ref[0])
    u = _dot(x, wu_ref[0])
    h = (g * _sigmoid(g) * u).astype(BF16)
    o_ref[0] = _dot(h, wd_ref[0]).astype(BF16)


def _moe_ffn(xe, w):
    E, cap, _ = xe.shape
    tf = min(512, cap)
    wspec = pl.BlockSpec((1, D_MODEL, E_FF), lambda e, i: (e, 0, 0))
    return pl.pallas_call(
        _ffn_kernel,
        grid=(E, cap // tf),
        in_specs=[pl.BlockSpec((1, tf, D_MODEL), lambda e, i: (e, i, 0)), wspec, wspec,
                  pl.BlockSpec((1, E_FF, D_MODEL), lambda e, i: (e, 0, 0))],
        out_specs=pl.BlockSpec((1, tf, D_MODEL), lambda e, i: (e, i, 0)),
        out_shape=jax.ShapeDtypeStruct((E, cap, D_MODEL), BF16),
        compiler_params=_cparams(("parallel", "arbitrary")),
        name="moe_ffn",
    )(xe, w["w_e_gate"], w["w_e_up"], w["w_e_down"])


def _combine_kernel(u_tab, e_tab, s_tab, valid_tab, first_tab, last_tab,
                    pos_ref, ye_ref, aff_ref, x_ref, g_ref, o_ref, acc, *, tc):
    step = pl.program_id(0)
    e = e_tab[step]
    s = s_tab[step]
    tu = pos_ref.shape[-1]

    @pl.when(first_tab[step] == 1)
    def _():
        acc[...] = jnp.zeros(acc.shape, F32)

    @pl.when(valid_tab[step] == 1)
    def _():
        slot = lax.broadcasted_iota(I32, (tc, tu), 0) + s * tc
        onehot = jnp.where(pos_ref[0] == slot, 1.0, 0.0).astype(BF16)
        contrib = _dot_tn(onehot, ye_ref[0])
        lane = lax.broadcasted_iota(I32, aff_ref.shape, 1)
        gate = jnp.sum(jnp.where(lane == e, aff_ref[...], 0.0), axis=1, keepdims=True)
        acc[...] = acc[...] + gate * contrib

    @pl.when(last_tab[step] == 1)
    def _():
        o_ref[...] = _rms(x_ref[...] + acc[...], g_ref[...])


def _moe_combine(ye, pos_row, aff, x2, tabs, final_g, cap, tu, tc):
    T = x2.shape[0]
    E = N_EXPERTS
    steps = E * (T // tu + cap // tc)
    kern = functools.partial(_combine_kernel, tc=tc)
    grid_spec = pltpu.PrefetchScalarGridSpec(
        num_scalar_prefetch=6,
        grid=(steps,),
        in_specs=[
            pl.BlockSpec((1, 1, tu), lambda k, u, e, s, v, f, l: (e[k], 0, u[k])),
            pl.BlockSpec((1, tc, D_MODEL), lambda k, u, e, s, v, f, l: (e[k], s[k], 0)),
            pl.BlockSpec((tu, N_EXPERTS), lambda k, u, e, s, v, f, l: (u[k], 0)),
            pl.BlockSpec((tu, D_MODEL), lambda k, u, e, s, v, f, l: (u[k], 0)),
            pl.BlockSpec((1, D_MODEL), lambda k, u, e, s, v, f, l: (0, 0)),
        ],
        out_specs=pl.BlockSpec((tu, D_MODEL), lambda k, u, e, s, v, f, l: (u[k], 0)),
        scratch_shapes=[pltpu.VMEM((tu, D_MODEL), F32)],
    )
    return pl.pallas_call(
        kern,
        grid_spec=grid_spec,
        out_shape=jax.ShapeDtypeStruct((T, D_MODEL), F32),
        compiler_params=_cparams(("arbitrary",)),
        name="moe_combine",
    )(*tabs, pos_row, ye, aff, x2, final_g)


def _rope_tables(S):
    pos = jnp.arange(S, dtype=F32)
    inv = ROPE_BASE ** (-jnp.arange(0, A_ROPE, 2, dtype=F32) / A_ROPE)
    ang = pos[:, None] * inv[None, :]
    pad = jnp.zeros((S, LANES - A_ROPE), F32)
    cos = jnp.concatenate([jnp.cos(ang), jnp.cos(ang), pad], axis=1)
    sin = jnp.concatenate([jnp.sin(ang), jnp.sin(ang), pad], axis=1)
    return cos, sin


def _rotate_half_cols(w):
    half = A_ROPE // 2
    return jnp.concatenate([-w[..., half:], w[..., :half]], axis=-1)


def _prep_weights(norm_mix_g, w_in, b_gates, conv_w, conv_b, mh_norm_g, g_cq, g_ckv, w_uq, w_ukv,
                  w_br_a, w_br_b, w_out, norm_x_g, norm_mem_g, w_xq, w_xk, w_xv, w_xo, norm_ffn_g,
                  w_router, b_router, w_e_gate, w_e_up, w_e_down, final_norm_g):
    l = 0
    wi = w_in[l]
    o = 0
    cols = {}
    for name, n in (("qm", M_WIDTH), ("km", M_WIDTH), ("vm", M_WIDTH), ("om", M_WIDTH), ("gates", 4 * M_HEADS),
                    ("cq", A_Q_RANK), ("ckv", A_KV_RANK), ("kr", A_ROPE), ("ga", D_MODEL), ("gb", D_MODEL)):
        cols[name] = wi[:, o:o + n]
        o += n
    zpad = jnp.zeros((D_MODEL, LANES - A_ROPE), F32)
    w_kr = jnp.concatenate([cols["kr"], zpad, _rotate_half_cols(cols["kr"]), zpad], axis=1)
    uq = w_uq[l].reshape(A_Q_RANK, A_HEADS, A_NOPE + A_ROPE)
    uq_rope = uq[:, :, A_NOPE:]
    hpad = jnp.zeros((A_Q_RANK, A_HEADS, LANES - A_ROPE), F32)
    w_qa = jnp.concatenate([uq, hpad], axis=2).reshape(A_Q_RANK, A_HEADS * A_QK_PAD)
    w_qr = jnp.concatenate([_rotate_half_cols(uq_rope), hpad], axis=2).reshape(A_Q_RANK, A_HEADS * LANES)
    row = lambda v: v.reshape(1, -1).astype(F32)
    return {
        "norm_mix_g": row(norm_mix_g[l]),
        "w_big": jnp.concatenate([cols[n] for n in ("qm", "km", "vm", "om", "ga", "gb")], axis=1).astype(BF16),
        "w_c": jnp.concatenate([cols["cq"], cols["ckv"]], axis=1).astype(BF16),
        "w_kr": w_kr.astype(BF16),
        "w_g": cols["gates"].astype(BF16),
        "w_gt": cols["gates"].T.astype(BF16),
        "b_g": row(b_gates[l]),
        "b_gt": b_gates[l].reshape(-1, 1).astype(F32),
        "conv_w": conv_w[l],
        "conv_b": row(conv_b[l]),
        "mh_norm_g": row(mh_norm_g[l]),
        "g_cq": row(g_cq[l]),
        "g_ckv": row(g_ckv[l]),
        "w_qa": w_qa.astype(BF16),
        "w_qr": w_qr.astype(BF16),
        "w_ukv": w_ukv[l].astype(BF16),
        "w_br_a": w_br_a[l].astype(BF16),
        "w_br_b": w_br_b[l].astype(BF16),
        "w_out": w_out[l].astype(BF16),
        "norm_x_g": row(norm_x_g[l]),
        "norm_mem_g": row(norm_mem_g[l]),
        "w_xq": w_xq[l].astype(BF16),
        "w_xk": w_xk[l].astype(BF16),
        "w_xv": w_xv[l].astype(BF16),
        "w_xo": w_xo[l].astype(BF16),
        "norm_ffn_g": row(norm_ffn_g[l]),
        "w_router": w_router[l],
        "w_router_t": w_router[l].T,
        "b_router": row(b_router[l]),
        "b_router_t": b_router[l].reshape(-1, 1).astype(F32),
        "w_e_gate": w_e_gate[l].astype(BF16),
        "w_e_up": w_e_up[l].astype(BF16),
        "w_e_down": w_e_down[l].astype(BF16),
        "final_norm_g": row(final_norm_g),
    }


def _trunk(x, mem, w):
    B, S, _ = x.shape
    T = B * S
    n_mem = mem.shape[1]
    w = dict(w)
    w["cos"], w["sin"] = _rope_tables(S)
    x2d = x.reshape(T, D_MODEL)

    z, c, kr, gate, gatet = _inproj(x2d, S, w)
    qk = _conv(z, S, w)
    hf, hb = _mlstm(qk, z, gate, gatet, B, S)
    qc, kc, vc = _mla_proj(c, kr, B, S, w)
    att = _flash(qc, kc, vc)
    x1 = _mixer_out(hf, hb, z, att, x2d, w)

    kx, vx = _mem_proj(mem.reshape(B * n_mem, D_MODEL), w)
    x2, xn, aff, aff_t = _cross_router(x1, kx, vx, S, n_mem, w)

    cap = max(1, EC_FACTOR * T // N_EXPERTS)
    pos, bst = _select(aff_t, cap)
    tu = min(MOE_TOKEN_TILE, T)
    tc = min(MOE_SLOT_TILE, cap)
    gather_tabs, combine_tabs = _moe_tables(bst.reshape(N_EXPERTS, -1), cap, T, tu, tc)
    pos_row = pos.reshape(N_EXPERTS, 1, T)
    xe = _moe_gather(xn, pos_row, gather_tabs, cap, tu, tc)
    ye = _moe_ffn(xe, w)
    y = _moe_combine(ye, pos_row, aff, x2, combine_tabs, w["final_norm_g"], cap, tu, tc)
    return y.reshape(B, S, D_MODEL)


def kernel(x_prompt, x_sample, mem_prompt, mem_sample, norm_mix_g, w_in, b_gates, conv_w, conv_b, mh_norm_g, g_cq, g_ckv, w_uq, w_ukv, w_br_a, w_br_b, w_out, norm_x_g, norm_mem_g, w_xq, w_xk, w_xv, w_xo, norm_ffn_g, w_router, b_router, w_e_gate, w_e_up, w_e_down, final_norm_g):
    w = _prep_weights(norm_mix_g, w_in, b_gates, conv_w, conv_b, mh_norm_g, g_cq, g_ckv, w_uq, w_ukv,
                      w_br_a, w_br_b, w_out, norm_x_g, norm_mem_g, w_xq, w_xk, w_xv, w_xo, norm_ffn_g,
                      w_router, b_router, w_e_gate, w_e_up, w_e_down, final_norm_g)
    return (_trunk(x_prompt, mem_prompt, w), _trunk(x_sample, mem_sample, w))
```

```python
import functools

import jax
import jax.numpy as jnp
from jax import lax
from jax.experimental import pallas as pl
from jax.experimental.pallas import tpu as pltpu

F32 = jnp.float32
BF16 = jnp.bfloat16
I32 = jnp.int32

D_MODEL = 1024
M_WIDTH = 1024
M_HEADS = 4
M_HEAD_DIM = M_WIDTH // M_HEADS
M_CHUNK = 128
M_CONV = 5
A_HEADS = 8
A_NOPE = 128
A_ROPE = 64
A_V = 128
A_Q_RANK = 256
A_KV_RANK = 256
A_QK_PAD = 256
ROPE_BASE = 10000.0
X_HEADS = 4
X_HEAD_DIM = D_MODEL // X_HEADS
N_EXPERTS = 16
EC_FACTOR = 2
E_FF = 1024
NORM_EPS = 1e-6
LANES = 128
BIG_COLS = 6 * 1024
VMEM_LIMIT = 56 * 1024 * 1024
MOE_TOKEN_TILE = 1024
MOE_SLOT_TILE = 256


def _cparams(sem):
    return pltpu.CompilerParams(dimension_semantics=sem, vmem_limit_bytes=VMEM_LIMIT)


def _dot(a, b):
    return jnp.dot(a, b, preferred_element_type=F32)


def _dot_nt(a, b):
    return lax.dot_general(a, b, (((1,), (1,)), ((), ())), preferred_element_type=F32)


def _dot_tn(a, b):
    return lax.dot_general(a, b, (((0,), (0,)), ((), ())), preferred_element_type=F32)


def _rms(x, g):
    return x * lax.rsqrt(jnp.mean(x * x, axis=-1, keepdims=True) + NORM_EPS) * g


def _sigmoid(x):
    return 1.0 / (1.0 + jnp.exp(-x))


def _split3(x):
    hi = x.astype(BF16)
    r = x - hi.astype(F32)
    mid = r.astype(BF16)
    lo = (r - mid.astype(F32)).astype(BF16)
    return hi, mid, lo


def _inproj_kernel(x_ref, g_ref, wbig_ref, wc_ref, wkr_ref, wg_ref, wgt_ref, bg_ref, bgt_ref,
                   cos_ref, sin_ref,
                   z_ref, c_ref, kr_ref, gate_ref, gatet_ref, xn_scr):
    j = pl.program_id(1)

    @pl.when(j == 0)
    def _():
        xn = _rms(x_ref[...], g_ref[...]).astype(BF16)
        xn_scr[...] = xn
        c_ref[...] = _dot(xn, wc_ref[...]).astype(BF16)
        kr = _dot(xn, wkr_ref[...])
        kr_ref[...] = (kr[:, :LANES] * cos_ref[...] + kr[:, LANES:] * sin_ref[...]).astype(BF16)
        gate_ref[...] = _dot(xn, wg_ref[...]) + bg_ref[...]
        gatet_ref[...] = _dot_nt(wgt_ref[...], xn) + bgt_ref[...]

    z_ref[...] = _dot(xn_scr[...], wbig_ref[...]).astype(BF16)


def _inproj(x, S, w):
    T = x.shape[0]
    tm = min(1024, S)
    tn = 1024
    nseq = S // tm
    row = lambda i, j: (i, 0)
    const = lambda i, j: (0, 0)
    return pl.pallas_call(
        _inproj_kernel,
        grid=(T // tm, BIG_COLS // tn),
        in_specs=[
            pl.BlockSpec((tm, D_MODEL), row),
            pl.BlockSpec((1, D_MODEL), const),
            pl.BlockSpec((D_MODEL, tn), lambda i, j: (0, j)),
            pl.BlockSpec((D_MODEL, 512), const),
            pl.BlockSpec((D_MODEL, 256), const),
            pl.BlockSpec((D_MODEL, 16), const),
            pl.BlockSpec((16, D_MODEL), const),
            pl.BlockSpec((1, 16), const),
            pl.BlockSpec((16, 1), const),
            pl.BlockSpec((tm, LANES), lambda i, j: (i % nseq, 0)),
            pl.BlockSpec((tm, LANES), lambda i, j: (i % nseq, 0)),
        ],
        out_specs=[
            pl.BlockSpec((tm, tn), lambda i, j: (i, j)),
            pl.BlockSpec((tm, 512), row),
            pl.BlockSpec((tm, LANES), row),
            pl.BlockSpec((tm, 16), row),
            pl.BlockSpec((16, tm), lambda i, j: (0, i)),
        ],
        out_shape=[
            jax.ShapeDtypeStruct((T, BIG_COLS), BF16),
            jax.ShapeDtypeStruct((T, 512), BF16),
            jax.ShapeDtypeStruct((T, LANES), BF16),
            jax.ShapeDtypeStruct((T, 16), F32),
            jax.ShapeDtypeStruct((16, T), F32),
        ],
        scratch_shapes=[pltpu.VMEM((tm, D_MODEL), BF16)],
        compiler_params=_cparams(("parallel", "arbitrary")),
        name="inproj",
    )(x, w["norm_mix_g"], w["w_big"], w["w_c"], w["w_kr"], w["w_g"], w["w_gt"], w["b_g"], w["b_gt"],
      w["cos"], w["sin"])


CONV_HALO = 16


def _conv_kernel(z_ref, zp_ref, zn_ref, w_ref, b_ref, o_ref, scr, *, tr, tiles_per_seq, q_tiles):
    i = pl.program_id(0)
    j = pl.program_id(1)
    it = i % tiles_per_seq
    keep_prev = jnp.where(it == 0, 0.0, 1.0)
    keep_next = jnp.where(it == tiles_per_seq - 1, 0.0, 1.0)
    scr[0:8, :] = zp_ref[...].astype(F32)[8:16, :] * keep_prev
    scr[8:8 + tr, :] = z_ref[...].astype(F32)
    scr[8 + tr:16 + tr, :] = zn_ref[...].astype(F32)[0:8, :] * keep_next
    acc = jnp.zeros(o_ref.shape, F32) + b_ref[...]
    for k in range(M_CONV):
        off = 8 - M_CONV // 2 + k
        acc = acc + w_ref[k:k + 1, :] * scr[off:off + tr, :]
    y = acc * _sigmoid(acc)
    scale = jnp.where(j < q_tiles, M_HEAD_DIM ** -0.5, 1.0)
    o_ref[...] = (y * scale).astype(BF16)


def _conv(z, S, w):
    T = z.shape[0]
    tr = min(512, S)
    tcw = 512
    tiles_per_seq = S // tr
    hb = tr // CONV_HALO
    nhalo = T // CONV_HALO
    kern = functools.partial(_conv_kernel, tr=tr, tiles_per_seq=tiles_per_seq, q_tiles=M_WIDTH // tcw)
    return pl.pallas_call(
        kern,
        grid=(T // tr, 2 * M_WIDTH // tcw),
        in_specs=[
            pl.BlockSpec((tr, tcw), lambda i, j: (i, j)),
            pl.BlockSpec((CONV_HALO, tcw), lambda i, j: (jnp.maximum(i * hb - 1, 0), j)),
            pl.BlockSpec((CONV_HALO, tcw), lambda i, j: (jnp.minimum((i + 1) * hb, nhalo - 1), j)),
            pl.BlockSpec((M_CONV, tcw), lambda i, j: (0, j)),
            pl.BlockSpec((1, tcw), lambda i, j: (0, j)),
        ],
        out_specs=pl.BlockSpec((tr, tcw), lambda i, j: (i, j)),
        out_shape=jax.ShapeDtypeStruct((T, 2 * M_WIDTH), BF16),
        scratch_shapes=[pltpu.VMEM((tr + 16, tcw), F32)],
        compiler_params=_cparams(("parallel", "parallel")),
        name="conv_silu",
    )(z, z, z, w["conv_w"], w["conv_b"])


def _log_sigmoid(x):
    return -(jnp.maximum(-x, 0.0) + jnp.log1p(jnp.exp(-jnp.abs(x))))


def _mlstm_dir(d, q_ref, k_ref, v_ref, g_ref, gt_ref, o_ref, c_scr, n_scr, m_scr):
    L = M_CHUNK
    r = lax.broadcasted_iota(I32, (L, L), 0)
    c = lax.broadcasted_iota(I32, (L, L), 1)
    if d == 0:
        mask = c <= r
    else:
        mask = c >= r
    tri_col = jnp.where(mask, 1.0, 0.0).astype(BF16)
    tri_row = jnp.where(r <= c if d == 0 else r >= c, 1.0, 0.0).astype(BF16)
    g = g_ref[...]
    gt = gt_ref[...]
    lf = _log_sigmoid(g)
    lft = _log_sigmoid(gt)
    b_col_all = sum(_dot(tri_col, p) for p in _split3(lf))
    b_row_all = sum(_dot(p, tri_row) for p in _split3(lft))
    edge = L - 1 if d == 0 else 0
    for h in range(M_HEADS):
        idx = d * M_HEADS + h
        ci = d * 2 * M_HEADS + h
        cf = ci + M_HEADS
        i_col = g[:, ci:ci + 1]
        i_row = gt[ci:ci + 1, :]
        b_col = b_col_all[:, cf:cf + 1]
        b_row = b_row_all[cf:cf + 1, :]
        b_last = b_col[edge:edge + 1, :]
        m_prev = m_scr[idx][0:1, 0:1]
        a = b_col + m_prev
        dm = jnp.where(mask, b_col - b_row + i_row, -jnp.inf)
        m_t = jnp.maximum(a, jnp.max(dm, axis=1, keepdims=True))
        w_inter = jnp.exp(a - m_t)
        p = jnp.exp(dm - m_t)
        sl = slice(h * M_HEAD_DIM, (h + 1) * M_HEAD_DIM)
        q = q_ref[:, sl]
        k = k_ref[:, sl]
        v = v_ref[:, sl]
        s = _dot_nt(q, k) * p
        cst = c_scr[idx]
        nst = n_scr[idx]
        num = _dot(s.astype(BF16), v) + w_inter * _dot(q, cst.astype(BF16))
        qn = jnp.sum(q.astype(F32) * nst, axis=1, keepdims=True)
        den = jnp.sum(s, axis=1, keepdims=True) + w_inter * qn
        o_ref[:, sl] = num / jnp.maximum(jnp.abs(den), jnp.exp(-m_t))
        gk = b_last - b_col + i_col
        m_new = jnp.maximum(b_last + m_prev, jnp.max(gk, axis=0, keepdims=True))
        decay = jnp.exp(b_last + m_prev - m_new)
        wk = jnp.exp(gk - m_new)
        wv = (wk * v.astype(F32)).astype(BF16)
        c_scr[idx] = decay * cst + _dot_tn(k, wv)
        n_scr[idx] = decay * nst + jnp.sum(wk * k.astype(F32), axis=0, keepdims=True)
        m_scr[idx] = jnp.broadcast_to(m_new, (8, LANES))


def _mlstm_kernel(qf, kf, vf, gf, gtf, qb, kb, vb, gb, gtb, of, ob, c_scr, n_scr, m_scr):
    @pl.when(pl.program_id(1) == 0)
    def _():
        c_scr[...] = jnp.zeros(c_scr.shape, F32)
        n_scr[...] = jnp.zeros(n_scr.shape, F32)
        m_scr[...] = jnp.zeros(m_scr.shape, F32)

    _mlstm_dir(0, qf, kf, vf, gf, gtf, of, c_scr, n_scr, m_scr)
    _mlstm_dir(1, qb, kb, vb, gb, gtb, ob, c_scr, n_scr, m_scr)


def _mlstm(qk, z, gate, gatet, B, S):
    T = B * S
    L = M_CHUNK
    nc = S // L
    fwd = lambda b, c: b * nc + c
    bwd = lambda b, c: b * nc + nc - 1 - c

    def specs(pos):
        return [
            pl.BlockSpec((L, M_WIDTH), lambda b, c: (pos(b, c), 0)),
            pl.BlockSpec((L, M_WIDTH), lambda b, c: (pos(b, c), 1)),
            pl.BlockSpec((L, M_WIDTH), lambda b, c: (pos(b, c), 2)),
            pl.BlockSpec((L, 16), lambda b, c: (pos(b, c), 0)),
            pl.BlockSpec((16, L), lambda b, c: (0, pos(b, c))),
        ]

    nstate = 2 * M_HEADS
    return pl.pallas_call(
        _mlstm_kernel,
        grid=(B, nc),
        in_specs=specs(fwd) + specs(bwd),
        out_specs=[
            pl.BlockSpec((L, M_WIDTH), lambda b, c: (fwd(b, c), 0)),
            pl.BlockSpec((L, M_WIDTH), lambda b, c: (bwd(b, c), 0)),
        ],
        out_shape=[jax.ShapeDtypeStruct((T, M_WIDTH), F32)] * 2,
        scratch_shapes=[
            pltpu.VMEM((nstate, M_HEAD_DIM, M_HEAD_DIM), F32),
            pltpu.VMEM((nstate, 1, M_HEAD_DIM), F32),
            pltpu.VMEM((nstate, 8, LANES), F32),
        ],
        compiler_params=_cparams(("parallel", "arbitrary")),
        name="mlstm",
    )(qk, qk, z, gate, gatet, qk, qk, z, gate, gatet)


ATT_TILE = 512


def _mla_proj_kernel(c_ref, kr_ref, cost_ref, sint_ref, gq_ref, gkv_ref, wqat_ref, wqrt_ref, wk_ref, wvt_ref,
                     qt_ref, k_ref, vt_ref):
    cq = c_ref[:, :A_Q_RANK].astype(F32)
    ckv = c_ref[:, A_Q_RANK:].astype(F32)
    cqn = _rms(cq, gq_ref[...]).astype(BF16)
    ckvn = _rms(ckv, gkv_ref[...]).astype(BF16)
    qat = _dot_nt(wqat_ref[...], cqn)
    qrt = _dot_nt(wqrt_ref[...], cqn)
    kn = _dot(ckvn, wk_ref[...])
    vt = _dot_nt(wvt_ref[...], ckvn)
    cost = cost_ref[...]
    sint = sint_ref[...]
    kr = kr_ref[...]
    for h in range(A_HEADS):
        o = h * A_QK_PAD
        qt_ref[0, h, :LANES, :] = qat[o:o + LANES, :].astype(BF16)
        qt_ref[0, h, LANES:, :] = (qat[o + LANES:o + 2 * LANES, :] * cost
                                   + qrt[h * LANES:(h + 1) * LANES, :] * sint).astype(BF16)
        k_ref[0, h, :, :LANES] = kn[:, h * A_NOPE:(h + 1) * A_NOPE].astype(BF16)
        k_ref[0, h, :, LANES:] = kr
        vt_ref[0, h, 0] = vt[h * A_V:(h + 1) * A_V, :].astype(BF16)


def _mla_proj(c, kr, B, S, w):
    tm = min(ATT_TILE, S)
    nseq = S // tm
    row = lambda i: (i, 0)
    const = lambda i: (0, 0)
    seq_t = lambda i: (0, i % nseq)
    return pl.pallas_call(
        _mla_proj_kernel,
        grid=(B * S // tm,),
        in_specs=[
            pl.BlockSpec((tm, 512), row),
            pl.BlockSpec((tm, LANES), row),
            pl.BlockSpec((LANES, tm), seq_t),
            pl.BlockSpec((LANES, tm), seq_t),
            pl.BlockSpec((1, A_Q_RANK), const),
            pl.BlockSpec((1, A_KV_RANK), const),
            pl.BlockSpec((A_HEADS * A_QK_PAD, A_Q_RANK), const),
            pl.BlockSpec((A_HEADS * LANES, A_Q_RANK), const),
            pl.BlockSpec((A_KV_RANK, A_HEADS * A_NOPE), const),
            pl.BlockSpec((A_HEADS * A_V, A_KV_RANK), const),
        ],
        out_specs=[
            pl.BlockSpec((1, A_HEADS, A_QK_PAD, tm), lambda i: (i // nseq, 0, 0, i % nseq)),
            pl.BlockSpec((1, A_HEADS, tm, A_QK_PAD), lambda i: (i // nseq, 0, i % nseq, 0)),
            pl.BlockSpec((1, A_HEADS, 1, A_V, tm), lambda i: (i // nseq, 0, i % nseq, 0, 0)),
        ],
        out_shape=[
            jax.ShapeDtypeStruct((B, A_HEADS, A_QK_PAD, S), BF16),
            jax.ShapeDtypeStruct((B, A_HEADS, S, A_QK_PAD), BF16),
            jax.ShapeDtypeStruct((B, A_HEADS, nseq, A_V, tm), BF16),
        ],
        compiler_params=_cparams(("parallel",)),
        name="mla_proj",
    )(c, kr, w["cos_t"], w["sin_t"], w["g_cq"], w["g_ckv"], w["w_qat"], w["w_qrt"], w["w_uk"], w["w_uvt"])


def _flash_kernel(q_ref, k_ref, vt_ref, o_ref, s_scr, *, tk, nk):
    qt = q_ref[0, 0]
    tq = qt.shape[1]
    c = (A_NOPE + A_ROPE) ** -0.5 * 1.4426950408889634

    def scores(j, slot):
        start = pl.multiple_of(j * tk, tk)
        s_scr[slot] = _dot(k_ref[0, 0, pl.ds(start, tk), :], qt)

    def accumulate(j, slot, carry):
        m, l, acc = carry
        s = s_scr[slot]
        m_new = jnp.maximum(m, jnp.max(s, axis=0, keepdims=True))
        p = jnp.exp2((s - m_new) * c)
        alpha = jnp.exp2((m - m_new) * c)
        l = alpha * l + jnp.sum(p, axis=0, keepdims=True)
        acc = alpha * acc + _dot(vt_ref[0, 0, j], p.astype(BF16))
        return m_new, l, acc

    def body(jj, carry):
        j = 2 * jj
        scores(j + 1, 1)
        carry = accumulate(j, 0, carry)
        scores(jnp.minimum(j + 2, nk - 1), 0)
        return accumulate(j + 1, 1, carry)

    carry = (jnp.full((1, tq), -jnp.inf, F32), jnp.zeros((1, tq), F32), jnp.zeros((A_V, tq), F32))
    scores(0, 0)
    carry = lax.fori_loop(0, nk // 2, body, carry)
    if nk % 2 == 1:
        carry = accumulate(nk - 1, 0, carry)
    _, l, acc = carry
    o_ref[...] = (acc / l).T.astype(BF16)


def _flash(qt, k, vt):
    B, H, S, _ = k.shape
    tq = min(ATT_TILE, S)
    tk = min(ATT_TILE, S)
    nq = S // tq
    nk = S // tk
    kern = functools.partial(_flash_kernel, tk=tk, nk=nk)
    return pl.pallas_call(
        kern,
        grid=(B, H, nq),
        in_specs=[
            pl.BlockSpec((1, 1, A_QK_PAD, tq), lambda b, h, i: (b, h, 0, i)),
            pl.BlockSpec((1, 1, S, A_QK_PAD), lambda b, h, i: (b, h, 0, 0)),
            pl.BlockSpec((1, 1, nk, A_V, tk), lambda b, h, i: (b, h, 0, 0, 0)),
        ],
        out_specs=pl.BlockSpec((tq, A_V), lambda b, h, i: (b * nq + i, h)),
        out_shape=jax.ShapeDtypeStruct((B * S, A_HEADS * A_V), BF16),
        scratch_shapes=[pltpu.VMEM((2, tk, tq), F32)],
        compiler_params=_cparams(("parallel", "parallel", "arbitrary")),
        name="mla_flash",
    )(qt, k, vt)


def _mixer_out_kernel(hf_ref, hb_ref, om_ref, ga_ref, gb_ref, att_ref, x_ref, mg_ref,
                      wa_ref, wb_ref, wo_ref, o_ref):
    hs = hf_ref[...] + hb_ref[...]
    mg = mg_ref[...]
    parts = []
    for h in range(M_HEADS):
        sl = slice(h * M_HEAD_DIM, (h + 1) * M_HEAD_DIM)
        parts.append(_rms(hs[:, sl], mg[:, sl]))
    hn = jnp.concatenate(parts, axis=1) * _sigmoid(om_ref[...].astype(F32))
    y_a = _dot(hn.astype(BF16), wa_ref[...])
    y_b = _dot(att_ref[...], wb_ref[...])
    merged = _sigmoid(ga_ref[...].astype(F32)) * y_a + _sigmoid(gb_ref[...].astype(F32)) * y_b
    o_ref[...] = x_ref[...] + _dot(merged.astype(BF16), wo_ref[...])


def _mixer_out(hf, hb, z, att, x, w):
    T = x.shape[0]
    tm = min(512, T)
    row = lambda i: (i, 0)
    const = lambda i: (0, 0)
    wspec = pl.BlockSpec((D_MODEL, D_MODEL), const)
    return pl.pallas_call(
        _mixer_out_kernel,
        grid=(T // tm,),
        in_specs=[
            pl.BlockSpec((tm, M_WIDTH), row),
            pl.BlockSpec((tm, M_WIDTH), row),
            pl.BlockSpec((tm, M_WIDTH), lambda i: (i, 3)),
            pl.BlockSpec((tm, D_MODEL), lambda i: (i, 4)),
            pl.BlockSpec((tm, D_MODEL), lambda i: (i, 5)),
            pl.BlockSpec((tm, D_MODEL), row),
            pl.BlockSpec((tm, D_MODEL), row),
            pl.BlockSpec((1, M_WIDTH), const),
            wspec, wspec, wspec,
        ],
        out_specs=pl.BlockSpec((tm, D_MODEL), row),
        out_shape=jax.ShapeDtypeStruct((T, D_MODEL), F32),
        compiler_params=_cparams(("parallel",)),
        name="mixer_out",
    )(hf, hb, z, z, z, att, x, w["mh_norm_g"], w["w_br_a"], w["w_br_b"], w["w_out"])


def _mem_kernel(m_ref, g_ref, wk_ref, wv_ref, k_ref, v_ref):
    mn = _rms(m_ref[...], g_ref[...]).astype(BF16)
    k_ref[...] = _dot(mn, wk_ref[...]).astype(BF16)
    v_ref[...] = _dot(mn, wv_ref[...]).astype(BF16)


def _mem_proj(mem, w):
    R = mem.shape[0]
    tm = 256
    row = lambda i: (i, 0)
    const = lambda i: (0, 0)
    wspec = pl.BlockSpec((D_MODEL, D_MODEL), const)
    return pl.pallas_call(
        _mem_kernel,
        grid=(R // tm,),
        in_specs=[pl.BlockSpec((tm, D_MODEL), row), pl.BlockSpec((1, D_MODEL), const), wspec, wspec],
        out_specs=[pl.BlockSpec((tm, D_MODEL), row)] * 2,
        out_shape=[jax.ShapeDtypeStruct((R, D_MODEL), BF16)] * 2,
        compiler_params=_cparams(("parallel",)),
        name="mem_proj",
    )(mem, w["norm_mem_g"], w["w_xk"], w["w_xv"])


def _cross_router_kernel(x_ref, kx_ref, vx_ref, gx_ref, gf_ref, wq_ref, wo_ref, wr_ref, wrt_ref,
                         br_ref, brt_ref, x2_ref, xn_ref, aff_ref, afft_ref):
    x1 = x_ref[...]
    xn = _rms(x1, gx_ref[...]).astype(BF16)
    q = _dot(xn, wq_ref[...])
    outs = []
    for h in range(X_HEADS):
        sl = slice(h * X_HEAD_DIM, (h + 1) * X_HEAD_DIM)
        s = _dot_nt(q[:, sl].astype(BF16), kx_ref[:, sl]) * (X_HEAD_DIM ** -0.5)
        e = jnp.exp(s - jnp.max(s, axis=1, keepdims=True))
        p = e / jnp.sum(e, axis=1, keepdims=True)
        outs.append(_dot(p.astype(BF16), vx_ref[:, sl]))
    o = jnp.concatenate(outs, axis=1)
    x2 = x1 + _dot(o.astype(BF16), wo_ref[...])
    x2_ref[...] = x2
    xf = _rms(x2, gf_ref[...])
    hi = xf.astype(BF16)
    xn_ref[...] = hi
    lo = (xf - hi.astype(F32)).astype(BF16)
    wr = wr_ref[...]
    wr_hi = wr.astype(BF16)
    wr_lo = (wr - wr_hi.astype(F32)).astype(BF16)
    logit = _dot(hi, wr_hi) + _dot(lo, wr_hi) + _dot(hi, wr_lo) + br_ref[...]
    e = jnp.exp(logit - jnp.max(logit, axis=1, keepdims=True))
    aff_ref[...] = e / jnp.sum(e, axis=1, keepdims=True)
    wrt = wrt_ref[...]
    wrt_hi = wrt.astype(BF16)
    wrt_lo = (wrt - wrt_hi.astype(F32)).astype(BF16)
    logit_t = _dot_nt(wrt_hi, hi) + _dot_nt(wrt_hi, lo) + _dot_nt(wrt_lo, hi) + brt_ref[...]
    et = jnp.exp(logit_t - jnp.max(logit_t, axis=0, keepdims=True))
    afft_ref[...] = et / jnp.sum(et, axis=0, keepdims=True)


def _cross_router(x1, kx, vx, S, n_mem, w):
    T = x1.shape[0]
    tm = min(512, S)
    nseq = S // tm
    row = lambda i: (i, 0)
    const = lambda i: (0, 0)
    wspec = pl.BlockSpec((D_MODEL, D_MODEL), const)
    memspec = pl.BlockSpec((n_mem, D_MODEL), lambda i: (i // nseq, 0))
    return pl.pallas_call(
        _cross_router_kernel,
        grid=(T // tm,),
        in_specs=[
            pl.BlockSpec((tm, D_MODEL), row), memspec, memspec,
            pl.BlockSpec((1, D_MODEL), const), pl.BlockSpec((1, D_MODEL), const),
            wspec, wspec,
            pl.BlockSpec((D_MODEL, N_EXPERTS), const), pl.BlockSpec((N_EXPERTS, D_MODEL), const),
            pl.BlockSpec((1, N_EXPERTS), const), pl.BlockSpec((N_EXPERTS, 1), const),
        ],
        out_specs=[
            pl.BlockSpec((tm, D_MODEL), row),
            pl.BlockSpec((tm, D_MODEL), row),
            pl.BlockSpec((tm, N_EXPERTS), row),
            pl.BlockSpec((N_EXPERTS, tm), lambda i: (0, i)),
        ],
        out_shape=[
            jax.ShapeDtypeStruct((T, D_MODEL), F32),
            jax.ShapeDtypeStruct((T, D_MODEL), BF16),
            jax.ShapeDtypeStruct((T, N_EXPERTS), F32),
            jax.ShapeDtypeStruct((N_EXPERTS, T), F32),
        ],
        compiler_params=_cparams(("parallel",)),
        name="cross_router",
    )(x1, kx, vx, w["norm_x_g"], w["norm_ffn_g"], w["w_xq"], w["w_xo"], w["w_router"], w["w_router_t"],
      w["b_router"], w["b_router_t"])


def _excl_cumsum(mask_f, strict_lane, strict_blk):
    nb = mask_f.shape[0]
    within = _dot(mask_f.astype(BF16), strict_lane)
    tot = jnp.sum(mask_f, axis=1, keepdims=True)
    bstart = _dot(strict_blk, jnp.broadcast_to(tot, (nb, LANES)).astype(BF16))
    return within + bstart, bstart


def _select_kernel(aff_ref, pos_ref, bst_ref, *, cap):
    a = aff_ref[0]
    nb = a.shape[0]
    bits = pltpu.bitcast(a, I32)

    def radix(i, prefix):
        cand = prefix | jnp.left_shift(jnp.int32(1), 30 - i)
        cnt = jnp.sum(jnp.where(bits >= cand, 1.0, 0.0), axis=(0, 1), keepdims=True)
        return jnp.where(cnt >= cap, cand, prefix)

    thr = lax.fori_loop(0, 31, radix, jnp.zeros((1, 1), I32))
    gt = bits > thr
    eq = bits == thr
    need = cap - jnp.sum(jnp.where(gt, 1.0, 0.0), axis=(0, 1), keepdims=True)
    li = lax.broadcasted_iota(I32, (LANES, LANES), 0)
    lj = lax.broadcasted_iota(I32, (LANES, LANES), 1)
    strict_lane = jnp.where(li < lj, 1.0, 0.0).astype(BF16)
    bi = lax.broadcasted_iota(I32, (nb, nb), 0)
    bj = lax.broadcasted_iota(I32, (nb, nb), 1)
    strict_blk = jnp.where(bj < bi, 1.0, 0.0).astype(BF16)
    rank, _ = _excl_cumsum(jnp.where(eq, 1.0, 0.0), strict_lane, strict_blk)
    sel = gt | (eq & (rank < need))
    pos, bstart = _excl_cumsum(jnp.where(sel, 1.0, 0.0), strict_lane, strict_blk)
    pos_ref[0] = jnp.where(sel, pos.astype(I32), -1)
    col = jnp.broadcast_to(bstart[:, 0:1], (nb, nb))
    bst_ref[0] = jnp.sum(jnp.where(bi == bj, col, 0.0), axis=0, keepdims=True).astype(I32)


def _select(aff_t, cap):
    E, T = aff_t.shape
    nb = T // LANES
    kern = functools.partial(_select_kernel, cap=cap)
    return pl.pallas_call(
        kern,
        grid=(E,),
        in_specs=[pl.BlockSpec((1, nb, LANES), lambda e: (e, 0, 0))],
        out_specs=[
            pl.BlockSpec((1, nb, LANES), lambda e: (e, 0, 0)),
            pl.BlockSpec((1, 1, nb), lambda e: (e, 0, 0)),
        ],
        out_shape=[
            jax.ShapeDtypeStruct((E, nb, LANES), I32),
            jax.ShapeDtypeStruct((E, 1, nb), I32),
        ],
        compiler_params=_cparams(("parallel",)),
        name="ec_select",
    )(aff_t.reshape(E, nb, LANES))


def _moe_tables(bst, cap, T, tu, tc):
    E = bst.shape[0]
    nu = T // tu
    ns = cap // tc
    start = bst[:, ::tu // LANES]
    end = jnp.concatenate([start[:, 1:], jnp.full((E, 1), cap, I32)], axis=1)
    cnt = end - start
    s_lo = jnp.minimum(start // tc, ns - 1)
    s_hi = jnp.where(cnt > 0, (end - 1) // tc, s_lo)
    ncell = jnp.where(cnt > 0, s_hi - s_lo + 1, 0)

    def enumerate_cells(nc_flat, slo_flat, steps):
        off_end = jnp.cumsum(nc_flat, axis=-1)
        off = off_end - nc_flat
        total = off_end[..., -1:]
        k = jnp.arange(steps, dtype=I32)
        kk = jnp.minimum(k, total - 1)
        grp = jnp.sum((off_end[..., None, :] <= kk[..., :, None]).astype(I32), axis=-1)
        s = jnp.take_along_axis(slo_flat, grp, axis=-1) + kk - jnp.take_along_axis(off, grp, axis=-1)
        valid = (k < total).astype(I32)
        return grp, s, valid

    pg = nu + ns
    g_u, g_s, g_valid = enumerate_cells(ncell, s_lo, pg)
    prev_s = jnp.concatenate([jnp.full((E, 1), -1, I32), g_s[:, :-1]], axis=1)
    g_first = g_valid * (g_s != prev_s).astype(I32)
    gather_tabs = tuple(t.reshape(-1) for t in (g_u, g_s, g_valid, g_first))

    ncell_c = ncell.at[0].set(jnp.maximum(ncell[0], 1))
    pc = E * (nu + ns)
    c_grp, c_s, c_valid = enumerate_cells(ncell_c.T.reshape(-1), s_lo.T.reshape(-1), pc)
    c_u = c_grp // E
    c_e = c_grp % E
    prev_u = jnp.concatenate([jnp.full((1,), -1, I32), c_u[:-1]])
    next_u = jnp.concatenate([c_u[1:], jnp.full((1,), -1, I32)])
    next_valid = jnp.concatenate([c_valid[1:], jnp.zeros((1,), I32)])
    c_first = c_valid * (c_u != prev_u).astype(I32)
    c_last = c_valid * jnp.maximum((c_u != next_u).astype(I32), 1 - next_valid)
    combine_tabs = (c_u, c_e, c_s, c_valid, c_first, c_last)
    return gather_tabs, combine_tabs


def _gather_kernel(u_tab, s_tab, valid_tab, first_tab, pos_ref, x_ref, o_ref, *, steps, tc):
    step = pl.program_id(0) * steps + pl.program_id(1)
    s = s_tab[step]
    tu = pos_ref.shape[-1]
    slot = lax.broadcasted_iota(I32, (tc, tu), 0) + s * tc
    onehot = jnp.where(pos_ref[0] == slot, 1.0, 0.0).astype(BF16)

    @pl.when(first_tab[step] == 1)
    def _():
        o_ref[0] = _dot(onehot, x_ref[...]).astype(BF16)

    @pl.when((valid_tab[step] == 1) & (first_tab[step] == 0))
    def _():
        o_ref[0] = o_ref[0] + _dot(onehot, x_ref[...]).astype(BF16)


def _moe_gather(xn, pos_row, tabs, cap, tu, tc):
    T = xn.shape[0]
    E = N_EXPERTS
    steps = T // tu + cap // tc
    kern = functools.partial(_gather_kernel, steps=steps, tc=tc)
    grid_spec = pltpu.PrefetchScalarGridSpec(
        num_scalar_prefetch=4,
        grid=(E, steps),
        in_specs=[
            pl.BlockSpec((1, 1, tu), lambda e, k, u, s, v, f: (e, 0, u[e * steps + k])),
            pl.BlockSpec((tu, D_MODEL), lambda e, k, u, s, v, f: (u[e * steps + k], 0)),
        ],
        out_specs=pl.BlockSpec((1, tc, D_MODEL), lambda e, k, u, s, v, f: (e, s[e * steps + k], 0)),
    )
    return pl.pallas_call(
        kern,
        grid_spec=grid_spec,
        out_shape=jax.ShapeDtypeStruct((E, cap, D_MODEL), BF16),
        compiler_params=_cparams(("parallel", "arbitrary")),
        name="moe_gather",
    )(*tabs, pos_row, xn)


def _ffn_kernel(x_ref, wg_ref, wu_ref, wd_ref, o_ref):
    x = x_ref[0]
    g = _dot(x, wg_ref[0])
    u = _dot(x, wu_ref[0])
    h = (g * _sigmoid(g) * u).astype(BF16)
    o_ref[0] = _dot(h, wd_ref[0]).astype(BF16)


def _moe_ffn(xe, w):
    E, cap, _ = xe.shape
    tf = min(512, cap)
    wspec = pl.BlockSpec((1, D_MODEL, E_FF), lambda e, i: (e, 0, 0))
    return pl.pallas_call(
        _ffn_kernel,
        grid=(E, cap // tf),
        in_specs=[pl.BlockSpec((1, tf, D_MODEL), lambda e, i: (e, i, 0)), wspec, wspec,
                  pl.BlockSpec((1, E_FF, D_MODEL), lambda e, i: (e, 0, 0))],
        out_specs=pl.BlockSpec((1, tf, D_MODEL), lambda e, i: (e, i, 0)),
        out_shape=jax.ShapeDtypeStruct((E, cap, D_MODEL), BF16),
        compiler_params=_cparams(("parallel", "arbitrary")),
        name="moe_ffn",
    )(xe, w["w_e_gate"], w["w_e_up"], w["w_e_down"])


def _combine_kernel(u_tab, e_tab, s_tab, valid_tab, first_tab, last_tab,
                    pos_ref, ye_ref, aff_ref, x_ref, g_ref, o_ref, acc, *, tc):
    step = pl.program_id(0)
    e = e_tab[step]
    s = s_tab[step]
    tu = pos_ref.shape[-1]

    @pl.when(first_tab[step] == 1)
    def _():
        acc[...] = jnp.zeros(acc.shape, F32)

    @pl.when(valid_tab[step] == 1)
    def _():
        slot = lax.broadcasted_iota(I32, (tc, tu), 0) + s * tc
        onehot = jnp.where(pos_ref[0] == slot, 1.0, 0.0).astype(BF16)
        contrib = _dot_tn(onehot, ye_ref[0])
        lane = lax.broadcasted_iota(I32, aff_ref.shape, 1)
        gate = jnp.sum(jnp.where(lane == e, aff_ref[...], 0.0), axis=1, keepdims=True)
        acc[...] = acc[...] + gate * contrib

    @pl.when(last_tab[step] == 1)
    def _():
        o_ref[...] = _rms(x_ref[...] + acc[...], g_ref[...])


def _moe_combine(ye, pos_row, aff, x2, tabs, final_g, cap, tu, tc):
    T = x2.shape[0]
    E = N_EXPERTS
    steps = E * (T // tu + cap // tc)
    kern = functools.partial(_combine_kernel, tc=tc)
    grid_spec = pltpu.PrefetchScalarGridSpec(
        num_scalar_prefetch=6,
        grid=(steps,),
        in_specs=[
            pl.BlockSpec((1, 1, tu), lambda k, u, e, s, v, f, l: (e[k], 0, u[k])),
            pl.BlockSpec((1, tc, D_MODEL), lambda k, u, e, s, v, f, l: (e[k], s[k], 0)),
            pl.BlockSpec((tu, N_EXPERTS), lambda k, u, e, s, v, f, l: (u[k], 0)),
            pl.BlockSpec((tu, D_MODEL), lambda k, u, e, s, v, f, l: (u[k], 0)),
            pl.BlockSpec((1, D_MODEL), lambda k, u, e, s, v, f, l: (0, 0)),
        ],
        out_specs=pl.BlockSpec((tu, D_MODEL), lambda k, u, e, s, v, f, l: (u[k], 0)),
        scratch_shapes=[pltpu.VMEM((tu, D_MODEL), F32)],
    )
    return pl.pallas_call(
        kern,
        grid_spec=grid_spec,
        out_shape=jax.ShapeDtypeStruct((T, D_MODEL), F32),
        compiler_params=_cparams(("arbitrary",)),
        name="moe_combine",
    )(*tabs, pos_row, ye, aff, x2, final_g)


def _rope_tables(S):
    pos = jnp.arange(S, dtype=F32)
    inv = ROPE_BASE ** (-jnp.arange(0, A_ROPE, 2, dtype=F32) / A_ROPE)
    ang = pos[:, None] * inv[None, :]
    pad = jnp.zeros((S, LANES - A_ROPE), F32)
    cos = jnp.concatenate([jnp.cos(ang), jnp.cos(ang), pad], axis=1)
    sin = jnp.concatenate([jnp.sin(ang), jnp.sin(ang), pad], axis=1)
    return cos, sin


def _rotate_half_cols(w):
    half = A_ROPE // 2
    return jnp.concatenate([-w[..., half:], w[..., :half]], axis=-1)


def _prep_weights(norm_mix_g, w_in, b_gates, conv_w, conv_b, mh_norm_g, g_cq, g_ckv, w_uq, w_ukv,
                  w_br_a, w_br_b, w_out, norm_x_g, norm_mem_g, w_xq, w_xk, w_xv, w_xo, norm_ffn_g,
                  w_router, b_router, w_e_gate, w_e_up, w_e_down, final_norm_g):
    l = 0
    wi = w_in[l]
    o = 0
    cols = {}
    for name, n in (("qm", M_WIDTH), ("km", M_WIDTH), ("vm", M_WIDTH), ("om", M_WIDTH), ("gates", 4 * M_HEADS),
                    ("cq", A_Q_RANK), ("ckv", A_KV_RANK), ("kr", A_ROPE), ("ga", D_MODEL), ("gb", D_MODEL)):
        cols[name] = wi[:, o:o + n]
        o += n
    zpad = jnp.zeros((D_MODEL, LANES - A_ROPE), F32)
    w_kr = jnp.concatenate([cols["kr"], zpad, _rotate_half_cols(cols["kr"]), zpad], axis=1)
    uq = w_uq[l].reshape(A_Q_RANK, A_HEADS, A_NOPE + A_ROPE)
    uq_rope = uq[:, :, A_NOPE:]
    hpad = jnp.zeros((A_Q_RANK, A_HEADS, LANES - A_ROPE), F32)
    w_qa = jnp.concatenate([uq, hpad], axis=2).reshape(A_Q_RANK, A_HEADS * A_QK_PAD)
    w_qr = jnp.concatenate([_rotate_half_cols(uq_rope), hpad], axis=2).reshape(A_Q_RANK, A_HEADS * LANES)
    ukv = w_ukv[l].reshape(A_KV_RANK, A_HEADS, A_NOPE + A_V)
    row = lambda v: v.reshape(1, -1).astype(F32)
    return {
        "norm_mix_g": row(norm_mix_g[l]),
        "w_big": jnp.concatenate([cols[n] for n in ("qm", "km", "vm", "om", "ga", "gb")], axis=1).astype(BF16),
        "w_c": jnp.concatenate([cols["cq"], cols["ckv"]], axis=1).astype(BF16),
        "w_kr": w_kr.astype(BF16),
        "w_g": cols["gates"].astype(BF16),
        "w_gt": cols["gates"].T.astype(BF16),
        "b_g": row(b_gates[l]),
        "b_gt": b_gates[l].reshape(-1, 1).astype(F32),
        "conv_w": conv_w[l],
        "conv_b": row(conv_b[l]),
        "mh_norm_g": row(mh_norm_g[l]),
        "g_cq": row(g_cq[l]),
        "g_ckv": row(g_ckv[l]),
        "w_qat": w_qa.T.astype(BF16),
        "w_qrt": w_qr.T.astype(BF16),
        "w_uk": ukv[:, :, :A_NOPE].reshape(A_KV_RANK, A_HEADS * A_NOPE).astype(BF16),
        "w_uvt": ukv[:, :, A_NOPE:].reshape(A_KV_RANK, A_HEADS * A_V).T.astype(BF16),
        "w_br_a": w_br_a[l].astype(BF16),
        "w_br_b": w_br_b[l].astype(BF16),
        "w_out": w_out[l].astype(BF16),
        "norm_x_g": row(norm_x_g[l]),
        "norm_mem_g": row(norm_mem_g[l]),
        "w_xq": w_xq[l].astype(BF16),
        "w_xk": w_xk[l].astype(BF16),
        "w_xv": w_xv[l].astype(BF16),
        "w_xo": w_xo[l].astype(BF16),
        "norm_ffn_g": row(norm_ffn_g[l]),
        "w_router": w_router[l],
        "w_router_t": w_router[l].T,
        "b_router": row(b_router[l]),
        "b_router_t": b_router[l].reshape(-1, 1).astype(F32),
        "w_e_gate": w_e_gate[l].astype(BF16),
        "w_e_up": w_e_up[l].astype(BF16),
        "w_e_down": w_e_down[l].astype(BF16),
        "final_norm_g": row(final_norm_g),
    }


def _trunk(x, mem, w):
    B, S, _ = x.shape
    T = B * S
    n_mem = mem.shape[1]
    w = dict(w)
    w["cos"], w["sin"] = _rope_tables(S)
    w["cos_t"], w["sin_t"] = w["cos"].T, w["sin"].T
    x2d = x.reshape(T, D_MODEL)

    z, c, kr, gate, gatet = _inproj(x2d, S, w)
    qk = _conv(z, S, w)
    hf, hb = _mlstm(qk, z, gate, gatet, B, S)
    qc, kc, vc = _mla_proj(c, kr, B, S, w)
    att = _flash(qc, kc, vc)
    x1 = _mixer_out(hf, hb, z, att, x2d, w)

    kx, vx = _mem_proj(mem.reshape(B * n_mem, D_MODEL), w)
    x2, xn, aff, aff_t = _cross_router(x1, kx, vx, S, n_mem, w)

    cap = max(1, EC_FACTOR * T // N_EXPERTS)
    pos, bst = _select(aff_t, cap)
    tu = min(MOE_TOKEN_TILE, T)
    tc = min(MOE_SLOT_TILE, cap)
    gather_tabs, combine_tabs = _moe_tables(bst.reshape(N_EXPERTS, -1), cap, T, tu, tc)
    pos_row = pos.reshape(N_EXPERTS, 1, T)
    xe = _moe_gather(xn, pos_row, gather_tabs, cap, tu, tc)
    ye = _moe_ffn(xe, w)
    y = _moe_combine(ye, pos_row, aff, x2, combine_tabs, w["final_norm_g"], cap, tu, tc)
    return y.reshape(B, S, D_MODEL)


def kernel(x_prompt, x_sample, mem_prompt, mem_sample, norm_mix_g, w_in, b_gates, conv_w, conv_b, mh_norm_g, g_cq, g_ckv, w_uq, w_ukv, w_br_a, w_br_b, w_out, norm_x_g, norm_mem_g, w_xq, w_xk, w_xv, w_xo, norm_ffn_g, w_router, b_router, w_e_gate, w_e_up, w_e_down, final_norm_g):
    w = _prep_weights(norm_mix_g, w_in, b_gates, conv_w, conv_b, mh_norm_g, g_cq, g_ckv, w_uq, w_ukv,
                      w_br_a, w_br_b, w_out, norm_x_g, norm_mem_g, w_xq, w_xk, w_xv, w_xo, norm_ffn_g,
                      w_router, b_router, w_e_gate, w_e_up, w_e_down, final_norm_g)
    return (_trunk(x_prompt, mem_prompt, w), _trunk(x_sample, mem_sample, w))
```

```python
import functools

import jax
import jax.numpy as jnp
from jax import lax
from jax.experimental import pallas as pl
from jax.experimental.pallas import tpu as pltpu

F32 = jnp.float32
BF16 = jnp.bfloat16
I32 = jnp.int32

D_MODEL = 1024
M_WIDTH = 1024
M_HEADS = 4
M_HEAD_DIM = M_WIDTH // M_HEADS
M_CHUNK = 128
M_CONV = 5
A_HEADS = 8
A_NOPE = 128
A_ROPE = 64
A_V = 128
A_Q_RANK = 256
A_KV_RANK = 256
A_QK_PAD = 256
ROPE_BASE = 10000.0
X_HEADS = 4
X_HEAD_DIM = D_MODEL // X_HEADS
N_EXPERTS = 16
EC_FACTOR = 2
E_FF = 1024
NORM_EPS = 1e-6
LANES = 128
BIG_COLS = 6 * 1024
VMEM_LIMIT = 56 * 1024 * 1024
MOE_TOKEN_TILE = 1024
MOE_SLOT_TILE = 256


def _cparams(sem):
    return pltpu.CompilerParams(dimension_semantics=sem, vmem_limit_bytes=VMEM_LIMIT)


def _dot(a, b):
    return jnp.dot(a, b, preferred_element_type=F32)


def _dot_nt(a, b):
    return lax.dot_general(a, b, (((1,), (1,)), ((), ())), preferred_element_type=F32)


def _dot_tn(a, b):
    return lax.dot_general(a, b, (((0,), (0,)), ((), ())), preferred_element_type=F32)


def _rms(x, g):
    return x * lax.rsqrt(jnp.mean(x * x, axis=-1, keepdims=True) + NORM_EPS) * g


def _sigmoid(x):
    return 1.0 / (1.0 + jnp.exp(-x))


def _split3(x):
    hi = x.astype(BF16)
    r = x - hi.astype(F32)
    mid = r.astype(BF16)
    lo = (r - mid.astype(F32)).astype(BF16)
    return hi, mid, lo


def _inproj_kernel(x_ref, g_ref, wbig_ref, wc_ref, wkr_ref, wg_ref, wgt_ref, bg_ref, bgt_ref,
                   cos_ref, sin_ref,
                   z_ref, c_ref, kr_ref, gate_ref, gatet_ref, xn_scr):
    j = pl.program_id(1)

    @pl.when(j == 0)
    def _():
        xn = _rms(x_ref[...], g_ref[...]).astype(BF16)
        xn_scr[...] = xn
        c_ref[...] = _dot(xn, wc_ref[...]).astype(BF16)
        kr = _dot(xn, wkr_ref[...])
        kr_ref[...] = (kr[:, :LANES] * cos_ref[...] + kr[:, LANES:] * sin_ref[...]).astype(BF16)
        gate_ref[...] = _dot(xn, wg_ref[...]) + bg_ref[...]
        gatet_ref[...] = _dot_nt(wgt_ref[...], xn) + bgt_ref[...]

    z_ref[...] = _dot(xn_scr[...], wbig_ref[...]).astype(BF16)


def _inproj(x, S, w):
    T = x.shape[0]
    tm = min(1024, S)
    tn = 1024
    nseq = S // tm
    row = lambda i, j: (i, 0)
    const = lambda i, j: (0, 0)
    return pl.pallas_call(
        _inproj_kernel,
        grid=(T // tm, BIG_COLS // tn),
        in_specs=[
            pl.BlockSpec((tm, D_MODEL), row),
            pl.BlockSpec((1, D_MODEL), const),
            pl.BlockSpec((D_MODEL, tn), lambda i, j: (0, j)),
            pl.BlockSpec((D_MODEL, 512), const),
            pl.BlockSpec((D_MODEL, 256), const),
            pl.BlockSpec((D_MODEL, 16), const),
            pl.BlockSpec((16, D_MODEL), const),
            pl.BlockSpec((1, 16), const),
            pl.BlockSpec((16, 1), const),
            pl.BlockSpec((tm, LANES), lambda i, j: (i % nseq, 0)),
            pl.BlockSpec((tm, LANES), lambda i, j: (i % nseq, 0)),
        ],
        out_specs=[
            pl.BlockSpec((tm, tn), lambda i, j: (i, j)),
            pl.BlockSpec((tm, 512), row),
            pl.BlockSpec((tm, LANES), row),
            pl.BlockSpec((tm, 16), row),
            pl.BlockSpec((16, tm), lambda i, j: (0, i)),
        ],
        out_shape=[
            jax.ShapeDtypeStruct((T, BIG_COLS), BF16),
            jax.ShapeDtypeStruct((T, 512), BF16),
            jax.ShapeDtypeStruct((T, LANES), BF16),
            jax.ShapeDtypeStruct((T, 16), F32),
            jax.ShapeDtypeStruct((16, T), F32),
        ],
        scratch_shapes=[pltpu.VMEM((tm, D_MODEL), BF16)],
        compiler_params=_cparams(("parallel", "arbitrary")),
        name="inproj",
    )(x, w["norm_mix_g"], w["w_big"], w["w_c"], w["w_kr"], w["w_g"], w["w_gt"], w["b_g"], w["b_gt"],
      w["cos"], w["sin"])


CONV_HALO = 16


def _conv_kernel(z_ref, zp_ref, zn_ref, w_ref, b_ref, o_ref, scr, *, tr, tiles_per_seq, scale, transpose):
    it = pl.program_id(0) % tiles_per_seq
    keep_prev = jnp.where(it == 0, 0.0, 1.0)
    keep_next = jnp.where(it == tiles_per_seq - 1, 0.0, 1.0)
    scr[0:8, :] = zp_ref[...].astype(F32)[8:16, :] * keep_prev
    scr[8:8 + tr, :] = z_ref[...].astype(F32)
    scr[8 + tr:16 + tr, :] = zn_ref[...].astype(F32)[0:8, :] * keep_next
    acc = jnp.zeros((tr, scr.shape[1]), F32) + b_ref[...]
    for k in range(M_CONV):
        off = 8 - M_CONV // 2 + k
        acc = acc + w_ref[k:k + 1, :] * scr[off:off + tr, :]
    y = acc * _sigmoid(acc) * scale
    o_ref[...] = (y.T if transpose else y).astype(BF16)


def _conv(z, S, w, *, col0, scale, transpose):
    T = z.shape[0]
    tr = min(512, S)
    tcw = 512
    tiles_per_seq = S // tr
    hb = tr // CONV_HALO
    nhalo = T // CONV_HALO
    c0 = col0 * (M_WIDTH // tcw)
    kern = functools.partial(_conv_kernel, tr=tr, tiles_per_seq=tiles_per_seq, scale=scale, transpose=transpose)
    if transpose:
        out_spec = pl.BlockSpec((tcw, tr), lambda i, j: (j, i))
        out_shape = jax.ShapeDtypeStruct((M_WIDTH, T), BF16)
    else:
        out_spec = pl.BlockSpec((tr, tcw), lambda i, j: (i, j))
        out_shape = jax.ShapeDtypeStruct((T, M_WIDTH), BF16)
    return pl.pallas_call(
        kern,
        grid=(T // tr, M_WIDTH // tcw),
        in_specs=[
            pl.BlockSpec((tr, tcw), lambda i, j: (i, c0 + j)),
            pl.BlockSpec((CONV_HALO, tcw), lambda i, j: (jnp.maximum(i * hb - 1, 0), c0 + j)),
            pl.BlockSpec((CONV_HALO, tcw), lambda i, j: (jnp.minimum((i + 1) * hb, nhalo - 1), c0 + j)),
            pl.BlockSpec((M_CONV, tcw), lambda i, j: (0, c0 + j)),
            pl.BlockSpec((1, tcw), lambda i, j: (0, c0 + j)),
        ],
        out_specs=out_spec,
        out_shape=out_shape,
        scratch_shapes=[pltpu.VMEM((tr + 16, tcw), F32)],
        compiler_params=_cparams(("parallel", "parallel")),
        name="conv_silu_t" if transpose else "conv_silu",
    )(z, z, z, w["conv_w"], w["conv_b"])


def _log_sigmoid(x):
    return -(jnp.maximum(-x, 0.0) + jnp.log1p(jnp.exp(-jnp.abs(x))))


M_STATE_COLS = M_HEAD_DIM + LANES


def _mlstm_dir(d, q_ref, kt_ref, v_ref, g_ref, gt_ref, o_ref, c_scr, m_scr):
    L = M_CHUNK
    r = lax.broadcasted_iota(I32, (L, L), 0)
    c = lax.broadcasted_iota(I32, (L, L), 1)
    if d == 0:
        mask = c <= r
    else:
        mask = c >= r
    tri_col = jnp.where(mask, 1.0, 0.0).astype(BF16)
    tri_row = jnp.where(r <= c if d == 0 else r >= c, 1.0, 0.0).astype(BF16)
    g = g_ref[...]
    gt = gt_ref[...]
    b_row_all = sum(_dot(p, tri_row) for p in _split3(_log_sigmoid(gt)))
    sel_r = lax.broadcasted_iota(I32, (4 * M_HEADS, M_HEADS * LANES), 0)
    sel_h = lax.broadcasted_iota(I32, (4 * M_HEADS, M_HEADS * LANES), 1) // LANES
    pick_i = jnp.where(sel_r == d * 2 * M_HEADS + sel_h, 1.0, 0.0).astype(BF16)
    pick_f = jnp.where(sel_r == d * 2 * M_HEADS + M_HEADS + sel_h, 1.0, 0.0).astype(BF16)
    i_bc_all = sum(_dot(p, pick_i) for p in _split3(g))
    b_bc_all = sum(_dot(tri_col, _dot(p, pick_f).astype(BF16)) for p in _split3(_log_sigmoid(g)))
    ones_blk = jnp.ones((L, LANES), BF16)
    edge = L - 1 if d == 0 else 0
    for h in range(M_HEADS):
        idx = d * M_HEADS + h
        ci = d * 2 * M_HEADS + h
        cf = ci + M_HEADS
        i_bc = i_bc_all[:, h * LANES:(h + 1) * LANES]
        b_bc = b_bc_all[:, h * LANES:(h + 1) * LANES]
        i_row = gt[ci:ci + 1, :]
        b_row = b_row_all[cf:cf + 1, :]
        b_last = b_bc[edge:edge + 1, :]
        m_prev = m_scr[idx][0:1, :]
        a = b_bc + m_prev
        dm = jnp.where(mask, b_bc - b_row + i_row, -jnp.inf)
        m_t = jnp.maximum(a, jnp.max(dm, axis=1, keepdims=True))
        w_inter = jnp.exp(a - m_t)
        p = jnp.exp(dm - m_t)
        sl = slice(h * M_HEAD_DIM, (h + 1) * M_HEAD_DIM)
        q = q_ref[:, sl]
        kt = kt_ref[sl, :]
        v = v_ref[:, sl]
        s = _dot(q, kt) * p
        cst = c_scr[idx]
        ne = (_dot(s.astype(BF16), jnp.concatenate([v, ones_blk], axis=1))
              + jnp.concatenate([w_inter] * 3, axis=1) * _dot(q, cst.astype(BF16)))
        inv = 1.0 / jnp.maximum(jnp.abs(ne[:, M_HEAD_DIM:]), jnp.exp(-m_t))
        o_ref[:, sl] = ne[:, :M_HEAD_DIM] * jnp.concatenate([inv, inv], axis=1)
        gk = b_last - b_bc + i_bc
        m_new = jnp.maximum(b_last + m_prev, jnp.max(gk, axis=0, keepdims=True))
        decay = jnp.exp(b_last + m_prev - m_new)
        wk = jnp.exp(gk - m_new)
        wv = jnp.concatenate([(jnp.concatenate([wk, wk], axis=1) * v.astype(F32)).astype(BF16),
                              wk.astype(BF16)], axis=1)
        c_scr[idx] = jnp.concatenate([decay] * 3, axis=1) * cst + _dot(kt, wv)
        m_scr[idx] = jnp.broadcast_to(m_new, (8, LANES))


def _mlstm_kernel(qf, ktf, vf, gf, gtf, qb, ktb, vb, gb, gtb, of, ob, c_scr, m_scr):
    @pl.when(pl.program_id(1) == 0)
    def _():
        c_scr[...] = jnp.zeros(c_scr.shape, F32)
        m_scr[...] = jnp.zeros(m_scr.shape, F32)

    _mlstm_dir(0, qf, ktf, vf, gf, gtf, of, c_scr, m_scr)
    _mlstm_dir(1, qb, ktb, vb, gb, gtb, ob, c_scr, m_scr)


def _mlstm(q, kt, z, gate, gatet, B, S):
    T = B * S
    L = M_CHUNK
    nc = S // L
    fwd = lambda b, c: b * nc + c
    bwd = lambda b, c: b * nc + nc - 1 - c

    def specs(pos):
        return [
            pl.BlockSpec((L, M_WIDTH), lambda b, c: (pos(b, c), 0)),
            pl.BlockSpec((M_WIDTH, L), lambda b, c: (0, pos(b, c))),
            pl.BlockSpec((L, M_WIDTH), lambda b, c: (pos(b, c), 2)),
            pl.BlockSpec((L, 16), lambda b, c: (pos(b, c), 0)),
            pl.BlockSpec((16, L), lambda b, c: (0, pos(b, c))),
        ]

    nstate = 2 * M_HEADS
    return pl.pallas_call(
        _mlstm_kernel,
        grid=(B, nc),
        in_specs=specs(fwd) + specs(bwd),
        out_specs=[
            pl.BlockSpec((L, M_WIDTH), lambda b, c: (fwd(b, c), 0)),
            pl.BlockSpec((L, M_WIDTH), lambda b, c: (bwd(b, c), 0)),
        ],
        out_shape=[jax.ShapeDtypeStruct((T, M_WIDTH), F32)] * 2,
        scratch_shapes=[
            pltpu.VMEM((nstate, M_HEAD_DIM, M_STATE_COLS), F32),
            pltpu.VMEM((nstate, 8, LANES), F32),
        ],
        compiler_params=_cparams(("parallel", "arbitrary")),
        name="mlstm",
    )(q, kt, z, gate, gatet, q, kt, z, gate, gatet)


ATT_TILE = 512


def _flash_tiles(S):
    tk = min(ATT_TILE, S)
    nk = S // tk
    tq = min(1024 if nk <= 8 else 512, S)
    unroll = 8 if nk >= 16 else 4
    return tq, tk, unroll
ATT_LOGIT_SCALE =(A_NOPE + A_ROPE) ** -0.5 * 1.4426950408889634


def _mla_proj_kernel(c_ref, kr_ref, cost_ref, sint_ref, gq_ref, gkv_ref, wqat_ref, wqrt_ref, wk_ref, wvt_ref,
                     qt_ref, k_ref, vt_ref):
    cq = c_ref[:, :A_Q_RANK].astype(F32)
    ckv = c_ref[:, A_Q_RANK:].astype(F32)
    cqn = _rms(cq, gq_ref[...]).astype(BF16)
    ckvn = _rms(ckv, gkv_ref[...]).astype(BF16)
    qat = _dot_nt(wqat_ref[...], cqn)
    qrt = _dot_nt(wqrt_ref[...], cqn)
    kn = _dot(ckvn, wk_ref[...])
    vt = _dot_nt(wvt_ref[...], ckvn)
    cost = cost_ref[...]
    sint = sint_ref[...]
    kr = kr_ref[...]
    for h in range(A_HEADS):
        o = h * A_QK_PAD
        qt_ref[0, h, :LANES, :] = (qat[o:o + LANES, :] * ATT_LOGIT_SCALE).astype(BF16)
        qt_ref[0, h, LANES:, :] = ((qat[o + LANES:o + 2 * LANES, :] * cost
                                    + qrt[h * LANES:(h + 1) * LANES, :] * sint) * ATT_LOGIT_SCALE).astype(BF16)
        k_ref[0, h, :, :LANES] = kn[:, h * A_NOPE:(h + 1) * A_NOPE].astype(BF16)
        k_ref[0, h, :, LANES:] = kr
        vt_ref[0, h, 0] = vt[h * A_V:(h + 1) * A_V, :].astype(BF16)


def _mla_proj(c, kr, B, S, w):
    tm = min(ATT_TILE, S)
    nseq = S // tm
    row = lambda i: (i, 0)
    const = lambda i: (0, 0)
    seq_t = lambda i: (0, i % nseq)
    return pl.pallas_call(
        _mla_proj_kernel,
        grid=(B * S // tm,),
        in_specs=[
            pl.BlockSpec((tm, 512), row),
            pl.BlockSpec((tm, LANES), row),
            pl.BlockSpec((LANES, tm), seq_t),
            pl.BlockSpec((LANES, tm), seq_t),
            pl.BlockSpec((1, A_Q_RANK), const),
            pl.BlockSpec((1, A_KV_RANK), const),
            pl.BlockSpec((A_HEADS * A_QK_PAD, A_Q_RANK), const),
            pl.BlockSpec((A_HEADS * LANES, A_Q_RANK), const),
            pl.BlockSpec((A_KV_RANK, A_HEADS * A_NOPE), const),
            pl.BlockSpec((A_HEADS * A_V, A_KV_RANK), const),
        ],
        out_specs=[
            pl.BlockSpec((1, A_HEADS, A_QK_PAD, tm), lambda i: (i // nseq, 0, 0, i % nseq)),
            pl.BlockSpec((1, A_HEADS, tm, A_QK_PAD), lambda i: (i // nseq, 0, i % nseq, 0)),
            pl.BlockSpec((1, A_HEADS, 1, A_V, tm), lambda i: (i // nseq, 0, i % nseq, 0, 0)),
        ],
        out_shape=[
            jax.ShapeDtypeStruct((B, A_HEADS, A_QK_PAD, S), BF16),
            jax.ShapeDtypeStruct((B, A_HEADS, S, A_QK_PAD), BF16),
            jax.ShapeDtypeStruct((B, A_HEADS, nseq, A_V, tm), BF16),
        ],
        compiler_params=_cparams(("parallel",)),
        name="mla_proj",
    )(c, kr, w["cos_t"], w["sin_t"], w["g_cq"], w["g_ckv"], w["w_qat"], w["w_qrt"], w["w_uk"], w["w_uvt"])


def _flash_kernel(q_ref, k_ref, vt_ref, o_ref, s_scr, *, tk, nk, unroll):
    qt = q_ref[0, 0]
    tq = qt.shape[1]

    def scores(j, slot):
        start = pl.multiple_of(j * tk, tk)
        s_scr[slot] = _dot(k_ref[0, 0, pl.ds(start, tk), :], qt)

    def accumulate(j, slot, carry):
        m, l, acc = carry
        s = s_scr[slot]
        m_new = jnp.maximum(m, jnp.max(s, axis=0, keepdims=True))
        p = jnp.exp2(s - m_new)
        alpha = jnp.exp2(m - m_new)
        l = alpha * l + jnp.sum(p, axis=0, keepdims=True)
        acc = alpha * acc + _dot(vt_ref[0, 0, j], p.astype(BF16))
        return m_new, l, acc

    def body(jj, carry):
        for t in range(unroll):
            j = unroll * jj + t
            scores(jnp.minimum(j + 1, nk - 1), (t + 1) % 2)
            carry = accumulate(j, t % 2, carry)
        return carry

    carry = (jnp.full((1, tq), -jnp.inf, F32), jnp.zeros((1, tq), F32), jnp.zeros((A_V, tq), F32))
    scores(0, 0)
    n_loop = nk // unroll
    if n_loop > 0:
        carry = lax.fori_loop(0, n_loop, body, carry)
    for j in range(unroll * n_loop, nk):
        if j + 1 < nk:
            scores(j + 1, (j + 1) % 2)
        carry = accumulate(j, j % 2, carry)
    _, l, acc = carry
    o_ref[...] = (acc / l).T.astype(BF16)


def _flash(qt, k, vt):
    B, H, S, _ = k.shape
    tq, tk, unroll = _flash_tiles(S)
    nq = S // tq
    nk = S // tk
    kern = functools.partial(_flash_kernel, tk=tk, nk=nk, unroll=unroll)
    return pl.pallas_call(
        kern,
        grid=(B, H, nq),
        in_specs=[
            pl.BlockSpec((1, 1, A_QK_PAD, tq), lambda b, h, i: (b, h, 0, i)),
            pl.BlockSpec((1, 1, S, A_QK_PAD), lambda b, h, i: (b, h, 0, 0)),
            pl.BlockSpec((1, 1, nk, A_V, tk), lambda b, h, i: (b, h, 0, 0, 0)),
        ],
        out_specs=pl.BlockSpec((tq, A_V), lambda b, h, i: (b * nq + i, h)),
        out_shape=jax.ShapeDtypeStruct((B * S, A_HEADS * A_V), BF16),
        scratch_shapes=[pltpu.VMEM((2, tk, tq), F32)],
        compiler_params=_cparams(("parallel", "parallel", "arbitrary")),
        name="mla_flash",
    )(qt, k, vt)


def _mixer_out_kernel(hf_ref, hb_ref, om_ref, ga_ref, gb_ref, att_ref, x_ref, mg_ref,
                      wa_ref, wb_ref, wo_ref, o_ref):
    hs = hf_ref[...] + hb_ref[...]
    mg = mg_ref[...]
    parts = []
    for h in range(M_HEADS):
        sl = slice(h * M_HEAD_DIM, (h + 1) * M_HEAD_DIM)
        parts.append(_rms(hs[:, sl], mg[:, sl]))
    hn = jnp.concatenate(parts, axis=1) * _sigmoid(om_ref[...].astype(F32))
    y_a = _dot(hn.astype(BF16), wa_ref[...])
    y_b = _dot(att_ref[...], wb_ref[...])
    merged = _sigmoid(ga_ref[...].astype(F32)) * y_a + _sigmoid(gb_ref[...].astype(F32)) * y_b
    o_ref[...] = x_ref[...] + _dot(merged.astype(BF16), wo_ref[...])


def _mixer_out(hf, hb, z, att, x, w):
    T = x.shape[0]
    tm = min(512, T)
    row = lambda i: (i, 0)
    const = lambda i: (0, 0)
    wspec = pl.BlockSpec((D_MODEL, D_MODEL), const)
    return pl.pallas_call(
        _mixer_out_kernel,
        grid=(T // tm,),
        in_specs=[
            pl.BlockSpec((tm, M_WIDTH), row),
            pl.BlockSpec((tm, M_WIDTH), row),
            pl.BlockSpec((tm, M_WIDTH), lambda i: (i, 3)),
            pl.BlockSpec((tm, D_MODEL), lambda i: (i, 4)),
            pl.BlockSpec((tm, D_MODEL), lambda i: (i, 5)),
            pl.BlockSpec((tm, D_MODEL), row),
            pl.BlockSpec((tm, D_MODEL), row),
            pl.BlockSpec((1, M_WIDTH), const),
            wspec, wspec, wspec,
        ],
        out_specs=pl.BlockSpec((tm, D_MODEL), row),
        out_shape=jax.ShapeDtypeStruct((T, D_MODEL), F32),
        compiler_params=_cparams(("parallel",)),
        name="mixer_out",
    )(hf, hb, z, z, z, att, x, w["mh_norm_g"], w["w_br_a"], w["w_br_b"], w["w_out"])


def _mem_kernel(m_ref, g_ref, wk_ref, wv_ref, k_ref, v_ref):
    mn = _rms(m_ref[...], g_ref[...]).astype(BF16)
    k_ref[...] = _dot(mn, wk_ref[...]).astype(BF16)
    v_ref[...] = _dot(mn, wv_ref[...]).astype(BF16)


def _mem_proj(mem, w):
    R = mem.shape[0]
    tm = 256
    row = lambda i: (i, 0)
    const = lambda i: (0, 0)
    wspec = pl.BlockSpec((D_MODEL, D_MODEL), const)
    return pl.pallas_call(
        _mem_kernel,
        grid=(R // tm,),
        in_specs=[pl.BlockSpec((tm, D_MODEL), row), pl.BlockSpec((1, D_MODEL), const), wspec, wspec],
        out_specs=[pl.BlockSpec((tm, D_MODEL), row)] * 2,
        out_shape=[jax.ShapeDtypeStruct((R, D_MODEL), BF16)] * 2,
        compiler_params=_cparams(("parallel",)),
        name="mem_proj",
    )(mem, w["norm_mem_g"], w["w_xk"], w["w_xv"])


def _cross_router_kernel(x_ref, kx_ref, vx_ref, gx_ref, gf_ref, wq_ref, wo_ref, wr_ref, wrt_ref,
                         br_ref, brt_ref, x2_ref, xn_ref, aff_ref, afft_ref):
    x1 = x_ref[...]
    xn = _rms(x1, gx_ref[...]).astype(BF16)
    q = _dot(xn, wq_ref[...])
    outs = []
    for h in range(X_HEADS):
        sl = slice(h * X_HEAD_DIM, (h + 1) * X_HEAD_DIM)
        s = _dot_nt(q[:, sl].astype(BF16), kx_ref[:, sl]) * (X_HEAD_DIM ** -0.5)
        e = jnp.exp(s - jnp.max(s, axis=1, keepdims=True))
        p = e / jnp.sum(e, axis=1, keepdims=True)
        outs.append(_dot(p.astype(BF16), vx_ref[:, sl]))
    o = jnp.concatenate(outs, axis=1)
    x2 = x1 + _dot(o.astype(BF16), wo_ref[...])
    x2_ref[...] = x2
    xf = _rms(x2, gf_ref[...])
    hi = xf.astype(BF16)
    xn_ref[...] = hi
    lo = (xf - hi.astype(F32)).astype(BF16)
    wr = wr_ref[...]
    wr_hi = wr.astype(BF16)
    wr_lo = (wr - wr_hi.astype(F32)).astype(BF16)
    logit = _dot(hi, wr_hi) + _dot(lo, wr_hi) + _dot(hi, wr_lo) + br_ref[...]
    e = jnp.exp(logit - jnp.max(logit, axis=1, keepdims=True))
    aff_ref[...] = e / jnp.sum(e, axis=1, keepdims=True)
    wrt = wrt_ref[...]
    wrt_hi = wrt.astype(BF16)
    wrt_lo = (wrt - wrt_hi.astype(F32)).astype(BF16)
    logit_t = _dot_nt(wrt_hi, hi) + _dot_nt(wrt_hi, lo) + _dot_nt(wrt_lo, hi) + brt_ref[...]
    et = jnp.exp(logit_t - jnp.max(logit_t, axis=0, keepdims=True))
    afft_ref[...] = et / jnp.sum(et, axis=0, keepdims=True)


def _cross_router(x1, kx, vx, S, n_mem, w):
    T = x1.shape[0]
    tm = min(512, S)
    nseq = S // tm
    row = lambda i: (i, 0)
    const = lambda i: (0, 0)
    wspec = pl.BlockSpec((D_MODEL, D_MODEL), const)
    memspec = pl.BlockSpec((n_mem, D_MODEL), lambda i: (i // nseq, 0))
    return pl.pallas_call(
        _cross_router_kernel,
        grid=(T // tm,),
        in_specs=[
            pl.BlockSpec((tm, D_MODEL), row), memspec, memspec,
            pl.BlockSpec((1, D_MODEL), const), pl.BlockSpec((1, D_MODEL), const),
            wspec, wspec,
            pl.BlockSpec((D_MODEL, N_EXPERTS), const), pl.BlockSpec((N_EXPERTS, D_MODEL), const),
            pl.BlockSpec((1, N_EXPERTS), const), pl.BlockSpec((N_EXPERTS, 1), const),
        ],
        out_specs=[
            pl.BlockSpec((tm, D_MODEL), row),
            pl.BlockSpec((tm, D_MODEL), row),
            pl.BlockSpec((tm, N_EXPERTS), row),
            pl.BlockSpec((N_EXPERTS, tm), lambda i: (0, i)),
        ],
        out_shape=[
            jax.ShapeDtypeStruct((T, D_MODEL), F32),
            jax.ShapeDtypeStruct((T, D_MODEL), BF16),
            jax.ShapeDtypeStruct((T, N_EXPERTS), F32),
            jax.ShapeDtypeStruct((N_EXPERTS, T), F32),
        ],
        compiler_params=_cparams(("parallel",)),
        name="cross_router",
    )(x1, kx, vx, w["norm_x_g"], w["norm_ffn_g"], w["w_xq"], w["w_xo"], w["w_router"], w["w_router_t"],
      w["b_router"], w["b_router_t"])


def _excl_cumsum(mask_f, strict_lane, strict_blk):
    nb = mask_f.shape[0]
    within = _dot(mask_f.astype(BF16), strict_lane)
    tot = jnp.sum(mask_f, axis=1, keepdims=True)
    bstart = _dot(strict_blk, jnp.broadcast_to(tot, (nb, LANES)).astype(BF16))
    return within + bstart, bstart


def _select_kernel(aff_ref, pos_ref, bst_ref, *, cap):
    a = aff_ref[0]
    nb = a.shape[0]
    bits = pltpu.bitcast(a, I32)

    def radix(i, prefix):
        cand = prefix | jnp.left_shift(jnp.int32(1), 30 - i)
        cnt = jnp.sum(jnp.where(bits >= cand, 1.0, 0.0), axis=(0, 1), keepdims=True)
        return jnp.where(cnt >= cap, cand, prefix)

    thr = lax.fori_loop(0, 31, radix, jnp.zeros((1, 1), I32))
    gt = bits > thr
    eq = bits == thr
    need = cap - jnp.sum(jnp.where(gt, 1.0, 0.0), axis=(0, 1), keepdims=True)
    li = lax.broadcasted_iota(I32, (LANES, LANES), 0)
    lj = lax.broadcasted_iota(I32, (LANES, LANES), 1)
    strict_lane = jnp.where(li < lj, 1.0, 0.0).astype(BF16)
    bi = lax.broadcasted_iota(I32, (nb, nb), 0)
    bj = lax.broadcasted_iota(I32, (nb, nb), 1)
    strict_blk = jnp.where(bj < bi, 1.0, 0.0).astype(BF16)
    rank, _ = _excl_cumsum(jnp.where(eq, 1.0, 0.0), strict_lane, strict_blk)
    sel = gt | (eq & (rank < need))
    pos, bstart = _excl_cumsum(jnp.where(sel, 1.0, 0.0), strict_lane, strict_blk)
    pos_ref[0] = jnp.where(sel, pos.astype(I32), -1)
    col = jnp.broadcast_to(bstart[:, 0:1], (nb, nb))
    bst_ref[0] = jnp.sum(jnp.where(bi == bj, col, 0.0), axis=0, keepdims=True).astype(I32)


def _select(aff_t, cap):
    E, T = aff_t.shape
    nb = T // LANES
    kern = functools.partial(_select_kernel, cap=cap)
    return pl.pallas_call(
        kern,
        grid=(E,),
        in_specs=[pl.BlockSpec((1, nb, LANES), lambda e: (e, 0, 0))],
        out_specs=[
            pl.BlockSpec((1, nb, LANES), lambda e: (e, 0, 0)),
            pl.BlockSpec((1, 1, nb), lambda e: (e, 0, 0)),
        ],
        out_shape=[
            jax.ShapeDtypeStruct((E, nb, LANES), I32),
            jax.ShapeDtypeStruct((E, 1, nb), I32),
        ],
        compiler_params=_cparams(("parallel",)),
        name="ec_select",
    )(aff_t.reshape(E, nb, LANES))


def _moe_tables(bst, cap, T, tu, tc):
    E = bst.shape[0]
    nu = T // tu
    ns = cap // tc
    start = bst[:, ::tu // LANES]
    end = jnp.concatenate([start[:, 1:], jnp.full((E, 1), cap, I32)], axis=1)
    cnt = end - start
    s_lo = jnp.minimum(start // tc, ns - 1)
    s_hi = jnp.where(cnt > 0, (end - 1) // tc, s_lo)
    ncell = jnp.where(cnt > 0, s_hi - s_lo + 1, 0)

    def enumerate_cells(nc_flat, slo_flat, steps):
        off_end = jnp.cumsum(nc_flat, axis=-1)
        off = off_end - nc_flat
        total = off_end[..., -1:]
        k = jnp.arange(steps, dtype=I32)
        kk = jnp.minimum(k, total - 1)
        grp = jnp.sum((off_end[..., None, :] <= kk[..., :, None]).astype(I32), axis=-1)
        s = jnp.take_along_axis(slo_flat, grp, axis=-1) + kk - jnp.take_along_axis(off, grp, axis=-1)
        valid = (k < total).astype(I32)
        return grp, s, valid

    pg = nu + ns
    g_u, g_s, g_valid = enumerate_cells(ncell, s_lo, pg)
    prev_s = jnp.concatenate([jnp.full((E, 1), -1, I32), g_s[:, :-1]], axis=1)
    g_first = g_valid * (g_s != prev_s).astype(I32)
    gather_tabs = tuple(t.reshape(-1) for t in (g_u, g_s, g_valid, g_first))

    ncell_c = ncell.at[0].set(jnp.maximum(ncell[0], 1))
    pc = E * (nu + ns)
    c_grp, c_s, c_valid = enumerate_cells(ncell_c.T.reshape(-1), s_lo.T.reshape(-1), pc)
    c_u = c_grp // E
    c_e = c_grp % E
    prev_u = jnp.concatenate([jnp.full((1,), -1, I32), c_u[:-1]])
    next_u = jnp.concatenate([c_u[1:], jnp.full((1,), -1, I32)])
    next_valid = jnp.concatenate([c_valid[1:], jnp.zeros((1,), I32)])
    c_first = c_valid * (c_u != prev_u).astype(I32)
    c_last = c_valid * jnp.maximum((c_u != next_u).astype(I32), 1 - next_valid)
    combine_tabs = (c_u, c_e, c_s, c_valid, c_first, c_last)
    return gather_tabs, combine_tabs


def _gather_kernel(u_tab, s_tab, valid_tab, first_tab, pos_ref, x_ref, o_ref, *, steps, tc):
    step = pl.program_id(0) * steps + pl.program_id(1)
    s = s_tab[step]
    tu = pos_ref.shape[-1]
    slot = lax.broadcasted_iota(I32, (tc, tu), 0) + s * tc
    onehot = jnp.where(pos_ref[0] == slot, 1.0, 0.0).astype(BF16)

    @pl.when(first_tab[step] == 1)
    def _():
        o_ref[0] = _dot(onehot, x_ref[...]).astype(BF16)

    @pl.when((valid_tab[step] == 1) & (first_tab[step] == 0))
    def _():
        o_ref[0] = o_ref[0] + _dot(onehot, x_ref[...]).astype(BF16)


def _moe_gather(xn, pos_row, tabs, cap, tu, tc):
    T = xn.shape[0]
    E = N_EXPERTS
    steps = T // tu + cap // tc
    kern = functools.partial(_gather_kernel, steps=steps, tc=tc)
    grid_spec = pltpu.PrefetchScalarGridSpec(
        num_scalar_prefetch=4,
        grid=(E, steps),
        in_specs=[
            pl.BlockSpec((1, 1, tu), lambda e, k, u, s, v, f: (e, 0, u[e * steps + k])),
            pl.BlockSpec((tu, D_MODEL), lambda e, k, u, s, v, f: (u[e * steps + k], 0)),
        ],
        out_specs=pl.BlockSpec((1, tc, D_MODEL), lambda e, k, u, s, v, f: (e, s[e * steps + k], 0)),
    )
    return pl.pallas_call(
        kern,
        grid_spec=grid_spec,
        out_shape=jax.ShapeDtypeStruct((E, cap, D_MODEL), BF16),
        compiler_params=_cparams(("parallel", "arbitrary")),
        name="moe_gather",
    )(*tabs, pos_row, xn)


def _ffn_kernel(x_ref, wg_ref, wu_ref, wd_ref, o_ref):
    x = x_ref[0]
    g = _dot(x, wg_ref[0])
    u = _dot(x, wu_ref[0])
    h = (g * _sigmoid(g) * u).astype(BF16)
    o_ref[0] = _dot(h, wd_ref[0]).astype(BF16)


def _moe_ffn(xe, w):
    E, cap, _ = xe.shape
    tf = min(512, cap)
    wspec = pl.BlockSpec((1, D_MODEL, E_FF), lambda e, i: (e, 0, 0))
    return pl.pallas_call(
        _ffn_kernel,
        grid=(E, cap // tf),
        in_specs=[pl.BlockSpec((1, tf, D_MODEL), lambda e, i: (e, i, 0)), wspec, wspec,
                  pl.BlockSpec((1, E_FF, D_MODEL), lambda e, i: (e, 0, 0))],
        out_specs=pl.BlockSpec((1, tf, D_MODEL), lambda e, i: (e, i, 0)),
        out_shape=jax.ShapeDtypeStruct((E, cap, D_MODEL), BF16),
        compiler_params=_cparams(("parallel", "arbitrary")),
        name="moe_ffn",
    )(xe, w["w_e_gate"], w["w_e_up"], w["w_e_down"])


def _combine_kernel(u_tab, e_tab, s_tab, valid_tab, first_tab, last_tab,
                    pos_ref, ye_ref, aff_ref, x_ref, g_ref, o_ref, acc, *, tc):
    step = pl.program_id(0)
    e = e_tab[step]
    s = s_tab[step]
    tu = pos_ref.shape[-1]

    @pl.when(first_tab[step] == 1)
    def _():
        acc[...] = jnp.zeros(acc.shape, F32)

    @pl.when(valid_tab[step] == 1)
    def _():
        slot = lax.broadcasted_iota(I32, (tc, tu), 0) + s * tc
        onehot = jnp.where(pos_ref[0] == slot, 1.0, 0.0).astype(BF16)
        contrib = _dot_tn(onehot, ye_ref[0])
        lane = lax.broadcasted_iota(I32, aff_ref.shape, 1)
        gate = jnp.sum(jnp.where(lane == e, aff_ref[...], 0.0), axis=1, keepdims=True)
        acc[...] = acc[...] + gate * contrib

    @pl.when(last_tab[step] == 1)
    def _():
        o_ref[...] = _rms(x_ref[...] + acc[...], g_ref[...])


def _moe_combine(ye, pos_row, aff, x2, tabs, final_g, cap, tu, tc):
    T = x2.shape[0]
    E = N_EXPERTS
    steps = E * (T // tu + cap // tc)
    kern = functools.partial(_combine_kernel, tc=tc)
    grid_spec = pltpu.PrefetchScalarGridSpec(
        num_scalar_prefetch=6,
        grid=(steps,),
        in_specs=[
            pl.BlockSpec((1, 1, tu), lambda k, u, e, s, v, f, l: (e[k], 0, u[k])),
            pl.BlockSpec((1, tc, D_MODEL), lambda k, u, e, s, v, f, l: (e[k], s[k], 0)),
            pl.BlockSpec((tu, N_EXPERTS), lambda k, u, e, s, v, f, l: (u[k], 0)),
            pl.BlockSpec((tu, D_MODEL), lambda k, u, e, s, v, f, l: (u[k], 0)),
            pl.BlockSpec((1, D_MODEL), lambda k, u, e, s, v, f, l: (0, 0)),
        ],
        out_specs=pl.BlockSpec((tu, D_MODEL), lambda k, u, e, s, v, f, l: (u[k], 0)),
        scratch_shapes=[pltpu.VMEM((tu, D_MODEL), F32)],
    )
    return pl.pallas_call(
        kern,
        grid_spec=grid_spec,
        out_shape=jax.ShapeDtypeStruct((T, D_MODEL), F32),
        compiler_params=_cparams(("arbitrary",)),
        name="moe_combine",
    )(*tabs, pos_row, ye, aff, x2, final_g)


def _rope_tables(S):
    pos = jnp.arange(S, dtype=F32)
    inv = ROPE_BASE ** (-jnp.arange(0, A_ROPE, 2, dtype=F32) / A_ROPE)
    ang = pos[:, None] * inv[None, :]
    pad = jnp.zeros((S, LANES - A_ROPE), F32)
    cos = jnp.concatenate([jnp.cos(ang), jnp.cos(ang), pad], axis=1)
    sin = jnp.concatenate([jnp.sin(ang), jnp.sin(ang), pad], axis=1)
    return cos, sin


def _rotate_half_cols(w):
    half = A_ROPE // 2
    return jnp.concatenate([-w[..., half:], w[..., :half]], axis=-1)


def _prep_weights(norm_mix_g, w_in, b_gates, conv_w, conv_b, mh_norm_g, g_cq, g_ckv, w_uq, w_ukv,
                  w_br_a, w_br_b, w_out, norm_x_g, norm_mem_g, w_xq, w_xk, w_xv, w_xo, norm_ffn_g,
                  w_router, b_router, w_e_gate, w_e_up, w_e_down, final_norm_g):
    l = 0
    wi = w_in[l]
    o = 0
    cols = {}
    for name, n in (("qm", M_WIDTH), ("km", M_WIDTH), ("vm", M_WIDTH), ("om", M_WIDTH), ("gates", 4 * M_HEADS),
                    ("cq", A_Q_RANK), ("ckv", A_KV_RANK), ("kr", A_ROPE), ("ga", D_MODEL), ("gb", D_MODEL)):
        cols[name] = wi[:, o:o + n]
        o += n
    zpad = jnp.zeros((D_MODEL, LANES - A_ROPE), F32)
    w_kr = jnp.concatenate([cols["kr"], zpad, _rotate_half_cols(cols["kr"]), zpad], axis=1)
    uq = w_uq[l].reshape(A_Q_RANK, A_HEADS, A_NOPE + A_ROPE)
    uq_rope = uq[:, :, A_NOPE:]
    hpad = jnp.zeros((A_Q_RANK, A_HEADS, LANES - A_ROPE), F32)
    w_qa = jnp.concatenate([uq, hpad], axis=2).reshape(A_Q_RANK, A_HEADS * A_QK_PAD)
    w_qr = jnp.concatenate([_rotate_half_cols(uq_rope), hpad], axis=2).reshape(A_Q_RANK, A_HEADS * LANES)
    ukv = w_ukv[l].reshape(A_KV_RANK, A_HEADS, A_NOPE + A_V)
    row = lambda v: v.reshape(1, -1).astype(F32)
    return {
        "norm_mix_g": row(norm_mix_g[l]),
        "w_big": jnp.concatenate([cols[n] for n in ("qm", "km", "vm", "om", "ga", "gb")], axis=1).astype(BF16),
        "w_c": jnp.concatenate([cols["cq"], cols["ckv"]], axis=1).astype(BF16),
        "w_kr": w_kr.astype(BF16),
        "w_g": cols["gates"].astype(BF16),
        "w_gt": cols["gates"].T.astype(BF16),
        "b_g": row(b_gates[l]),
        "b_gt": b_gates[l].reshape(-1, 1).astype(F32),
        "conv_w": conv_w[l],
        "conv_b": row(conv_b[l]),
        "mh_norm_g": row(mh_norm_g[l]),
        "g_cq": row(g_cq[l]),
        "g_ckv": row(g_ckv[l]),
        "w_qat": w_qa.T.astype(BF16),
        "w_qrt": w_qr.T.astype(BF16),
        "w_uk": ukv[:, :, :A_NOPE].reshape(A_KV_RANK, A_HEADS * A_NOPE).astype(BF16),
        "w_uvt": ukv[:, :, A_NOPE:].reshape(A_KV_RANK, A_HEADS * A_V).T.astype(BF16),
        "w_br_a": w_br_a[l].astype(BF16),
        "w_br_b": w_br_b[l].astype(BF16),
        "w_out": w_out[l].astype(BF16),
        "norm_x_g": row(norm_x_g[l]),
        "norm_mem_g": row(norm_mem_g[l]),
        "w_xq": w_xq[l].astype(BF16),
        "w_xk": w_xk[l].astype(BF16),
        "w_xv": w_xv[l].astype(BF16),
        "w_xo": w_xo[l].astype(BF16),
        "norm_ffn_g": row(norm_ffn_g[l]),
        "w_router": w_router[l],
        "w_router_t": w_router[l].T,
        "b_router": row(b_router[l]),
        "b_router_t": b_router[l].reshape(-1, 1).astype(F32),
        "w_e_gate": w_e_gate[l].astype(BF16),
        "w_e_up": w_e_up[l].astype(BF16),
        "w_e_down": w_e_down[l].astype(BF16),
        "final_norm_g": row(final_norm_g),
    }


def _trunk(x, mem, w):
    B, S, _ = x.shape
    T = B * S
    n_mem = mem.shape[1]
    w = dict(w)
    w["cos"], w["sin"] = _rope_tables(S)
    w["cos_t"], w["sin_t"] = w["cos"].T, w["sin"].T
    x2d = x.reshape(T, D_MODEL)

    z, c, kr, gate, gatet = _inproj(x2d, S, w)
    qm = _conv(z, S, w, col0=0, scale=M_HEAD_DIM ** -0.5, transpose=False)
    kmt = _conv(z, S, w, col0=1, scale=1.0, transpose=True)
    hf, hb = _mlstm(qm, kmt, z, gate, gatet, B, S)
    qc, kc, vc = _mla_proj(c, kr, B, S, w)
    att = _flash(qc, kc, vc)
    x1 = _mixer_out(hf, hb, z, att, x2d, w)

    kx, vx = _mem_proj(mem.reshape(B * n_mem, D_MODEL), w)
    x2, xn, aff, aff_t = _cross_router(x1, kx, vx, S, n_mem, w)

    cap = max(1, EC_FACTOR * T // N_EXPERTS)
    pos, bst = _select(aff_t, cap)
    tu = min(MOE_TOKEN_TILE, T)
    tc = min(MOE_SLOT_TILE, cap)
    gather_tabs, combine_tabs = _moe_tables(bst.reshape(N_EXPERTS, -1), cap, T, tu, tc)
    pos_row = pos.reshape(N_EXPERTS, 1, T)
    xe = _moe_gather(xn, pos_row, gather_tabs, cap, tu, tc)
    ye = _moe_ffn(xe, w)
    y = _moe_combine(ye, pos_row, aff, x2, combine_tabs, w["final_norm_g"], cap, tu, tc)
    return y.reshape(B, S, D_MODEL)


def kernel(x_prompt, x_sample, mem_prompt, mem_sample, norm_mix_g, w_in, b_gates, conv_w, conv_b, mh_norm_g, g_cq, g_ckv, w_uq, w_ukv, w_br_a, w_br_b, w_out, norm_x_g, norm_mem_g, w_xq, w_xk, w_xv, w_xo, norm_ffn_g, w_router, b_router, w_e_gate, w_e_up, w_e_down, final_norm_g):
    w = _prep_weights(norm_mix_g, w_in, b_gates, conv_w, conv_b, mh_norm_g, g_cq, g_ckv, w_uq, w_ukv,
                      w_br_a, w_br_b, w_out, norm_x_g, norm_mem_g, w_xq, w_xk, w_xv, w_xo, norm_ffn_g,
                      w_router, b_router, w_e_gate, w_e_up, w_e_down, final_norm_g)
    return (_trunk(x_prompt, mem_prompt, w), _trunk(x_sample, mem_sample, w))
```

```python
import functools

import jax
import jax.numpy as jnp
from jax import lax
from jax.experimental import pallas as pl
from jax.experimental.pallas import tpu as pltpu

F32 = jnp.float32
BF16 = jnp.bfloat16
I32 = jnp.int32

D_MODEL = 1024
M_WIDTH = 1024
M_HEADS = 4
M_HEAD_DIM = M_WIDTH // M_HEADS
M_CHUNK = 128
M_CONV = 5
A_HEADS = 8
A_NOPE = 128
A_ROPE = 64
A_V = 128
A_Q_RANK = 256
A_KV_RANK = 256
A_QK_PAD = 256
ROPE_BASE = 10000.0
X_HEADS = 4
X_HEAD_DIM = D_MODEL // X_HEADS
N_EXPERTS = 16
EC_FACTOR = 2
E_FF = 1024
NORM_EPS = 1e-6
LANES = 128
BIG_COLS = 6 * 1024
VMEM_LIMIT = 56 * 1024 * 1024
MOE_TOKEN_TILE = 1024
MOE_SLOT_TILE = 256
MOE_COMBINE_SLOT_TILE = 128
MOE_COMBINE_GROUP = 8


def _cparams(sem):
    return pltpu.CompilerParams(dimension_semantics=sem, vmem_limit_bytes=VMEM_LIMIT)


def _dot(a, b):
    return jnp.dot(a, b, preferred_element_type=F32)


def _dot_nt(a, b):
    return lax.dot_general(a, b, (((1,), (1,)), ((), ())), preferred_element_type=F32)


def _dot_tn(a, b):
    return lax.dot_general(a, b, (((0,), (0,)), ((), ())), preferred_element_type=F32)


def _rms(x, g):
    return x * lax.rsqrt(jnp.mean(x * x, axis=-1, keepdims=True) + NORM_EPS) * g


def _sigmoid(x):
    return 1.0 / (1.0 + jnp.exp(-x))


def _split3(x):
    hi = x.astype(BF16)
    r = x - hi.astype(F32)
    mid = r.astype(BF16)
    lo = (r - mid.astype(F32)).astype(BF16)
    return hi, mid, lo


def _inproj_kernel(x_ref, g_ref, wbig_ref, wc_ref, wkr_ref, wg_ref, wgt_ref, bg_ref, bgt_ref,
                   cos_ref, sin_ref,
                   z_ref, c_ref, kr_ref, gate_ref, gatet_ref, xn_scr):
    j = pl.program_id(1)

    @pl.when(j == 0)
    def _():
        xn = _rms(x_ref[...], g_ref[...]).astype(BF16)
        xn_scr[...] = xn
        c_ref[...] = _dot(xn, wc_ref[...]).astype(BF16)
        kr = _dot(xn, wkr_ref[...])
        kr_ref[...] = (kr[:, :LANES] * cos_ref[...] + kr[:, LANES:] * sin_ref[...]).astype(BF16)
        gate_ref[...] = _dot(xn, wg_ref[...]) + bg_ref[...]
        gatet_ref[...] = _dot_nt(wgt_ref[...], xn) + bgt_ref[...]

    z_ref[...] = _dot(xn_scr[...], wbig_ref[...]).astype(BF16)


def _inproj(x, S, w):
    T = x.shape[0]
    tm = min(1024, S)
    tn = 1024
    nseq = S // tm
    row = lambda i, j: (i, 0)
    const = lambda i, j: (0, 0)
    return pl.pallas_call(
        _inproj_kernel,
        grid=(T // tm, BIG_COLS // tn),
        in_specs=[
            pl.BlockSpec((tm, D_MODEL), row),
            pl.BlockSpec((1, D_MODEL), const),
            pl.BlockSpec((D_MODEL, tn), lambda i, j: (0, j)),
            pl.BlockSpec((D_MODEL, 512), const),
            pl.BlockSpec((D_MODEL, 256), const),
            pl.BlockSpec((D_MODEL, 16), const),
            pl.BlockSpec((16, D_MODEL), const),
            pl.BlockSpec((1, 16), const),
            pl.BlockSpec((16, 1), const),
            pl.BlockSpec((tm, LANES), lambda i, j: (i % nseq, 0)),
            pl.BlockSpec((tm, LANES), lambda i, j: (i % nseq, 0)),
        ],
        out_specs=[
            pl.BlockSpec((tm, tn), lambda i, j: (i, j)),
            pl.BlockSpec((tm, 512), row),
            pl.BlockSpec((tm, LANES), row),
            pl.BlockSpec((tm, 16), row),
            pl.BlockSpec((16, tm), lambda i, j: (0, i)),
        ],
        out_shape=[
            jax.ShapeDtypeStruct((T, BIG_COLS), BF16),
            jax.ShapeDtypeStruct((T, 512), BF16),
            jax.ShapeDtypeStruct((T, LANES), BF16),
            jax.ShapeDtypeStruct((T, 16), F32),
            jax.ShapeDtypeStruct((16, T), F32),
        ],
        scratch_shapes=[pltpu.VMEM((tm, D_MODEL), BF16)],
        compiler_params=_cparams(("parallel", "arbitrary")),
        name="inproj",
    )(x, w["norm_mix_g"], w["w_big"], w["w_c"], w["w_kr"], w["w_g"], w["w_gt"], w["b_g"], w["b_gt"],
      w["cos"], w["sin"])


CONV_HALO = 16


def _conv_kernel(z_ref, zp_ref, zn_ref, w_ref, b_ref, o_ref, scr, *, tr, tiles_per_seq, scale, transpose):
    it = pl.program_id(0) % tiles_per_seq
    keep_prev = jnp.where(it == 0, 0.0, 1.0)
    keep_next = jnp.where(it == tiles_per_seq - 1, 0.0, 1.0)
    scr[0:8, :] = zp_ref[...].astype(F32)[8:16, :] * keep_prev
    scr[8:8 + tr, :] = z_ref[...].astype(F32)
    scr[8 + tr:16 + tr, :] = zn_ref[...].astype(F32)[0:8, :] * keep_next
    acc = jnp.zeros((tr, scr.shape[1]), F32) + b_ref[...]
    for k in range(M_CONV):
        off = 8 - M_CONV // 2 + k
        acc = acc + w_ref[k:k + 1, :] * scr[off:off + tr, :]
    y = acc * _sigmoid(acc) * scale
    o_ref[...] = (y.T if transpose else y).astype(BF16)


def _conv(z, S, w, *, col0, scale, transpose):
    T = z.shape[0]
    tr = min(512, S)
    tcw = 512
    tiles_per_seq = S // tr
    hb = tr // CONV_HALO
    nhalo = T // CONV_HALO
    c0 = col0 * (M_WIDTH // tcw)
    kern = functools.partial(_conv_kernel, tr=tr, tiles_per_seq=tiles_per_seq, scale=scale, transpose=transpose)
    if transpose:
        out_spec = pl.BlockSpec((tcw, tr), lambda i, j: (j, i))
        out_shape = jax.ShapeDtypeStruct((M_WIDTH, T), BF16)
    else:
        out_spec = pl.BlockSpec((tr, tcw), lambda i, j: (i, j))
        out_shape = jax.ShapeDtypeStruct((T, M_WIDTH), BF16)
    return pl.pallas_call(
        kern,
        grid=(T // tr, M_WIDTH // tcw),
        in_specs=[
            pl.BlockSpec((tr, tcw), lambda i, j: (i, c0 + j)),
            pl.BlockSpec((CONV_HALO, tcw), lambda i, j: (jnp.maximum(i * hb - 1, 0), c0 + j)),
            pl.BlockSpec((CONV_HALO, tcw), lambda i, j: (jnp.minimum((i + 1) * hb, nhalo - 1), c0 + j)),
            pl.BlockSpec((M_CONV, tcw), lambda i, j: (0, c0 + j)),
            pl.BlockSpec((1, tcw), lambda i, j: (0, c0 + j)),
        ],
        out_specs=out_spec,
        out_shape=out_shape,
        scratch_shapes=[pltpu.VMEM((tr + 16, tcw), F32)],
        compiler_params=_cparams(("parallel", "parallel")),
        name="conv_silu_t" if transpose else "conv_silu",
    )(z, z, z, w["conv_w"], w["conv_b"])


def _log_sigmoid(x):
    return -(jnp.maximum(-x, 0.0) + jnp.log1p(jnp.exp(-jnp.abs(x))))


M_STATE_COLS = M_HEAD_DIM + LANES


def _mlstm_dir(d, q_ref, kt_ref, v_ref, g_ref, gt_ref, o_ref, c_scr, m_scr):
    L = M_CHUNK
    r = lax.broadcasted_iota(I32, (L, L), 0)
    c = lax.broadcasted_iota(I32, (L, L), 1)
    if d == 0:
        mask = c <= r
    else:
        mask = c >= r
    tri_col = jnp.where(mask, 1.0, 0.0).astype(BF16)
    tri_row = jnp.where(r <= c if d == 0 else r >= c, 1.0, 0.0).astype(BF16)
    g = g_ref[...]
    gt = gt_ref[...]
    b_row_all = sum(_dot(p, tri_row) for p in _split3(_log_sigmoid(gt)))
    sel_r = lax.broadcasted_iota(I32, (4 * M_HEADS, M_HEADS * LANES), 0)
    sel_h = lax.broadcasted_iota(I32, (4 * M_HEADS, M_HEADS * LANES), 1) // LANES
    pick_i = jnp.where(sel_r == d * 2 * M_HEADS + sel_h, 1.0, 0.0).astype(BF16)
    pick_f = jnp.where(sel_r == d * 2 * M_HEADS + M_HEADS + sel_h, 1.0, 0.0).astype(BF16)
    i_bc_all = sum(_dot(p, pick_i) for p in _split3(g))
    b_bc_all = sum(_dot(tri_col, _dot(p, pick_f).astype(BF16)) for p in _split3(_log_sigmoid(g)))
    ones_blk = jnp.ones((L, LANES), BF16)
    edge = L - 1 if d == 0 else 0
    for h in range(M_HEADS):
        idx = d * M_HEADS + h
        ci = d * 2 * M_HEADS + h
        cf = ci + M_HEADS
        i_bc = i_bc_all[:, h * LANES:(h + 1) * LANES]
        b_bc = b_bc_all[:, h * LANES:(h + 1) * LANES]
        i_row = gt[ci:ci + 1, :]
        b_row = b_row_all[cf:cf + 1, :]
        b_last = b_bc[edge:edge + 1, :]
        m_prev = m_scr[idx][0:1, :]
        a = b_bc + m_prev
        dm = jnp.where(mask, b_bc - b_row + i_row, -jnp.inf)
        m_t = jnp.maximum(a, jnp.max(dm, axis=1, keepdims=True))
        w_inter = jnp.exp(a - m_t)
        p = jnp.exp(dm - m_t)
        sl = slice(h * M_HEAD_DIM, (h + 1) * M_HEAD_DIM)
        q = q_ref[:, sl]
        kt = kt_ref[sl, :]
        v = v_ref[:, sl]
        s = _dot(q, kt) * p
        cst = c_scr[idx]
        ne = (_dot(s.astype(BF16), jnp.concatenate([v, ones_blk], axis=1))
              + jnp.concatenate([w_inter] * 3, axis=1) * _dot(q, cst.astype(BF16)))
        inv = 1.0 / jnp.maximum(jnp.abs(ne[:, M_HEAD_DIM:]), jnp.exp(-m_t))
        o_ref[:, sl] = ne[:, :M_HEAD_DIM] * jnp.concatenate([inv, inv], axis=1)
        gk = b_last - b_bc + i_bc
        m_new = jnp.maximum(b_last + m_prev, jnp.max(gk, axis=0, keepdims=True))
        decay = jnp.exp(b_last + m_prev - m_new)
        wk = jnp.exp(gk - m_new)
        wv = jnp.concatenate([(jnp.concatenate([wk, wk], axis=1) * v.astype(F32)).astype(BF16),
                              wk.astype(BF16)], axis=1)
        c_scr[idx] = jnp.concatenate([decay] * 3, axis=1) * cst + _dot(kt, wv)
        m_scr[idx] = jnp.broadcast_to(m_new, (8, LANES))


def _mlstm_kernel(qf, ktf, vf, gf, gtf, qb, ktb, vb, gb, gtb, of, ob, c_scr, m_scr):
    @pl.when(pl.program_id(1) == 0)
    def _():
        c_scr[...] = jnp.zeros(c_scr.shape, F32)
        m_scr[...] = jnp.zeros(m_scr.shape, F32)

    _mlstm_dir(0, qf, ktf, vf, gf, gtf, of, c_scr, m_scr)
    _mlstm_dir(1, qb, ktb, vb, gb, gtb, ob, c_scr, m_scr)


def _mlstm(q, kt, z, gate, gatet, B, S):
    T = B * S
    L = M_CHUNK
    nc = S // L
    fwd = lambda b, c: b * nc + c
    bwd = lambda b, c: b * nc + nc - 1 - c

    def specs(pos):
        return [
            pl.BlockSpec((L, M_WIDTH), lambda b, c: (pos(b, c), 0)),
            pl.BlockSpec((M_WIDTH, L), lambda b, c: (0, pos(b, c))),
            pl.BlockSpec((L, M_WIDTH), lambda b, c: (pos(b, c), 2)),
            pl.BlockSpec((L, 16), lambda b, c: (pos(b, c), 0)),
            pl.BlockSpec((16, L), lambda b, c: (0, pos(b, c))),
        ]

    nstate = 2 * M_HEADS
    return pl.pallas_call(
        _mlstm_kernel,
        grid=(B, nc),
        in_specs=specs(fwd) + specs(bwd),
        out_specs=[
            pl.BlockSpec((L, M_WIDTH), lambda b, c: (fwd(b, c), 0)),
            pl.BlockSpec((L, M_WIDTH), lambda b, c: (bwd(b, c), 0)),
        ],
        out_shape=[jax.ShapeDtypeStruct((T, M_WIDTH), F32)] * 2,
        scratch_shapes=[
            pltpu.VMEM((nstate, M_HEAD_DIM, M_STATE_COLS), F32),
            pltpu.VMEM((nstate, 8, LANES), F32),
        ],
        compiler_params=_cparams(("parallel", "arbitrary")),
        name="mlstm",
    )(q, kt, z, gate, gatet, q, kt, z, gate, gatet)


ATT_TILE = 512


def _flash_tiles(S):
    tk = min(ATT_TILE, S)
    nk = S // tk
    tq = min(1024 if nk <= 8 else 512, S)
    unroll = 8 if nk >= 16 else 4
    return tq, tk, unroll
ATT_LOGIT_SCALE =(A_NOPE + A_ROPE) ** -0.5 * 1.4426950408889634


def _mla_proj_kernel(c_ref, kr_ref, cost_ref, sint_ref, gq_ref, gkv_ref, wqat_ref, wqrt_ref, wk_ref, wvt_ref,
                     qt_ref, k_ref, vt_ref):
    cq = c_ref[:, :A_Q_RANK].astype(F32)
    ckv = c_ref[:, A_Q_RANK:].astype(F32)
    cqn = _rms(cq, gq_ref[...]).astype(BF16)
    ckvn = _rms(ckv, gkv_ref[...]).astype(BF16)
    qat = _dot_nt(wqat_ref[...], cqn)
    qrt = _dot_nt(wqrt_ref[...], cqn)
    kn = _dot(ckvn, wk_ref[...])
    vt = _dot_nt(wvt_ref[...], ckvn)
    cost = cost_ref[...]
    sint = sint_ref[...]
    kr = kr_ref[...]
    for h in range(A_HEADS):
        o = h * A_QK_PAD
        qt_ref[0, h, :LANES, :] = (qat[o:o + LANES, :] * ATT_LOGIT_SCALE).astype(BF16)
        qt_ref[0, h, LANES:, :] = ((qat[o + LANES:o + 2 * LANES, :] * cost
                                    + qrt[h * LANES:(h + 1) * LANES, :] * sint) * ATT_LOGIT_SCALE).astype(BF16)
        k_ref[0, h, :, :LANES] = kn[:, h * A_NOPE:(h + 1) * A_NOPE].astype(BF16)
        k_ref[0, h, :, LANES:] = kr
        vt_ref[0, h, 0] = vt[h * A_V:(h + 1) * A_V, :].astype(BF16)


def _mla_proj(c, kr, B, S, w):
    tm = min(ATT_TILE, S)
    nseq = S // tm
    row = lambda i: (i, 0)
    const = lambda i: (0, 0)
    seq_t = lambda i: (0, i % nseq)
    return pl.pallas_call(
        _mla_proj_kernel,
        grid=(B * S // tm,),
        in_specs=[
            pl.BlockSpec((tm, 512), row),
            pl.BlockSpec((tm, LANES), row),
            pl.BlockSpec((LANES, tm), seq_t),
            pl.BlockSpec((LANES, tm), seq_t),
            pl.BlockSpec((1, A_Q_RANK), const),
            pl.BlockSpec((1, A_KV_RANK), const),
            pl.BlockSpec((A_HEADS * A_QK_PAD, A_Q_RANK), const),
            pl.BlockSpec((A_HEADS * LANES, A_Q_RANK), const),
            pl.BlockSpec((A_KV_RANK, A_HEADS * A_NOPE), const),
            pl.BlockSpec((A_HEADS * A_V, A_KV_RANK), const),
        ],
        out_specs=[
            pl.BlockSpec((1, A_HEADS, A_QK_PAD, tm), lambda i: (i // nseq, 0, 0, i % nseq)),
            pl.BlockSpec((1, A_HEADS, tm, A_QK_PAD), lambda i: (i // nseq, 0, i % nseq, 0)),
            pl.BlockSpec((1, A_HEADS, 1, A_V, tm), lambda i: (i // nseq, 0, i % nseq, 0, 0)),
        ],
        out_shape=[
            jax.ShapeDtypeStruct((B, A_HEADS, A_QK_PAD, S), BF16),
            jax.ShapeDtypeStruct((B, A_HEADS, S, A_QK_PAD), BF16),
            jax.ShapeDtypeStruct((B, A_HEADS, nseq, A_V, tm), BF16),
        ],
        compiler_params=_cparams(("parallel",)),
        name="mla_proj",
    )(c, kr, w["cos_t"], w["sin_t"], w["g_cq"], w["g_ckv"], w["w_qat"], w["w_qrt"], w["w_uk"], w["w_uvt"])


def _flash_kernel(q_ref, k_ref, vt_ref, o_ref, s_scr, *, tk, nk, unroll):
    qt = q_ref[0, 0]
    tq = qt.shape[1]

    def scores(j, slot):
        start = pl.multiple_of(j * tk, tk)
        s_scr[slot] = _dot(k_ref[0, 0, pl.ds(start, tk), :], qt)

    def accumulate(j, slot, carry):
        m, l, acc = carry
        s = s_scr[slot]
        m_new = jnp.maximum(m, jnp.max(s, axis=0, keepdims=True))
        p = jnp.exp2(s - m_new)
        alpha = jnp.exp2(m - m_new)
        l = alpha * l + jnp.sum(p, axis=0, keepdims=True)
        acc = alpha * acc + _dot(vt_ref[0, 0, j], p.astype(BF16))
        return m_new, l, acc

    def body(jj, carry):
        for t in range(unroll):
            j = unroll * jj + t
            scores(jnp.minimum(j + 1, nk - 1), (t + 1) % 2)
            carry = accumulate(j, t % 2, carry)
        return carry

    carry = (jnp.full((1, tq), -jnp.inf, F32), jnp.zeros((1, tq), F32), jnp.zeros((A_V, tq), F32))
    scores(0, 0)
    n_loop = nk // unroll
    if n_loop > 0:
        carry = lax.fori_loop(0, n_loop, body, carry)
    for j in range(unroll * n_loop, nk):
        if j + 1 < nk:
            scores(j + 1, (j + 1) % 2)
        carry = accumulate(j, j % 2, carry)
    _, l, acc = carry
    o_ref[...] = (acc / l).T.astype(BF16)


def _flash(qt, k, vt):
    B, H, S, _ = k.shape
    tq, tk, unroll = _flash_tiles(S)
    nq = S // tq
    nk = S // tk
    kern = functools.partial(_flash_kernel, tk=tk, nk=nk, unroll=unroll)
    return pl.pallas_call(
        kern,
        grid=(B, H, nq),
        in_specs=[
            pl.BlockSpec((1, 1, A_QK_PAD, tq), lambda b, h, i: (b, h, 0, i)),
            pl.BlockSpec((1, 1, S, A_QK_PAD), lambda b, h, i: (b, h, 0, 0)),
            pl.BlockSpec((1, 1, nk, A_V, tk), lambda b, h, i: (b, h, 0, 0, 0)),
        ],
        out_specs=pl.BlockSpec((tq, A_V), lambda b, h, i: (b * nq + i, h)),
        out_shape=jax.ShapeDtypeStruct((B * S, A_HEADS * A_V), BF16),
        scratch_shapes=[pltpu.VMEM((2, tk, tq), F32)],
        compiler_params=_cparams(("parallel", "parallel", "arbitrary")),
        name="mla_flash",
    )(qt, k, vt)


def _mixer_out_kernel(hf_ref, hb_ref, om_ref, ga_ref, gb_ref, att_ref, x_ref, mg_ref,
                      wa_ref, wb_ref, wo_ref, o_ref):
    hs = hf_ref[...] + hb_ref[...]
    mg = mg_ref[...]
    parts = []
    for h in range(M_HEADS):
        sl = slice(h * M_HEAD_DIM, (h + 1) * M_HEAD_DIM)
        parts.append(_rms(hs[:, sl], mg[:, sl]))
    hn = jnp.concatenate(parts, axis=1) * _sigmoid(om_ref[...].astype(F32))
    y_a = _dot(hn.astype(BF16), wa_ref[...])
    y_b = _dot(att_ref[...], wb_ref[...])
    merged = _sigmoid(ga_ref[...].astype(F32)) * y_a + _sigmoid(gb_ref[...].astype(F32)) * y_b
    o_ref[...] = x_ref[...] + _dot(merged.astype(BF16), wo_ref[...])


def _mixer_out(hf, hb, z, att, x, w):
    T = x.shape[0]
    tm = min(512, T)
    row = lambda i: (i, 0)
    const = lambda i: (0, 0)
    wspec = pl.BlockSpec((D_MODEL, D_MODEL), const)
    return pl.pallas_call(
        _mixer_out_kernel,
        grid=(T // tm,),
        in_specs=[
            pl.BlockSpec((tm, M_WIDTH), row),
            pl.BlockSpec((tm, M_WIDTH), row),
            pl.BlockSpec((tm, M_WIDTH), lambda i: (i, 3)),
            pl.BlockSpec((tm, D_MODEL), lambda i: (i, 4)),
            pl.BlockSpec((tm, D_MODEL), lambda i: (i, 5)),
            pl.BlockSpec((tm, D_MODEL), row),
            pl.BlockSpec((tm, D_MODEL), row),
            pl.BlockSpec((1, M_WIDTH), const),
            wspec, wspec, wspec,
        ],
        out_specs=pl.BlockSpec((tm, D_MODEL), row),
        out_shape=jax.ShapeDtypeStruct((T, D_MODEL), F32),
        compiler_params=_cparams(("parallel",)),
        name="mixer_out",
    )(hf, hb, z, z, z, att, x, w["mh_norm_g"], w["w_br_a"], w["w_br_b"], w["w_out"])


def _mem_kernel(m_ref, g_ref, wk_ref, wv_ref, k_ref, v_ref):
    mn = _rms(m_ref[...], g_ref[...]).astype(BF16)
    k_ref[...] = _dot(mn, wk_ref[...]).astype(BF16)
    v_ref[...] = _dot(mn, wv_ref[...]).astype(BF16)


def _mem_proj(mem, w):
    R = mem.shape[0]
    tm = 256
    row = lambda i: (i, 0)
    const = lambda i: (0, 0)
    wspec = pl.BlockSpec((D_MODEL, D_MODEL), const)
    return pl.pallas_call(
        _mem_kernel,
        grid=(R // tm,),
        in_specs=[pl.BlockSpec((tm, D_MODEL), row), pl.BlockSpec((1, D_MODEL), const), wspec, wspec],
        out_specs=[pl.BlockSpec((tm, D_MODEL), row)] * 2,
        out_shape=[jax.ShapeDtypeStruct((R, D_MODEL), BF16)] * 2,
        compiler_params=_cparams(("parallel",)),
        name="mem_proj",
    )(mem, w["norm_mem_g"], w["w_xk"], w["w_xv"])


def _cross_router_kernel(x_ref, kx_ref, vx_ref, gx_ref, gf_ref, wq_ref, wo_ref, wr_ref, wrt_ref,
                         br_ref, brt_ref, x2_ref, xn_ref, aff_ref, afft_ref):
    x1 = x_ref[...]
    xn = _rms(x1, gx_ref[...]).astype(BF16)
    q = _dot(xn, wq_ref[...])
    outs = []
    for h in range(X_HEADS):
        sl = slice(h * X_HEAD_DIM, (h + 1) * X_HEAD_DIM)
        s = _dot_nt(q[:, sl].astype(BF16), kx_ref[:, sl]) * (X_HEAD_DIM ** -0.5)
        e = jnp.exp(s - jnp.max(s, axis=1, keepdims=True))
        p = e / jnp.sum(e, axis=1, keepdims=True)
        outs.append(_dot(p.astype(BF16), vx_ref[:, sl]))
    o = jnp.concatenate(outs, axis=1)
    x2 = x1 + _dot(o.astype(BF16), wo_ref[...])
    x2_ref[...] = x2
    xf = _rms(x2, gf_ref[...])
    hi = xf.astype(BF16)
    xn_ref[...] = hi
    lo = (xf - hi.astype(F32)).astype(BF16)
    wr = wr_ref[...]
    wr_hi = wr.astype(BF16)
    wr_lo = (wr - wr_hi.astype(F32)).astype(BF16)
    logit = _dot(hi, wr_hi) + _dot(lo, wr_hi) + _dot(hi, wr_lo) + br_ref[...]
    e = jnp.exp(logit - jnp.max(logit, axis=1, keepdims=True))
    aff_ref[...] = e / jnp.sum(e, axis=1, keepdims=True)
    wrt = wrt_ref[...]
    wrt_hi = wrt.astype(BF16)
    wrt_lo = (wrt - wrt_hi.astype(F32)).astype(BF16)
    logit_t = _dot_nt(wrt_hi, hi) + _dot_nt(wrt_hi, lo) + _dot_nt(wrt_lo, hi) + brt_ref[...]
    et = jnp.exp(logit_t - jnp.max(logit_t, axis=0, keepdims=True))
    afft_ref[...] = et / jnp.sum(et, axis=0, keepdims=True)


def _cross_router(x1, kx, vx, S, n_mem, w):
    T = x1.shape[0]
    tm = min(512, S)
    nseq = S // tm
    row = lambda i: (i, 0)
    const = lambda i: (0, 0)
    wspec = pl.BlockSpec((D_MODEL, D_MODEL), const)
    memspec = pl.BlockSpec((n_mem, D_MODEL), lambda i: (i // nseq, 0))
    return pl.pallas_call(
        _cross_router_kernel,
        grid=(T // tm,),
        in_specs=[
            pl.BlockSpec((tm, D_MODEL), row), memspec, memspec,
            pl.BlockSpec((1, D_MODEL), const), pl.BlockSpec((1, D_MODEL), const),
            wspec, wspec,
            pl.BlockSpec((D_MODEL, N_EXPERTS), const), pl.BlockSpec((N_EXPERTS, D_MODEL), const),
            pl.BlockSpec((1, N_EXPERTS), const), pl.BlockSpec((N_EXPERTS, 1), const),
        ],
        out_specs=[
            pl.BlockSpec((tm, D_MODEL), row),
            pl.BlockSpec((tm, D_MODEL), row),
            pl.BlockSpec((tm, N_EXPERTS), row),
            pl.BlockSpec((N_EXPERTS, tm), lambda i: (0, i)),
        ],
        out_shape=[
            jax.ShapeDtypeStruct((T, D_MODEL), F32),
            jax.ShapeDtypeStruct((T, D_MODEL), BF16),
            jax.ShapeDtypeStruct((T, N_EXPERTS), F32),
            jax.ShapeDtypeStruct((N_EXPERTS, T), F32),
        ],
        compiler_params=_cparams(("parallel",)),
        name="cross_router",
    )(x1, kx, vx, w["norm_x_g"], w["norm_ffn_g"], w["w_xq"], w["w_xo"], w["w_router"], w["w_router_t"],
      w["b_router"], w["b_router_t"])


def _excl_cumsum(mask_f, strict_lane, strict_blk):
    nb = mask_f.shape[0]
    within = _dot(mask_f.astype(BF16), strict_lane)
    tot = jnp.sum(mask_f, axis=1, keepdims=True)
    bstart = _dot(strict_blk, jnp.broadcast_to(tot, (nb, LANES)).astype(BF16))
    return within + bstart, bstart


def _select_kernel(aff_ref, pos_ref, bst_ref, *, cap):
    a = aff_ref[0]
    nb = a.shape[0]
    bits = pltpu.bitcast(a, I32)

    def radix(i, prefix):
        cand = prefix | jnp.left_shift(jnp.int32(1), 30 - i)
        cnt = jnp.sum(jnp.where(bits >= cand, 1.0, 0.0), axis=(0, 1), keepdims=True)
        return jnp.where(cnt >= cap, cand, prefix)

    thr = lax.fori_loop(0, 31, radix, jnp.zeros((1, 1), I32))
    gt = bits > thr
    eq = bits == thr
    need = cap - jnp.sum(jnp.where(gt, 1.0, 0.0), axis=(0, 1), keepdims=True)
    li = lax.broadcasted_iota(I32, (LANES, LANES), 0)
    lj = lax.broadcasted_iota(I32, (LANES, LANES), 1)
    strict_lane = jnp.where(li < lj, 1.0, 0.0).astype(BF16)
    bi = lax.broadcasted_iota(I32, (nb, nb), 0)
    bj = lax.broadcasted_iota(I32, (nb, nb), 1)
    strict_blk = jnp.where(bj < bi, 1.0, 0.0).astype(BF16)
    rank, _ = _excl_cumsum(jnp.where(eq, 1.0, 0.0), strict_lane, strict_blk)
    sel = gt | (eq & (rank < need))
    pos, bstart = _excl_cumsum(jnp.where(sel, 1.0, 0.0), strict_lane, strict_blk)
    pos_ref[0] = jnp.where(sel, pos.astype(I32), -1)
    col = jnp.broadcast_to(bstart[:, 0:1], (nb, nb))
    bst_ref[0] = jnp.sum(jnp.where(bi == bj, col, 0.0), axis=0, keepdims=True).astype(I32)


def _select(aff_t, cap):
    E, T = aff_t.shape
    nb = T // LANES
    kern = functools.partial(_select_kernel, cap=cap)
    return pl.pallas_call(
        kern,
        grid=(E,),
        in_specs=[pl.BlockSpec((1, nb, LANES), lambda e: (e, 0, 0))],
        out_specs=[
            pl.BlockSpec((1, nb, LANES), lambda e: (e, 0, 0)),
            pl.BlockSpec((1, 1, nb), lambda e: (e, 0, 0)),
        ],
        out_shape=[
            jax.ShapeDtypeStruct((E, nb, LANES), I32),
            jax.ShapeDtypeStruct((E, 1, nb), I32),
        ],
        compiler_params=_cparams(("parallel",)),
        name="ec_select",
    )(aff_t.reshape(E, nb, LANES))


def _cells(bst, cap, T, tu, tc):
    E = bst.shape[0]
    ns = cap // tc
    start = bst[:, ::tu // LANES]
    end = jnp.concatenate([start[:, 1:], jnp.full((E, 1), cap, I32)], axis=1)
    cnt = end - start
    s_lo = jnp.minimum(start // tc, ns - 1)
    s_hi = jnp.where(cnt > 0, (end - 1) // tc, s_lo)
    return jnp.where(cnt > 0, s_hi - s_lo + 1, 0), s_lo


def _enumerate_cells(nc_flat, slo_flat, steps):
    off_end = jnp.cumsum(nc_flat, axis=-1)
    off = off_end - nc_flat
    total = off_end[..., -1:]
    k = jnp.arange(steps, dtype=I32)
    kk = jnp.minimum(k, total - 1)
    grp = jnp.sum((off_end[..., None, :] <= kk[..., :, None]).astype(I32), axis=-1)
    s = jnp.take_along_axis(slo_flat, grp, axis=-1) + kk - jnp.take_along_axis(off, grp, axis=-1)
    return grp, s, (k < total).astype(I32)


def _gather_tables(bst, cap, T, tu, tc):
    E = bst.shape[0]
    ncell, s_lo = _cells(bst, cap, T, tu, tc)
    g_u, g_s, g_valid = _enumerate_cells(ncell, s_lo, T // tu + cap // tc)
    prev_s = jnp.concatenate([jnp.full((E, 1), -1, I32), g_s[:, :-1]], axis=1)
    g_first = g_valid * (g_s != prev_s).astype(I32)
    return tuple(t.reshape(-1) for t in (g_u, g_s, g_valid, g_first))


def _combine_tables(bst, cap, T, tu, tc):
    E = bst.shape[0]
    nu = T // tu
    ns = cap // tc
    G = MOE_COMBINE_GROUP
    ncell, s_lo = _cells(bst, cap, T, tu, tc)
    ncell_c = ncell.at[0].set(jnp.maximum(ncell[0], 1))
    pc = E * (nu + ns)
    c_grp, c_s, _ = _enumerate_cells(ncell_c.T.reshape(-1), s_lo.T.reshape(-1), pc)
    c_e = c_grp % E
    cells_u = jnp.sum(ncell_c, axis=0)
    cell_off = jnp.cumsum(cells_u) - cells_u
    groups_u = (cells_u + G - 1) // G
    grp_end = jnp.cumsum(groups_u)
    n_steps = pc // G + nu
    k = jnp.arange(n_steps, dtype=I32)
    kk = jnp.minimum(k, grp_end[-1] - 1)
    t_u = jnp.sum((grp_end[None, :] <= kk[:, None]).astype(I32), axis=1)
    j = kk - (grp_end - groups_u)[t_u]
    t_valid = (k < grp_end[-1]).astype(I32)
    local = j[:, None] * G + jnp.arange(G, dtype=I32)[None, :]
    cell_ok = (local < cells_u[t_u][:, None]).astype(I32) * t_valid[:, None]
    cid = jnp.minimum(cell_off[t_u][:, None] + jnp.minimum(local, cells_u[t_u][:, None] - 1), pc - 1)
    t_first = t_valid * (j == 0).astype(I32)
    t_last = t_valid * (j == groups_u[t_u] - 1).astype(I32)
    return (t_u, c_e[cid].reshape(-1), c_s[cid].reshape(-1), cell_ok.reshape(-1), t_valid, t_first, t_last)


def _gather_kernel(u_tab, s_tab, valid_tab, first_tab, pos_ref, x_ref, o_ref, *, steps, tc):
    step = pl.program_id(0) * steps + pl.program_id(1)
    tu = pos_ref.shape[-1]
    slot = lax.broadcasted_iota(I32, (tc, tu), 0) + s_tab[step] * tc
    onehot = jnp.where(pos_ref[0] == slot, 1.0, 0.0).astype(BF16)

    @pl.when(first_tab[step] == 1)
    def _():
        o_ref[0] = _dot(onehot, x_ref[...]).astype(BF16)

    @pl.when((valid_tab[step] == 1) & (first_tab[step] == 0))
    def _():
        o_ref[0] = o_ref[0] + _dot(onehot, x_ref[...]).astype(BF16)


def _moe_gather(xn, pos_row, tabs, cap, tu, tc):
    T = xn.shape[0]
    E = N_EXPERTS
    steps = T // tu + cap // tc
    kern = functools.partial(_gather_kernel, steps=steps, tc=tc)
    grid_spec = pltpu.PrefetchScalarGridSpec(
        num_scalar_prefetch=4,
        grid=(E, steps),
        in_specs=[
            pl.BlockSpec((1, 1, tu), lambda e, k, u, s, v, f: (e, 0, u[e * steps + k])),
            pl.BlockSpec((tu, D_MODEL), lambda e, k, u, s, v, f: (u[e * steps + k], 0)),
        ],
        out_specs=pl.BlockSpec((1, tc, D_MODEL), lambda e, k, u, s, v, f: (e, s[e * steps + k], 0)),
    )
    return pl.pallas_call(
        kern,
        grid_spec=grid_spec,
        out_shape=jax.ShapeDtypeStruct((E, cap, D_MODEL), BF16),
        compiler_params=_cparams(("parallel", "arbitrary")),
        name="moe_gather",
    )(*tabs, pos_row, xn)


def _ffn_kernel(x_ref, wg_ref, wu_ref, wd_ref, o_ref, wg_b, wu_b, wd_b):
    @pl.when(pl.program_id(1) == 0)
    def _():
        wg_b[...] = wg_ref[0].astype(BF16)
        wu_b[...] = wu_ref[0].astype(BF16)
        wd_b[...] = wd_ref[0].astype(BF16)

    x = x_ref[0]
    g = _dot(x, wg_b[...])
    u = _dot(x, wu_b[...])
    h = (g * _sigmoid(g) * u).astype(BF16)
    o_ref[0] = _dot(h, wd_b[...]).astype(BF16)


def _moe_ffn(xe, w):
    E, cap, _ = xe.shape
    tf = min(512, cap)
    wspec = pl.BlockSpec((1, D_MODEL, E_FF), lambda e, i: (e, 0, 0))
    return pl.pallas_call(
        _ffn_kernel,
        grid=(E, cap // tf),
        in_specs=[pl.BlockSpec((1, tf, D_MODEL), lambda e, i: (e, i, 0)), wspec, wspec,
                  pl.BlockSpec((1, E_FF, D_MODEL), lambda e, i: (e, 0, 0))],
        out_specs=pl.BlockSpec((1, tf, D_MODEL), lambda e, i: (e, i, 0)),
        out_shape=jax.ShapeDtypeStruct((E, cap, D_MODEL), BF16),
        scratch_shapes=[pltpu.VMEM((D_MODEL, E_FF), BF16), pltpu.VMEM((D_MODEL, E_FF), BF16),
                        pltpu.VMEM((E_FF, D_MODEL), BF16)],
        compiler_params=_cparams(("parallel", "arbitrary")),
        name="moe_ffn",
    )(xe, w["w_e_gate"], w["w_e_up"], w["w_e_down"])


def _combine_kernel(u_tab, e_tab, s_tab, ok_tab, valid_tab, first_tab, last_tab, pos_ref, aff_ref, *rest, tc):
    G = MOE_COMBINE_GROUP
    ye_refs = rest[:G]
    x_ref, g_ref, o_ref, acc = rest[G:]
    step = pl.program_id(0)
    tu = pos_ref.shape[0]

    @pl.when(first_tab[step] == 1)
    def _():
        acc[...] = jnp.zeros(acc.shape, F32)

    @pl.when(valid_tab[step] == 1)
    def _():
        pos = pos_ref[...].astype(F32)
        aff = aff_ref[...]
        lane_e = lax.broadcasted_iota(I32, pos.shape, 1)
        lane_r = lax.broadcasted_iota(I32, (tu, tc), 1).astype(F32)
        hits = []
        for g in range(G):
            c = step * G + g
            mine = lane_e == e_tab[c]
            slot = jnp.sum(jnp.where(mine, pos, 0.0), axis=1, keepdims=True)
            gate = jnp.sum(jnp.where(mine, aff, 0.0), axis=1, keepdims=True)
            base = jnp.where(ok_tab[c] == 1, s_tab[c] * tc, -2 * tc).astype(F32)
            hits.append(jnp.where(slot - base == lane_r, gate, 0.0).astype(BF16))
        ye = jnp.concatenate([r[0] for r in ye_refs], axis=0)
        acc[...] = acc[...] + _dot(jnp.concatenate(hits, axis=1), ye)

    @pl.when(last_tab[step] == 1)
    def _():
        o_ref[...] = _rms(x_ref[...] + acc[...], g_ref[...])


def _moe_combine(ye, pos_col, aff, x2, tabs, final_g, cap, tu, tc):
    T = x2.shape[0]
    E = N_EXPERTS
    G = MOE_COMBINE_GROUP
    steps = E * (T // tu + cap // tc) // G + T // tu
    kern = functools.partial(_combine_kernel, tc=tc)

    def ye_spec(g):
        return pl.BlockSpec((1, tc, D_MODEL), lambda k, u, e, s, *_: (e[k * G + g], s[k * G + g], 0))

    tile = pl.BlockSpec((tu, D_MODEL), lambda k, u, *_: (u[k], 0))
    per_expert = pl.BlockSpec((tu, N_EXPERTS), lambda k, u, *_: (u[k], 0))
    grid_spec = pltpu.PrefetchScalarGridSpec(
        num_scalar_prefetch=7,
        grid=(steps,),
        in_specs=[per_expert, per_expert]
        + [ye_spec(g) for g in range(G)]
        + [tile, pl.BlockSpec((1, D_MODEL), lambda k, *_: (0, 0))],
        out_specs=tile,
        scratch_shapes=[pltpu.VMEM((tu, D_MODEL), F32)],
    )
    return pl.pallas_call(
        kern,
        grid_spec=grid_spec,
        out_shape=jax.ShapeDtypeStruct((T, D_MODEL), F32),
        compiler_params=_cparams(("arbitrary",)),
        name="moe_combine",
    )(*tabs, pos_col, aff, *([ye] * G), x2, final_g)


def _rope_tables(S):
    pos = jnp.arange(S, dtype=F32)
    inv = ROPE_BASE ** (-jnp.arange(0, A_ROPE, 2, dtype=F32) / A_ROPE)
    ang = pos[:, None] * inv[None, :]
    pad = jnp.zeros((S, LANES - A_ROPE), F32)
    cos = jnp.concatenate([jnp.cos(ang), jnp.cos(ang), pad], axis=1)
    sin = jnp.concatenate([jnp.sin(ang), jnp.sin(ang), pad], axis=1)
    return cos, sin


def _rotate_half_cols(w):
    half = A_ROPE // 2
    return jnp.concatenate([-w[..., half:], w[..., :half]], axis=-1)


def _prep_weights(norm_mix_g, w_in, b_gates, conv_w, conv_b, mh_norm_g, g_cq, g_ckv, w_uq, w_ukv,
                  w_br_a, w_br_b, w_out, norm_x_g, norm_mem_g, w_xq, w_xk, w_xv, w_xo, norm_ffn_g,
                  w_router, b_router, w_e_gate, w_e_up, w_e_down, final_norm_g):
    l = 0
    wi = w_in[l]
    o = 0
    cols = {}
    for name, n in (("qm", M_WIDTH), ("km", M_WIDTH), ("vm", M_WIDTH), ("om", M_WIDTH), ("gates", 4 * M_HEADS),
                    ("cq", A_Q_RANK), ("ckv", A_KV_RANK), ("kr", A_ROPE), ("ga", D_MODEL), ("gb", D_MODEL)):
        cols[name] = wi[:, o:o + n]
        o += n
    zpad = jnp.zeros((D_MODEL, LANES - A_ROPE), F32)
    w_kr = jnp.concatenate([cols["kr"], zpad, _rotate_half_cols(cols["kr"]), zpad], axis=1)
    uq = w_uq[l].reshape(A_Q_RANK, A_HEADS, A_NOPE + A_ROPE)
    uq_rope = uq[:, :, A_NOPE:]
    hpad = jnp.zeros((A_Q_RANK, A_HEADS, LANES - A_ROPE), F32)
    w_qa = jnp.concatenate([uq, hpad], axis=2).reshape(A_Q_RANK, A_HEADS * A_QK_PAD)
    w_qr = jnp.concatenate([_rotate_half_cols(uq_rope), hpad], axis=2).reshape(A_Q_RANK, A_HEADS * LANES)
    ukv = w_ukv[l].reshape(A_KV_RANK, A_HEADS, A_NOPE + A_V)
    row = lambda v: v.reshape(1, -1).astype(F32)
    return {
        "norm_mix_g": row(norm_mix_g[l]),
        "w_big": jnp.concatenate([cols[n] for n in ("qm", "km", "vm", "om", "ga", "gb")], axis=1).astype(BF16),
        "w_c": jnp.concatenate([cols["cq"], cols["ckv"]], axis=1).astype(BF16),
        "w_kr": w_kr.astype(BF16),
        "w_g": cols["gates"].astype(BF16),
        "w_gt": cols["gates"].T.astype(BF16),
        "b_g": row(b_gates[l]),
        "b_gt": b_gates[l].reshape(-1, 1).astype(F32),
        "conv_w": conv_w[l],
        "conv_b": row(conv_b[l]),
        "mh_norm_g": row(mh_norm_g[l]),
        "g_cq": row(g_cq[l]),
        "g_ckv": row(g_ckv[l]),
        "w_qat": w_qa.T.astype(BF16),
        "w_qrt": w_qr.T.astype(BF16),
        "w_uk": ukv[:, :, :A_NOPE].reshape(A_KV_RANK, A_HEADS * A_NOPE).astype(BF16),
        "w_uvt": ukv[:, :, A_NOPE:].reshape(A_KV_RANK, A_HEADS * A_V).T.astype(BF16),
        "w_br_a": w_br_a[l].astype(BF16),
        "w_br_b": w_br_b[l].astype(BF16),
        "w_out": w_out[l].astype(BF16),
        "norm_x_g": row(norm_x_g[l]),
        "norm_mem_g": row(norm_mem_g[l]),
        "w_xq": w_xq[l].astype(BF16),
        "w_xk": w_xk[l].astype(BF16),
        "w_xv": w_xv[l].astype(BF16),
        "w_xo": w_xo[l].astype(BF16),
        "norm_ffn_g": row(norm_ffn_g[l]),
        "w_router": w_router[l],
        "w_router_t": w_router[l].T,
        "b_router": row(b_router[l]),
        "b_router_t": b_router[l].reshape(-1, 1).astype(F32),
        "w_e_gate": w_e_gate[l],
        "w_e_up": w_e_up[l],
        "w_e_down": w_e_down[l],
        "final_norm_g": row(final_norm_g),
    }


def _trunk(x, mem, w):
    B, S, _ = x.shape
    T = B * S
    n_mem = mem.shape[1]
    w = dict(w)
    w["cos"], w["sin"] = _rope_tables(S)
    w["cos_t"], w["sin_t"] = w["cos"].T, w["sin"].T
    x2d = x.reshape(T, D_MODEL)

    z, c, kr, gate, gatet = _inproj(x2d, S, w)
    qm = _conv(z, S, w, col0=0, scale=M_HEAD_DIM ** -0.5, transpose=False)
    kmt = _conv(z, S, w, col0=1, scale=1.0, transpose=True)
    hf, hb = _mlstm(qm, kmt, z, gate, gatet, B, S)
    qc, kc, vc = _mla_proj(c, kr, B, S, w)
    att = _flash(qc, kc, vc)
    x1 = _mixer_out(hf, hb, z, att, x2d, w)

    kx, vx = _mem_proj(mem.reshape(B * n_mem, D_MODEL), w)
    x2, xn, aff, aff_t = _cross_router(x1, kx, vx, S, n_mem, w)

    cap = max(1, EC_FACTOR * T // N_EXPERTS)
    pos, bst = _select(aff_t, cap)
    tu = min(MOE_TOKEN_TILE, T)
    tc = min(MOE_SLOT_TILE, cap)
    tcc = min(MOE_COMBINE_SLOT_TILE, cap)
    bst = bst.reshape(N_EXPERTS, -1)
    xe = _moe_gather(xn, pos.reshape(N_EXPERTS, 1, T), _gather_tables(bst, cap, T, tu, tc), cap, tu, tc)
    ye = _moe_ffn(xe, w)
    y = _moe_combine(ye, pos.reshape(N_EXPERTS, T).T, aff, x2, _combine_tables(bst, cap, T, tu, tcc),
                     w["final_norm_g"], cap, tu, tcc)
    return y.reshape(B, S, D_MODEL)


def kernel(x_prompt, x_sample, mem_prompt, mem_sample, norm_mix_g, w_in, b_gates, conv_w, conv_b, mh_norm_g, g_cq, g_ckv, w_uq, w_ukv, w_br_a, w_br_b, w_out, norm_x_g, norm_mem_g, w_xq, w_xk, w_xv, w_xo, norm_ffn_g, w_router, b_router, w_e_gate, w_e_up, w_e_down, final_norm_g):
    w = _prep_weights(norm_mix_g, w_in, b_gates, conv_w, conv_b, mh_norm_g, g_cq, g_ckv, w_uq, w_ukv,
                      w_br_a, w_br_b, w_out, norm_x_g, norm_mem_g, w_xq, w_xk, w_xv, w_xo, norm_ffn_g,
                      w_router, b_router, w_e_gate, w_e_up, w_e_down, final_norm_g)
    return (_trunk(x_prompt, mem_prompt, w), _trunk(x_sample, mem_sample, w))
```

```python
import functools

import jax
import jax.numpy as jnp
from jax import lax
from jax.experimental import pallas as pl
from jax.experimental.pallas import tpu as pltpu

F32 = jnp.float32
BF16 = jnp.bfloat16
I32 = jnp.int32

D_MODEL = 1024
M_WIDTH = 1024
M_HEADS = 4
M_HEAD_DIM = M_WIDTH // M_HEADS
M_CHUNK = 128
M_CONV = 5
A_HEADS = 8
A_NOPE = 128
A_ROPE = 64
A_V = 128
A_Q_RANK = 256
A_KV_RANK = 256
A_QK_PAD = 256
ROPE_BASE = 10000.0
X_HEADS = 4
X_HEAD_DIM = D_MODEL // X_HEADS
N_EXPERTS = 16
EC_FACTOR = 2
E_FF = 1024
NORM_EPS = 1e-6
LANES = 128
BIG_COLS = 6 * 1024
VMEM_LIMIT = 56 * 1024 * 1024
MOE_TOKEN_TILE = 1024
MOE_COMBINE_TOKEN_TILE = 512
MOE_SLOT_TILE = 256
MOE_COMBINE_SLOT_TILE = 128
MOE_COMBINE_GROUP = 8


def _cparams(sem):
    return pltpu.CompilerParams(dimension_semantics=sem, vmem_limit_bytes=VMEM_LIMIT)


def _dot(a, b):
    return jnp.dot(a, b, preferred_element_type=F32)


def _dot_nt(a, b):
    return lax.dot_general(a, b, (((1,), (1,)), ((), ())), preferred_element_type=F32)


def _dot_tn(a, b):
    return lax.dot_general(a, b, (((0,), (0,)), ((), ())), preferred_element_type=F32)


def _rms(x, g):
    return x * lax.rsqrt(jnp.mean(x * x, axis=-1, keepdims=True) + NORM_EPS) * g


def _sigmoid(x):
    return 1.0 / (1.0 + jnp.exp(-x))


def _split3(x):
    hi = x.astype(BF16)
    r = x - hi.astype(F32)
    mid = r.astype(BF16)
    lo = (r - mid.astype(F32)).astype(BF16)
    return hi, mid, lo


def _inproj_kernel(x_ref, g_ref, wbig_ref, wc_ref, wkr_ref, wg_ref, wgt_ref, bg_ref, bgt_ref,
                   cos_ref, sin_ref,
                   z_ref, c_ref, kr_ref, gate_ref, gatet_ref, xn_scr):
    j = pl.program_id(1)

    @pl.when(j == 0)
    def _():
        xn = _rms(x_ref[...], g_ref[...]).astype(BF16)
        xn_scr[...] = xn
        c_ref[...] = _dot(xn, wc_ref[...]).astype(BF16)
        kr = _dot(xn, wkr_ref[...])
        kr_ref[...] = (kr[:, :LANES] * cos_ref[...] + kr[:, LANES:] * sin_ref[...]).astype(BF16)
        gate_ref[...] = _dot(xn, wg_ref[...]) + bg_ref[...]
        gatet_ref[...] = _dot_nt(wgt_ref[...], xn) + bgt_ref[...]

    z_ref[...] = _dot(xn_scr[...], wbig_ref[...]).astype(BF16)


def _inproj(x, S, w):
    T = x.shape[0]
    tm = min(1024, S)
    tn = 1024
    nseq = S // tm
    row = lambda i, j: (i, 0)
    const = lambda i, j: (0, 0)
    return pl.pallas_call(
        _inproj_kernel,
        grid=(T // tm, BIG_COLS // tn),
        in_specs=[
            pl.BlockSpec((tm, D_MODEL), row),
            pl.BlockSpec((1, D_MODEL), const),
            pl.BlockSpec((D_MODEL, tn), lambda i, j: (0, j)),
            pl.BlockSpec((D_MODEL, 512), const),
            pl.BlockSpec((D_MODEL, 256), const),
            pl.BlockSpec((D_MODEL, 16), const),
            pl.BlockSpec((16, D_MODEL), const),
            pl.BlockSpec((1, 16), const),
            pl.BlockSpec((16, 1), const),
            pl.BlockSpec((tm, LANES), lambda i, j: (i % nseq, 0)),
            pl.BlockSpec((tm, LANES), lambda i, j: (i % nseq, 0)),
        ],
        out_specs=[
            pl.BlockSpec((tm, tn), lambda i, j: (i, j)),
            pl.BlockSpec((tm, 512), row),
            pl.BlockSpec((tm, LANES), row),
            pl.BlockSpec((tm, 16), row),
            pl.BlockSpec((16, tm), lambda i, j: (0, i)),
        ],
        out_shape=[
            jax.ShapeDtypeStruct((T, BIG_COLS), BF16),
            jax.ShapeDtypeStruct((T, 512), BF16),
            jax.ShapeDtypeStruct((T, LANES), BF16),
            jax.ShapeDtypeStruct((T, 16), F32),
            jax.ShapeDtypeStruct((16, T), F32),
        ],
        scratch_shapes=[pltpu.VMEM((tm, D_MODEL), BF16)],
        compiler_params=_cparams(("parallel", "arbitrary")),
        name="inproj",
    )(x, w["norm_mix_g"], w["w_big"], w["w_c"], w["w_kr"], w["w_g"], w["w_gt"], w["b_g"], w["b_gt"],
      w["cos"], w["sin"])


CONV_HALO = 16


def _conv_kernel(z_ref, zp_ref, zn_ref, w_ref, b_ref, o_ref, scr, *, tr, tiles_per_seq, scale, transpose):
    it = pl.program_id(0) % tiles_per_seq
    keep_prev = jnp.where(it == 0, 0.0, 1.0)
    keep_next = jnp.where(it == tiles_per_seq - 1, 0.0, 1.0)
    scr[0:8, :] = zp_ref[...].astype(F32)[8:16, :] * keep_prev
    scr[8:8 + tr, :] = z_ref[...].astype(F32)
    scr[8 + tr:16 + tr, :] = zn_ref[...].astype(F32)[0:8, :] * keep_next
    acc = jnp.zeros((tr, scr.shape[1]), F32) + b_ref[...]
    xe = scr[...]
    for k in range(M_CONV):
        d = k - M_CONV // 2
        win = xe if d == 0 else pltpu.roll(xe, (-d) % (tr + 16), axis=0)
        acc = acc + w_ref[k:k + 1, :] * win[8:8 + tr, :]
    y = acc * _sigmoid(acc) * scale
    o_ref[...] = (y.T if transpose else y).astype(BF16)


def _conv(z, S, w, *, col0, scale, transpose):
    T = z.shape[0]
    tr = min(512, S)
    tcw = 512
    tiles_per_seq = S // tr
    hb = tr // CONV_HALO
    nhalo = T // CONV_HALO
    c0 = col0 * (M_WIDTH // tcw)
    kern = functools.partial(_conv_kernel, tr=tr, tiles_per_seq=tiles_per_seq, scale=scale, transpose=transpose)
    if transpose:
        out_spec = pl.BlockSpec((tcw, tr), lambda i, j: (j, i))
        out_shape = jax.ShapeDtypeStruct((M_WIDTH, T), BF16)
    else:
        out_spec = pl.BlockSpec((tr, tcw), lambda i, j: (i, j))
        out_shape = jax.ShapeDtypeStruct((T, M_WIDTH), BF16)
    return pl.pallas_call(
        kern,
        grid=(T // tr, M_WIDTH // tcw),
        in_specs=[
            pl.BlockSpec((tr, tcw), lambda i, j: (i, c0 + j)),
            pl.BlockSpec((CONV_HALO, tcw), lambda i, j: (jnp.maximum(i * hb - 1, 0), c0 + j)),
            pl.BlockSpec((CONV_HALO, tcw), lambda i, j: (jnp.minimum((i + 1) * hb, nhalo - 1), c0 + j)),
            pl.BlockSpec((M_CONV, tcw), lambda i, j: (0, c0 + j)),
            pl.BlockSpec((1, tcw), lambda i, j: (0, c0 + j)),
        ],
        out_specs=out_spec,
        out_shape=out_shape,
        scratch_shapes=[pltpu.VMEM((tr + 16, tcw), F32)],
        compiler_params=_cparams(("parallel", "parallel")),
        name="conv_silu_t" if transpose else "conv_silu",
    )(z, z, z, w["conv_w"], w["conv_b"])


def _log_sigmoid(x):
    return -(jnp.maximum(-x, 0.0) + jnp.log1p(jnp.exp(-jnp.abs(x))))


M_STATE_COLS = M_HEAD_DIM + LANES


def _mlstm_gates(d, g_ref, gt_ref):
    L = M_CHUNK
    r = lax.broadcasted_iota(I32, (L, L), 0)
    c = lax.broadcasted_iota(I32, (L, L), 1)
    if d == 0:
        mask = c <= r
    else:
        mask = c >= r
    tri_col = jnp.where(mask, 1.0, 0.0).astype(BF16)
    tri_row = jnp.where(r <= c if d == 0 else r >= c, 1.0, 0.0).astype(BF16)
    g = g_ref[...]
    gt = gt_ref[...]
    b_row_all = sum(_dot(p, tri_row) for p in _split3(_log_sigmoid(gt)))
    sel_r = lax.broadcasted_iota(I32, (4 * M_HEADS, M_HEADS * LANES), 0)
    sel_h = lax.broadcasted_iota(I32, (4 * M_HEADS, M_HEADS * LANES), 1) // LANES
    pick_i = jnp.where(sel_r == d * 2 * M_HEADS + sel_h, 1.0, 0.0).astype(BF16)
    pick_f = jnp.where(sel_r == d * 2 * M_HEADS + M_HEADS + sel_h, 1.0, 0.0).astype(BF16)
    i_bc_all = sum(_dot(p, pick_i) for p in _split3(g))
    b_bc_all = sum(_dot(tri_col, _dot(p, pick_f).astype(BF16)) for p in _split3(_log_sigmoid(g)))
    return mask, i_bc_all, b_bc_all, gt, b_row_all


def _mlstm_kernel(qf, ktf, vf, gf, gtf, qb, ktb, vb, gb, gtb, of, ob, c_scr, m_scr):
    @pl.when(pl.program_id(1) == 0)
    def _():
        c_scr[...] = jnp.zeros(c_scr.shape, F32)
        m_scr[...] = jnp.zeros(m_scr.shape, F32)

    L = M_CHUNK
    gates = (_mlstm_gates(0, gf, gtf), _mlstm_gates(1, gb, gtb))
    refs = ((qf, ktf, vf, of), (qb, ktb, vb, ob))
    units = [(d, h) for d in range(2) for h in range(M_HEADS)]
    sls = [slice(h * M_HEAD_DIM, (h + 1) * M_HEAD_DIM) for _, h in units]
    ones_blk = jnp.ones((L, LANES), BF16)

    qs = [refs[d][0][:, sl] for (d, _), sl in zip(units, sls)]
    kts = [refs[d][1][sl, :] for (d, _), sl in zip(units, sls)]
    vs = [refs[d][2][:, sl] for (d, _), sl in zip(units, sls)]
    csts = [c_scr[u] for u in range(len(units))]
    qk = [_dot(q, kt) for q, kt in zip(qs, kts)]
    qc = [_dot(q, cst.astype(BF16)) for q, cst in zip(qs, csts)]
    i_bc = [gates[d][1][:, h * LANES:(h + 1) * LANES] for d, h in units]
    b_bc = [gates[d][2][:, h * LANES:(h + 1) * LANES] for d, h in units]
    i_row = [gates[d][3][d * 2 * M_HEADS + h:d * 2 * M_HEADS + h + 1, :] for d, h in units]
    b_row = [gates[d][4][d * 2 * M_HEADS + M_HEADS + h:d * 2 * M_HEADS + M_HEADS + h + 1, :] for d, h in units]
    b_last = [b[(L - 1 if d == 0 else 0):(L if d == 0 else 1), :] for (d, _), b in zip(units, b_bc)]
    m_prev = [m_scr[u][0:1, :] for u in range(len(units))]
    gk = [bl - b + i for bl, b, i in zip(b_last, b_bc, i_bc)]
    m_new = [jnp.maximum(bl + mp, jnp.max(x, axis=0, keepdims=True)) for bl, mp, x in zip(b_last, m_prev, gk)]
    decay = [jnp.exp(bl + mp - mn) for bl, mp, mn in zip(b_last, m_prev, m_new)]
    wk = [jnp.exp(x - mn) for x, mn in zip(gk, m_new)]
    wv = [jnp.concatenate([(jnp.concatenate([x, x], axis=1) * v.astype(F32)).astype(BF16), x.astype(BF16)], axis=1)
          for x, v in zip(wk, vs)]
    upd = [_dot(kt, x) for kt, x in zip(kts, wv)]
    a = [b + mp for b, mp in zip(b_bc, m_prev)]
    dm = [jnp.where(gates[d][0], b - br + ir, -jnp.inf) for (d, _), b, br, ir in zip(units, b_bc, b_row, i_row)]
    m_t = [jnp.maximum(x, jnp.max(y, axis=1, keepdims=True)) for x, y in zip(a, dm)]
    w_inter = [jnp.exp(x - mt) for x, mt in zip(a, m_t)]
    s = [x * jnp.exp(y - mt) for x, y, mt in zip(qk, dm, m_t)]
    sv = [_dot(x.astype(BF16), jnp.concatenate([v, ones_blk], axis=1)) for x, v in zip(s, vs)]
    for u, (d, _) in enumerate(units):
        ne = sv[u] + jnp.concatenate([w_inter[u]] * 3, axis=1) * qc[u]
        inv = 1.0 / jnp.maximum(jnp.abs(ne[:, M_HEAD_DIM:]), jnp.exp(-m_t[u]))
        refs[d][3][:, sls[u]] = ne[:, :M_HEAD_DIM] * jnp.concatenate([inv, inv], axis=1)
        c_scr[u] = jnp.concatenate([decay[u]] * 3, axis=1) * csts[u] + upd[u]
        m_scr[u] = jnp.broadcast_to(m_new[u], (8, LANES))


def _mlstm(q, kt, z, gate, gatet, B, S):
    T = B * S
    L = M_CHUNK
    nc = S // L
    fwd = lambda b, c: b * nc + c
    bwd = lambda b, c: b * nc + nc - 1 - c

    def specs(pos):
        return [
            pl.BlockSpec((L, M_WIDTH), lambda b, c: (pos(b, c), 0)),
            pl.BlockSpec((M_WIDTH, L), lambda b, c: (0, pos(b, c))),
            pl.BlockSpec((L, M_WIDTH), lambda b, c: (pos(b, c), 2)),
            pl.BlockSpec((L, 16), lambda b, c: (pos(b, c), 0)),
            pl.BlockSpec((16, L), lambda b, c: (0, pos(b, c))),
        ]

    nstate = 2 * M_HEADS
    return pl.pallas_call(
        _mlstm_kernel,
        grid=(B, nc),
        in_specs=specs(fwd) + specs(bwd),
        out_specs=[
            pl.BlockSpec((L, M_WIDTH), lambda b, c: (fwd(b, c), 0)),
            pl.BlockSpec((L, M_WIDTH), lambda b, c: (bwd(b, c), 0)),
        ],
        out_shape=[jax.ShapeDtypeStruct((T, M_WIDTH), F32)] * 2,
        scratch_shapes=[
            pltpu.VMEM((nstate, M_HEAD_DIM, M_STATE_COLS), F32),
            pltpu.VMEM((nstate, 8, LANES), F32),
        ],
        compiler_params=_cparams(("parallel", "arbitrary")),
        name="mlstm",
    )(q, kt, z, gate, gatet, q, kt, z, gate, gatet)


ATT_TILE = 512


def _flash_tiles(S):
    tk = min(ATT_TILE, S)
    nk = S // tk
    tq = min(1024 if nk <= 8 else 512, S)
    unroll = 4 * ATT_SLOTS if nk >= 16 else 2 * ATT_SLOTS
    return tq, tk, unroll


ATT_SLOTS = 2
ATT_LOGIT_SCALE = (A_NOPE + A_ROPE) ** -0.5 * 1.4426950408889634


def _mla_proj_kernel(c_ref, kr_ref, cost_ref, sint_ref, gq_ref, gkv_ref, wqat_ref, wqrt_ref, wk_ref, wvt_ref,
                     qt_ref, k_ref, vt_ref):
    cq = c_ref[:, :A_Q_RANK].astype(F32)
    ckv = c_ref[:, A_Q_RANK:].astype(F32)
    cqn = _rms(cq, gq_ref[...]).astype(BF16)
    ckvn = _rms(ckv, gkv_ref[...]).astype(BF16)
    qat = _dot_nt(wqat_ref[...], cqn)
    qrt = _dot_nt(wqrt_ref[...], cqn)
    kn = _dot(ckvn, wk_ref[...])
    vt = _dot_nt(wvt_ref[...], ckvn)
    cost = cost_ref[...]
    sint = sint_ref[...]
    kr = kr_ref[...]
    for h in range(A_HEADS):
        o = h * A_QK_PAD
        qt_ref[0, h, :LANES, :] = (qat[o:o + LANES, :] * ATT_LOGIT_SCALE).astype(BF16)
        qt_ref[0, h, LANES:, :] = ((qat[o + LANES:o + 2 * LANES, :] * cost
                                    + qrt[h * LANES:(h + 1) * LANES, :] * sint) * ATT_LOGIT_SCALE).astype(BF16)
        k_ref[0, h, :, :LANES] = kn[:, h * A_NOPE:(h + 1) * A_NOPE].astype(BF16)
        k_ref[0, h, :, LANES:] = kr
        vt_ref[0, h, 0] = vt[h * A_V:(h + 1) * A_V, :].astype(BF16)


def _mla_proj(c, kr, B, S, w):
    tm = min(ATT_TILE, S)
    nseq = S // tm
    row = lambda i: (i, 0)
    const = lambda i: (0, 0)
    seq_t = lambda i: (0, i % nseq)
    return pl.pallas_call(
        _mla_proj_kernel,
        grid=(B * S // tm,),
        in_specs=[
            pl.BlockSpec((tm, 512), row),
            pl.BlockSpec((tm, LANES), row),
            pl.BlockSpec((LANES, tm), seq_t),
            pl.BlockSpec((LANES, tm), seq_t),
            pl.BlockSpec((1, A_Q_RANK), const),
            pl.BlockSpec((1, A_KV_RANK), const),
            pl.BlockSpec((A_HEADS * A_QK_PAD, A_Q_RANK), const),
            pl.BlockSpec((A_HEADS * LANES, A_Q_RANK), const),
            pl.BlockSpec((A_KV_RANK, A_HEADS * A_NOPE), const),
            pl.BlockSpec((A_HEADS * A_V, A_KV_RANK), const),
        ],
        out_specs=[
            pl.BlockSpec((1, A_HEADS, A_QK_PAD, tm), lambda i: (i // nseq, 0, 0, i % nseq)),
            pl.BlockSpec((1, A_HEADS, tm, A_QK_PAD), lambda i: (i // nseq, 0, i % nseq, 0)),
            pl.BlockSpec((1, A_HEADS, 1, A_V, tm), lambda i: (i // nseq, 0, i % nseq, 0, 0)),
        ],
        out_shape=[
            jax.ShapeDtypeStruct((B, A_HEADS, A_QK_PAD, S), BF16),
            jax.ShapeDtypeStruct((B, A_HEADS, S, A_QK_PAD), BF16),
            jax.ShapeDtypeStruct((B, A_HEADS, nseq, A_V, tm), BF16),
        ],
        compiler_params=_cparams(("parallel",)),
        name="mla_proj",
    )(c, kr, w["cos_t"], w["sin_t"], w["g_cq"], w["g_ckv"], w["w_qat"], w["w_qrt"], w["w_uk"], w["w_uvt"])


def _flash_kernel(q_ref, k_ref, vt_ref, o_ref, s_scr, *, tk, nk, unroll):
    qt = q_ref[0, 0]
    tq = qt.shape[1]

    def scores(j, slot):
        start = pl.multiple_of(j * tk, tk)
        s_scr[slot] = _dot(k_ref[0, 0, pl.ds(start, tk), :], qt)

    def accumulate(j, slot, carry):
        m, l, acc = carry
        s = s_scr[slot]
        m_new = jnp.maximum(m, jnp.max(s, axis=0, keepdims=True))
        p = jnp.exp2(s - m_new)
        alpha = jnp.exp2(m - m_new)
        l = alpha * l + jnp.sum(p, axis=0, keepdims=True)
        acc = alpha * acc + _dot(vt_ref[0, 0, j], p.astype(BF16))
        return m_new, l, acc

    nslot = s_scr.shape[0]
    ahead = nslot - 1

    def body(jj, carry):
        for t in range(unroll):
            j = unroll * jj + t
            scores(jnp.minimum(j + ahead, nk - 1), (t + ahead) % nslot)
            carry = accumulate(j, t % nslot, carry)
        return carry

    carry = (jnp.full((1, tq), -jnp.inf, F32), jnp.zeros((1, tq), F32), jnp.zeros((A_V, tq), F32))
    for j in range(min(ahead, nk)):
        scores(j, j % nslot)
    n_loop = nk // unroll
    if n_loop > 0:
        carry = lax.fori_loop(0, n_loop, body, carry)
    for j in range(unroll * n_loop, nk):
        if j + ahead < nk:
            scores(j + ahead, (j + ahead) % nslot)
        carry = accumulate(j, j % nslot, carry)
    _, l, acc = carry
    o_ref[...] = (acc / l).T.astype(BF16)


def _flash(qt, k, vt):
    B, H, S, _ = k.shape
    tq, tk, unroll = _flash_tiles(S)
    nq = S // tq
    nk = S // tk
    kern = functools.partial(_flash_kernel, tk=tk, nk=nk, unroll=unroll)
    return pl.pallas_call(
        kern,
        grid=(B, H, nq),
        in_specs=[
            pl.BlockSpec((1, 1, A_QK_PAD, tq), lambda b, h, i: (b, h, 0, i)),
            pl.BlockSpec((1, 1, S, A_QK_PAD), lambda b, h, i: (b, h, 0, 0)),
            pl.BlockSpec((1, 1, nk, A_V, tk), lambda b, h, i: (b, h, 0, 0, 0)),
        ],
        out_specs=pl.BlockSpec((tq, A_V), lambda b, h, i: (b * nq + i, h)),
        out_shape=jax.ShapeDtypeStruct((B * S, A_HEADS * A_V), BF16),
        scratch_shapes=[pltpu.VMEM((ATT_SLOTS, tk, tq), F32)],
        compiler_params=_cparams(("parallel", "parallel", "arbitrary")),
        name="mla_flash",
    )(qt, k, vt)


def _mixer_out_kernel(hf_ref, hb_ref, om_ref, ga_ref, gb_ref, att_ref, x_ref, mg_ref,
                      wa_ref, wb_ref, wo_ref, o_ref):
    hs = hf_ref[...] + hb_ref[...]
    mg = mg_ref[...]
    parts = []
    for h in range(M_HEADS):
        sl = slice(h * M_HEAD_DIM, (h + 1) * M_HEAD_DIM)
        parts.append(_rms(hs[:, sl], mg[:, sl]))
    hn = jnp.concatenate(parts, axis=1) * _sigmoid(om_ref[...].astype(F32))
    y_a = _dot(hn.astype(BF16), wa_ref[...])
    y_b = _dot(att_ref[...], wb_ref[...])
    merged = _sigmoid(ga_ref[...].astype(F32)) * y_a + _sigmoid(gb_ref[...].astype(F32)) * y_b
    o_ref[...] = x_ref[...] + _dot(merged.astype(BF16), wo_ref[...])


def _mixer_out(hf, hb, z, att, x, w):
    T = x.shape[0]
    tm = min(512, T)
    row = lambda i: (i, 0)
    const = lambda i: (0, 0)
    wspec = pl.BlockSpec((D_MODEL, D_MODEL), const)
    return pl.pallas_call(
        _mixer_out_kernel,
        grid=(T // tm,),
        in_specs=[
            pl.BlockSpec((tm, M_WIDTH), row),
            pl.BlockSpec((tm, M_WIDTH), row),
            pl.BlockSpec((tm, M_WIDTH), lambda i: (i, 3)),
            pl.BlockSpec((tm, D_MODEL), lambda i: (i, 4)),
            pl.BlockSpec((tm, D_MODEL), lambda i: (i, 5)),
            pl.BlockSpec((tm, D_MODEL), row),
            pl.BlockSpec((tm, D_MODEL), row),
            pl.BlockSpec((1, M_WIDTH), const),
            wspec, wspec, wspec,
        ],
        out_specs=pl.BlockSpec((tm, D_MODEL), row),
        out_shape=jax.ShapeDtypeStruct((T, D_MODEL), F32),
        compiler_params=_cparams(("parallel",)),
        name="mixer_out",
    )(hf, hb, z, z, z, att, x, w["mh_norm_g"], w["w_br_a"], w["w_br_b"], w["w_out"])


def _mem_kernel(m_ref, g_ref, wk_ref, wv_ref, k_ref, v_ref):
    mn = _rms(m_ref[...], g_ref[...]).astype(BF16)
    k_ref[...] = _dot(mn, wk_ref[...]).astype(BF16)
    v_ref[...] = _dot(mn, wv_ref[...]).astype(BF16)


def _mem_proj(mem, w):
    R = mem.shape[0]
    tm = 256
    row = lambda i: (i, 0)
    const = lambda i: (0, 0)
    wspec = pl.BlockSpec((D_MODEL, D_MODEL), const)
    return pl.pallas_call(
        _mem_kernel,
        grid=(R // tm,),
        in_specs=[pl.BlockSpec((tm, D_MODEL), row), pl.BlockSpec((1, D_MODEL), const), wspec, wspec],
        out_specs=[pl.BlockSpec((tm, D_MODEL), row)] * 2,
        out_shape=[jax.ShapeDtypeStruct((R, D_MODEL), BF16)] * 2,
        compiler_params=_cparams(("parallel",)),
        name="mem_proj",
    )(mem, w["norm_mem_g"], w["w_xk"], w["w_xv"])


def _cross_router_kernel(x_ref, kx_ref, vx_ref, gx_ref, gf_ref, wq_ref, wo_ref, wr_ref, wrt_ref,
                         br_ref, brt_ref, x2_ref, xn_ref, aff_ref, afft_ref):
    x1 = x_ref[...]
    xn = _rms(x1, gx_ref[...]).astype(BF16)
    q = _dot(xn, wq_ref[...])
    sls = [slice(h * X_HEAD_DIM, (h + 1) * X_HEAD_DIM) for h in range(X_HEADS)]
    ss = [_dot_nt(q[:, sl].astype(BF16), kx_ref[:, sl]) * (X_HEAD_DIM ** -0.5) for sl in sls]
    es = [jnp.exp(s - jnp.max(s, axis=1, keepdims=True)) for s in ss]
    ps = [e / jnp.sum(e, axis=1, keepdims=True) for e in es]
    o = jnp.concatenate([_dot(p.astype(BF16), vx_ref[:, sl]) for p, sl in zip(ps, sls)], axis=1)
    x2 = x1 + _dot(o.astype(BF16), wo_ref[...])
    x2_ref[...] = x2
    xf = _rms(x2, gf_ref[...])
    hi = xf.astype(BF16)
    xn_ref[...] = hi
    lo = (xf - hi.astype(F32)).astype(BF16)
    wr = wr_ref[...]
    wr_hi = wr.astype(BF16)
    wr_lo = (wr - wr_hi.astype(F32)).astype(BF16)
    logit = _dot(hi, wr_hi) + _dot(lo, wr_hi) + _dot(hi, wr_lo) + br_ref[...]
    e = jnp.exp(logit - jnp.max(logit, axis=1, keepdims=True))
    aff_ref[...] = e / jnp.sum(e, axis=1, keepdims=True)
    wrt = wrt_ref[...]
    wrt_hi = wrt.astype(BF16)
    wrt_lo = (wrt - wrt_hi.astype(F32)).astype(BF16)
    logit_t = _dot_nt(wrt_hi, hi) + _dot_nt(wrt_hi, lo) + _dot_nt(wrt_lo, hi) + brt_ref[...]
    et = jnp.exp(logit_t - jnp.max(logit_t, axis=0, keepdims=True))
    afft_ref[...] = et / jnp.sum(et, axis=0, keepdims=True)


def _cross_router(x1, kx, vx, S, n_mem, w):
    T = x1.shape[0]
    tm = min(512, S)
    nseq = S // tm
    row = lambda i: (i, 0)
    const = lambda i: (0, 0)
    wspec = pl.BlockSpec((D_MODEL, D_MODEL), const)
    memspec = pl.BlockSpec((n_mem, D_MODEL), lambda i: (i // nseq, 0))
    return pl.pallas_call(
        _cross_router_kernel,
        grid=(T // tm,),
        in_specs=[
            pl.BlockSpec((tm, D_MODEL), row), memspec, memspec,
            pl.BlockSpec((1, D_MODEL), const), pl.BlockSpec((1, D_MODEL), const),
            wspec, wspec,
            pl.BlockSpec((D_MODEL, N_EXPERTS), const), pl.BlockSpec((N_EXPERTS, D_MODEL), const),
            pl.BlockSpec((1, N_EXPERTS), const), pl.BlockSpec((N_EXPERTS, 1), const),
        ],
        out_specs=[
            pl.BlockSpec((tm, D_MODEL), row),
            pl.BlockSpec((tm, D_MODEL), row),
            pl.BlockSpec((tm, N_EXPERTS), row),
            pl.BlockSpec((N_EXPERTS, tm), lambda i: (0, i)),
        ],
        out_shape=[
            jax.ShapeDtypeStruct((T, D_MODEL), F32),
            jax.ShapeDtypeStruct((T, D_MODEL), BF16),
            jax.ShapeDtypeStruct((T, N_EXPERTS), F32),
            jax.ShapeDtypeStruct((N_EXPERTS, T), F32),
        ],
        compiler_params=_cparams(("parallel",)),
        name="cross_router",
    )(x1, kx, vx, w["norm_x_g"], w["norm_ffn_g"], w["w_xq"], w["w_xo"], w["w_router"], w["w_router_t"],
      w["b_router"], w["b_router_t"])


def _excl_cumsum(mask_f, strict_lane, strict_blk):
    nb = mask_f.shape[0]
    within = _dot(mask_f.astype(BF16), strict_lane)
    tot = jnp.sum(mask_f, axis=1, keepdims=True)
    bstart = _dot(strict_blk, jnp.broadcast_to(tot, (nb, LANES)).astype(BF16))
    return within + bstart, bstart


def _select_kernel(aff_ref, pos_ref, bst_ref, *, cap):
    a = aff_ref[0]
    nb = a.shape[0]
    bits = pltpu.bitcast(a, I32)

    def radix(i, prefix):
        cand = prefix | jnp.left_shift(jnp.int32(1), 30 - i)
        cnt = jnp.sum(jnp.where(bits >= cand, 1.0, 0.0), axis=(0, 1), keepdims=True)
        return jnp.where(cnt >= cap, cand, prefix)

    thr = lax.fori_loop(0, 31, radix, jnp.zeros((1, 1), I32))
    gt = bits > thr
    eq = bits == thr
    need = cap - jnp.sum(jnp.where(gt, 1.0, 0.0), axis=(0, 1), keepdims=True)
    li = lax.broadcasted_iota(I32, (LANES, LANES), 0)
    lj = lax.broadcasted_iota(I32, (LANES, LANES), 1)
    strict_lane = jnp.where(li < lj, 1.0, 0.0).astype(BF16)
    bi = lax.broadcasted_iota(I32, (nb, nb), 0)
    bj = lax.broadcasted_iota(I32, (nb, nb), 1)
    strict_blk = jnp.where(bj < bi, 1.0, 0.0).astype(BF16)
    rank, _ = _excl_cumsum(jnp.where(eq, 1.0, 0.0), strict_lane, strict_blk)
    sel = gt | (eq & (rank < need))
    pos, bstart = _excl_cumsum(jnp.where(sel, 1.0, 0.0), strict_lane, strict_blk)
    pos_ref[0] = jnp.where(sel, pos.astype(I32), -1)
    col = jnp.broadcast_to(bstart[:, 0:1], (nb, nb))
    bst_ref[0] = jnp.sum(jnp.where(bi == bj, col, 0.0), axis=0, keepdims=True).astype(I32)


def _select(aff_t, cap):
    E, T = aff_t.shape
    nb = T // LANES
    kern = functools.partial(_select_kernel, cap=cap)
    return pl.pallas_call(
        kern,
        grid=(E,),
        in_specs=[pl.BlockSpec((1, nb, LANES), lambda e: (e, 0, 0))],
        out_specs=[
            pl.BlockSpec((1, nb, LANES), lambda e: (e, 0, 0)),
            pl.BlockSpec((1, 1, nb), lambda e: (e, 0, 0)),
        ],
        out_shape=[
            jax.ShapeDtypeStruct((E, nb, LANES), I32),
            jax.ShapeDtypeStruct((E, 1, nb), I32),
        ],
        compiler_params=_cparams(("parallel",)),
        name="ec_select",
    )(aff_t.reshape(E, nb, LANES))


def _cells(bst, cap, T, tu, tc):
    E = bst.shape[0]
    ns = cap // tc
    start = bst[:, ::tu // LANES]
    end = jnp.concatenate([start[:, 1:], jnp.full((E, 1), cap, I32)], axis=1)
    cnt = end - start
    s_lo = jnp.minimum(start // tc, ns - 1)
    s_hi = jnp.where(cnt > 0, (end - 1) // tc, s_lo)
    return jnp.where(cnt > 0, s_hi - s_lo + 1, 0), s_lo


def _enumerate_cells(nc_flat, slo_flat, steps):
    off_end = jnp.cumsum(nc_flat, axis=-1)
    off = off_end - nc_flat
    total = off_end[..., -1:]
    k = jnp.arange(steps, dtype=I32)
    kk = jnp.minimum(k, total - 1)
    grp = jnp.sum((off_end[..., None, :] <= kk[..., :, None]).astype(I32), axis=-1)
    s = jnp.take_along_axis(slo_flat, grp, axis=-1) + kk - jnp.take_along_axis(off, grp, axis=-1)
    return grp, s, (k < total).astype(I32)


def _gather_tables(bst, cap, T, tu, tc):
    E = bst.shape[0]
    ncell, s_lo = _cells(bst, cap, T, tu, tc)
    g_u, g_s, g_valid = _enumerate_cells(ncell, s_lo, T // tu + cap // tc)
    prev_s = jnp.concatenate([jnp.full((E, 1), -1, I32), g_s[:, :-1]], axis=1)
    g_first = g_valid * (g_s != prev_s).astype(I32)
    return tuple(t.reshape(-1) for t in (g_u, g_s, g_valid, g_first))


def _combine_tables(bst, cap, T, tu, tc):
    E = bst.shape[0]
    nu = T // tu
    ns = cap // tc
    G = MOE_COMBINE_GROUP
    ncell, s_lo = _cells(bst, cap, T, tu, tc)
    ncell_c = ncell.at[0].set(jnp.maximum(ncell[0], 1))
    pc = E * (nu + ns)
    c_grp, c_s, _ = _enumerate_cells(ncell_c.T.reshape(-1), s_lo.T.reshape(-1), pc)
    c_e = c_grp % E
    cells_u = jnp.sum(ncell_c, axis=0)
    cell_off = jnp.cumsum(cells_u) - cells_u
    groups_u = (cells_u + G - 1) // G
    grp_end = jnp.cumsum(groups_u)
    n_steps = pc // G + nu
    k = jnp.arange(n_steps, dtype=I32)
    kk = jnp.minimum(k, grp_end[-1] - 1)
    t_u = jnp.sum((grp_end[None, :] <= kk[:, None]).astype(I32), axis=1)
    j = kk - (grp_end - groups_u)[t_u]
    t_valid = (k < grp_end[-1]).astype(I32)
    local = j[:, None] * G + jnp.arange(G, dtype=I32)[None, :]
    cell_ok = (local < cells_u[t_u][:, None]).astype(I32) * t_valid[:, None]
    cid = jnp.minimum(cell_off[t_u][:, None] + jnp.minimum(local, cells_u[t_u][:, None] - 1), pc - 1)
    t_first = t_valid * (j == 0).astype(I32)
    t_last = t_valid * (j == groups_u[t_u] - 1).astype(I32)
    return (t_u, c_e[cid].reshape(-1), c_s[cid].reshape(-1), cell_ok.reshape(-1), t_valid, t_first, t_last)


def _gather_kernel(u_tab, s_tab, valid_tab, first_tab, pos_ref, x_ref, o_ref, *, steps, tc):
    step = pl.program_id(0) * steps + pl.program_id(1)
    tu = pos_ref.shape[-1]
    slot = lax.broadcasted_iota(I32, (tc, tu), 0) + s_tab[step] * tc
    onehot = jnp.where(pos_ref[0] == slot, 1.0, 0.0).astype(BF16)

    @pl.when(first_tab[step] == 1)
    def _():
        o_ref[0] = _dot(onehot, x_ref[...]).astype(BF16)

    @pl.when((valid_tab[step] == 1) & (first_tab[step] == 0))
    def _():
        o_ref[0] = o_ref[0] + _dot(onehot, x_ref[...]).astype(BF16)


def _moe_gather(xn, pos_row, tabs, cap, tu, tc):
    T = xn.shape[0]
    E = N_EXPERTS
    steps = T // tu + cap // tc
    kern = functools.partial(_gather_kernel, steps=steps, tc=tc)
    grid_spec = pltpu.PrefetchScalarGridSpec(
        num_scalar_prefetch=4,
        grid=(E, steps),
        in_specs=[
            pl.BlockSpec((1, 1, tu), lambda e, k, u, s, v, f: (e, 0, u[e * steps + k])),
            pl.BlockSpec((tu, D_MODEL), lambda e, k, u, s, v, f: (u[e * steps + k], 0)),
        ],
        out_specs=pl.BlockSpec((1, tc, D_MODEL), lambda e, k, u, s, v, f: (e, s[e * steps + k], 0)),
    )
    return pl.pallas_call(
        kern,
        grid_spec=grid_spec,
        out_shape=jax.ShapeDtypeStruct((E, cap, D_MODEL), BF16),
        compiler_params=_cparams(("parallel", "arbitrary")),
        name="moe_gather",
    )(*tabs, pos_row, xn)


def _ffn_kernel(x_ref, wg_ref, wu_ref, wd_ref, o_ref, wg_b, wu_b, wd_b):
    @pl.when(pl.program_id(1) == 0)
    def _():
        wg_b[...] = wg_ref[0].astype(BF16)
        wu_b[...] = wu_ref[0].astype(BF16)
        wd_b[...] = wd_ref[0].astype(BF16)

    x = x_ref[0]
    g = _dot(x, wg_b[...])
    u = _dot(x, wu_b[...])
    h = (g * _sigmoid(g) * u).astype(BF16)
    o_ref[0] = _dot(h, wd_b[...]).astype(BF16)


def _moe_ffn(xe, w):
    E, cap, _ = xe.shape
    tf = min(512, cap)
    wspec = pl.BlockSpec((1, D_MODEL, E_FF), lambda e, i: (e, 0, 0))
    return pl.pallas_call(
        _ffn_kernel,
        grid=(E, cap // tf),
        in_specs=[pl.BlockSpec((1, tf, D_MODEL), lambda e, i: (e, i, 0)), wspec, wspec,
                  pl.BlockSpec((1, E_FF, D_MODEL), lambda e, i: (e, 0, 0))],
        out_specs=pl.BlockSpec((1, tf, D_MODEL), lambda e, i: (e, i, 0)),
        out_shape=jax.ShapeDtypeStruct((E, cap, D_MODEL), BF16),
        scratch_shapes=[pltpu.VMEM((D_MODEL, E_FF), BF16), pltpu.VMEM((D_MODEL, E_FF), BF16),
                        pltpu.VMEM((E_FF, D_MODEL), BF16)],
        compiler_params=_cparams(("parallel", "arbitrary")),
        name="moe_ffn",
    )(xe, w["w_e_gate"], w["w_e_up"], w["w_e_down"])


def _combine_kernel(u_tab, e_tab, s_tab, ok_tab, valid_tab, first_tab, last_tab, pos_ref, aff_ref, *rest, tc):
    G = MOE_COMBINE_GROUP
    ye_refs = rest[:G]
    x_ref, g_ref, o_ref, acc = rest[G:]
    step = pl.program_id(0)
    tu = pos_ref.shape[0]

    @pl.when(first_tab[step] == 1)
    def _():
        acc[...] = jnp.zeros(acc.shape, F32)

    @pl.when(valid_tab[step] == 1)
    def _():
        pos = pos_ref[...].astype(F32)
        aff = aff_ref[...]
        lane_e = lax.broadcasted_iota(I32, pos.shape, 1)
        lane_r = lax.broadcasted_iota(I32, (tu, tc), 1).astype(F32)
        hits = []
        for g in range(G):
            c = step * G + g
            mine = lane_e == e_tab[c]
            slot = jnp.sum(jnp.where(mine, pos, 0.0), axis=1, keepdims=True)
            gate = jnp.sum(jnp.where(mine, aff, 0.0), axis=1, keepdims=True)
            base = jnp.where(ok_tab[c] == 1, s_tab[c] * tc, -2 * tc).astype(F32)
            hits.append(jnp.where(slot - base == lane_r, gate, 0.0).astype(BF16))
        ye = jnp.concatenate([r[0] for r in ye_refs], axis=0)
        acc[...] = acc[...] + _dot(jnp.concatenate(hits, axis=1), ye)

    @pl.when(last_tab[step] == 1)
    def _():
        o_ref[...] = _rms(x_ref[...] + acc[...], g_ref[...])


def _moe_combine(ye, pos_col, aff, x2, tabs, final_g, cap, tu, tc):
    T = x2.shape[0]
    E = N_EXPERTS
    G = MOE_COMBINE_GROUP
    steps = E * (T // tu + cap // tc) // G + T // tu
    kern = functools.partial(_combine_kernel, tc=tc)

    def ye_spec(g):
        return pl.BlockSpec((1, tc, D_MODEL), lambda k, u, e, s, *_: (e[k * G + g], s[k * G + g], 0))

    tile = pl.BlockSpec((tu, D_MODEL), lambda k, u, *_: (u[k], 0))
    per_expert = pl.BlockSpec((tu, N_EXPERTS), lambda k, u, *_: (u[k], 0))
    grid_spec = pltpu.PrefetchScalarGridSpec(
        num_scalar_prefetch=7,
        grid=(steps,),
        in_specs=[per_expert, per_expert]
        + [ye_spec(g) for g in range(G)]
        + [tile, pl.BlockSpec((1, D_MODEL), lambda k, *_: (0, 0))],
        out_specs=tile,
        scratch_shapes=[pltpu.VMEM((tu, D_MODEL), F32)],
    )
    return pl.pallas_call(
        kern,
        grid_spec=grid_spec,
        out_shape=jax.ShapeDtypeStruct((T, D_MODEL), F32),
        compiler_params=_cparams(("arbitrary",)),
        name="moe_combine",
    )(*tabs, pos_col, aff, *([ye] * G), x2, final_g)


def _rope_tables(S):
    pos = jnp.arange(S, dtype=F32)
    inv = ROPE_BASE ** (-jnp.arange(0, A_ROPE, 2, dtype=F32) / A_ROPE)
    ang = pos[:, None] * inv[None, :]
    pad = jnp.zeros((S, LANES - A_ROPE), F32)
    cos = jnp.concatenate([jnp.cos(ang), jnp.cos(ang), pad], axis=1)
    sin = jnp.concatenate([jnp.sin(ang), jnp.sin(ang), pad], axis=1)
    return cos, sin


def _rotate_half_cols(w):
    half = A_ROPE // 2
    return jnp.concatenate([-w[..., half:], w[..., :half]], axis=-1)


def _prep_weights(norm_mix_g, w_in, b_gates, conv_w, conv_b, mh_norm_g, g_cq, g_ckv, w_uq, w_ukv,
                  w_br_a, w_br_b, w_out, norm_x_g, norm_mem_g, w_xq, w_xk, w_xv, w_xo, norm_ffn_g,
                  w_router, b_router, w_e_gate, w_e_up, w_e_down, final_norm_g):
    l = 0
    wi = w_in[l]
    o = 0
    cols = {}
    for name, n in (("qm", M_WIDTH), ("km", M_WIDTH), ("vm", M_WIDTH), ("om", M_WIDTH), ("gates", 4 * M_HEADS),
                    ("cq", A_Q_RANK), ("ckv", A_KV_RANK), ("kr", A_ROPE), ("ga", D_MODEL), ("gb", D_MODEL)):
        cols[name] = wi[:, o:o + n]
        o += n
    zpad = jnp.zeros((D_MODEL, LANES - A_ROPE), F32)
    w_kr = jnp.concatenate([cols["kr"], zpad, _rotate_half_cols(cols["kr"]), zpad], axis=1)
    uq = w_uq[l].reshape(A_Q_RANK, A_HEADS, A_NOPE + A_ROPE)
    uq_rope = uq[:, :, A_NOPE:]
    hpad = jnp.zeros((A_Q_RANK, A_HEADS, LANES - A_ROPE), F32)
    w_qa = jnp.concatenate([uq, hpad], axis=2).reshape(A_Q_RANK, A_HEADS * A_QK_PAD)
    w_qr = jnp.concatenate([_rotate_half_cols(uq_rope), hpad], axis=2).reshape(A_Q_RANK, A_HEADS * LANES)
    ukv = w_ukv[l].reshape(A_KV_RANK, A_HEADS, A_NOPE + A_V)
    row = lambda v: v.reshape(1, -1).astype(F32)
    return {
        "norm_mix_g": row(norm_mix_g[l]),
        "w_big": jnp.concatenate([cols[n] for n in ("qm", "km", "vm", "om", "ga", "gb")], axis=1).astype(BF16),
        "w_c": jnp.concatenate([cols["cq"], cols["ckv"]], axis=1).astype(BF16),
        "w_kr": w_kr.astype(BF16),
        "w_g": cols["gates"].astype(BF16),
        "w_gt": cols["gates"].T.astype(BF16),
        "b_g": row(b_gates[l]),
        "b_gt": b_gates[l].reshape(-1, 1).astype(F32),
        "conv_w": conv_w[l],
        "conv_b": row(conv_b[l]),
        "mh_norm_g": row(mh_norm_g[l]),
        "g_cq": row(g_cq[l]),
        "g_ckv": row(g_ckv[l]),
        "w_qat": w_qa.T.astype(BF16),
        "w_qrt": w_qr.T.astype(BF16),
        "w_uk": ukv[:, :, :A_NOPE].reshape(A_KV_RANK, A_HEADS * A_NOPE).astype(BF16),
        "w_uvt": ukv[:, :, A_NOPE:].reshape(A_KV_RANK, A_HEADS * A_V).T.astype(BF16),
        "w_br_a": w_br_a[l].astype(BF16),
        "w_br_b": w_br_b[l].astype(BF16),
        "w_out": w_out[l].astype(BF16),
        "norm_x_g": row(norm_x_g[l]),
        "norm_mem_g": row(norm_mem_g[l]),
        "w_xq": w_xq[l].astype(BF16),
        "w_xk": w_xk[l].astype(BF16),
        "w_xv": w_xv[l].astype(BF16),
        "w_xo": w_xo[l].astype(BF16),
        "norm_ffn_g": row(norm_ffn_g[l]),
        "w_router": w_router[l],
        "w_router_t": w_router[l].T,
        "b_router": row(b_router[l]),
        "b_router_t": b_router[l].reshape(-1, 1).astype(F32),
        "w_e_gate": w_e_gate[l],
        "w_e_up": w_e_up[l],
        "w_e_down": w_e_down[l],
        "final_norm_g": row(final_norm_g),
    }


def _trunk(x, mem, w):
    B, S, _ = x.shape
    T = B * S
    n_mem = mem.shape[1]
    w = dict(w)
    w["cos"], w["sin"] = _rope_tables(S)
    w["cos_t"], w["sin_t"] = w["cos"].T, w["sin"].T
    x2d = x.reshape(T, D_MODEL)

    z, c, kr, gate, gatet = _inproj(x2d, S, w)
    qm = _conv(z, S, w, col0=0, scale=M_HEAD_DIM ** -0.5, transpose=False)
    kmt = _conv(z, S, w, col0=1, scale=1.0, transpose=True)
    hf, hb = _mlstm(qm, kmt, z, gate, gatet, B, S)
    qc, kc, vc = _mla_proj(c, kr, B, S, w)
    att = _flash(qc, kc, vc)
    x1 = _mixer_out(hf, hb, z, att, x2d, w)

    kx, vx = _mem_proj(mem.reshape(B * n_mem, D_MODEL), w)
    x2, xn, aff, aff_t = _cross_router(x1, kx, vx, S, n_mem, w)

    cap = max(1, EC_FACTOR * T // N_EXPERTS)
    pos, bst = _select(aff_t, cap)
    tu = min(MOE_TOKEN_TILE, T)
    tc = min(MOE_SLOT_TILE, cap)
    tuc = min(MOE_COMBINE_TOKEN_TILE, T)
    tcc = min(MOE_COMBINE_SLOT_TILE, cap)
    bst = bst.reshape(N_EXPERTS, -1)
    xe = _moe_gather(xn, pos.reshape(N_EXPERTS, 1, T), _gather_tables(bst, cap, T, tu, tc), cap, tu, tc)
    ye = _moe_ffn(xe, w)
    y = _moe_combine(ye, pos.reshape(N_EXPERTS, T).T, aff, x2, _combine_tables(bst, cap, T, tuc, tcc),
                     w["final_norm_g"], cap, tuc, tcc)
    return y.reshape(B, S, D_MODEL)


def kernel(x_prompt, x_sample, mem_prompt, mem_sample, norm_mix_g, w_in, b_gates, conv_w, conv_b, mh_norm_g, g_cq, g_ckv, w_uq, w_ukv, w_br_a, w_br_b, w_out, norm_x_g, norm_mem_g, w_xq, w_xk, w_xv, w_xo, norm_ffn_g, w_router, b_router, w_e_gate, w_e_up, w_e_down, final_norm_g):
    w = _prep_weights(norm_mix_g, w_in, b_gates, conv_w, conv_b, mh_norm_g, g_cq, g_ckv, w_uq, w_ukv,
                      w_br_a, w_br_b, w_out, norm_x_g, norm_mem_g, w_xq, w_xk, w_xv, w_xo, norm_ffn_g,
                      w_router, b_router, w_e_gate, w_e_up, w_e_down, final_norm_g)
    return (_trunk(x_prompt, mem_prompt, w), _trunk(x_sample, mem_sample, w))
```

```python
import functools

import jax
import jax.numpy as jnp
from jax import lax
from jax.experimental import pallas as pl
from jax.experimental.pallas import tpu as pltpu

F32 = jnp.float32
BF16 = jnp.bfloat16
I32 = jnp.int32

D_MODEL = 1024
M_WIDTH = 1024
M_HEADS = 4
M_HEAD_DIM = M_WIDTH // M_HEADS
M_CHUNK = 128
M_CONV = 5
A_HEADS = 8
A_NOPE = 128
A_ROPE = 64
A_V = 128
A_Q_RANK = 256
A_KV_RANK = 256
A_QK_PAD = 256
ROPE_BASE = 10000.0
X_HEADS = 4
X_HEAD_DIM = D_MODEL // X_HEADS
N_EXPERTS = 16
EC_FACTOR = 2
E_FF = 1024
NORM_EPS = 1e-6
LANES = 128
BIG_COLS = 6 * 1024
VMEM_LIMIT = 56 * 1024 * 1024
MOE_TOKEN_TILE = 1024
MOE_COMBINE_TOKEN_TILE = 512
MOE_SLOT_TILE = 256
MOE_GATHER_GROUP = 4
MOE_COMBINE_SLOT_TILE = 128
MOE_COMBINE_GROUP = 8


def _cparams(sem):
    return pltpu.CompilerParams(dimension_semantics=sem, vmem_limit_bytes=VMEM_LIMIT)


def _dot(a, b):
    return jnp.dot(a, b, preferred_element_type=F32)


def _dot_nt(a, b):
    return lax.dot_general(a, b, (((1,), (1,)), ((), ())), preferred_element_type=F32)


def _dot_tn(a, b):
    return lax.dot_general(a, b, (((0,), (0,)), ((), ())), preferred_element_type=F32)


def _rms(x, g):
    return x * lax.rsqrt(jnp.mean(x * x, axis=-1, keepdims=True) + NORM_EPS) * g


def _sigmoid(x):
    return 1.0 / (1.0 + jnp.exp(-x))


def _split3(x):
    hi = x.astype(BF16)
    r = x - hi.astype(F32)
    mid = r.astype(BF16)
    lo = (r - mid.astype(F32)).astype(BF16)
    return hi, mid, lo


def _inproj_kernel(x_ref, g_ref, wbig_ref, wc_ref, wkr_ref, wg_ref, wgt_ref, bg_ref, bgt_ref,
                   cos_ref, sin_ref,
                   z_ref, c_ref, kr_ref, gate_ref, gatet_ref, xn_scr):
    j = pl.program_id(1)

    @pl.when(j == 0)
    def _():
        xn = _rms(x_ref[...], g_ref[...]).astype(BF16)
        xn_scr[...] = xn
        c_ref[...] = _dot(xn, wc_ref[...]).astype(BF16)
        kr = _dot(xn, wkr_ref[...])
        kr_ref[...] = (kr[:, :LANES] * cos_ref[...] + kr[:, LANES:] * sin_ref[...]).astype(BF16)
        gate_ref[...] = _dot(xn, wg_ref[...]) + bg_ref[...]
        gatet_ref[...] = _dot_nt(wgt_ref[...], xn) + bgt_ref[...]

    z_ref[...] = _dot(xn_scr[...], wbig_ref[...]).astype(BF16)


def _inproj(x, S, w):
    T = x.shape[0]
    tm = min(1024, S)
    tn = 1024
    nseq = S // tm
    row = lambda i, j: (i, 0)
    const = lambda i, j: (0, 0)
    return pl.pallas_call(
        _inproj_kernel,
        grid=(T // tm, BIG_COLS // tn),
        in_specs=[
            pl.BlockSpec((tm, D_MODEL), row),
            pl.BlockSpec((1, D_MODEL), const),
            pl.BlockSpec((D_MODEL, tn), lambda i, j: (0, j)),
            pl.BlockSpec((D_MODEL, 512), const),
            pl.BlockSpec((D_MODEL, 256), const),
            pl.BlockSpec((D_MODEL, 16), const),
            pl.BlockSpec((16, D_MODEL), const),
            pl.BlockSpec((1, 16), const),
            pl.BlockSpec((16, 1), const),
            pl.BlockSpec((tm, LANES), lambda i, j: (i % nseq, 0)),
            pl.BlockSpec((tm, LANES), lambda i, j: (i % nseq, 0)),
        ],
        out_specs=[
            pl.BlockSpec((tm, tn), lambda i, j: (i, j)),
            pl.BlockSpec((tm, 512), row),
            pl.BlockSpec((tm, LANES), row),
            pl.BlockSpec((tm, 16), row),
            pl.BlockSpec((16, tm), lambda i, j: (0, i)),
        ],
        out_shape=[
            jax.ShapeDtypeStruct((T, BIG_COLS), BF16),
            jax.ShapeDtypeStruct((T, 512), BF16),
            jax.ShapeDtypeStruct((T, LANES), BF16),
            jax.ShapeDtypeStruct((T, 16), F32),
            jax.ShapeDtypeStruct((16, T), F32),
        ],
        scratch_shapes=[pltpu.VMEM((tm, D_MODEL), BF16)],
        compiler_params=_cparams(("parallel", "arbitrary")),
        name="inproj",
    )(x, w["norm_mix_g"], w["w_big"], w["w_c"], w["w_kr"], w["w_g"], w["w_gt"], w["b_g"], w["b_gt"],
      w["cos"], w["sin"])


CONV_HALO = 16


def _conv_kernel(z_ref, zp_ref, zn_ref, w_ref, b_ref, o_ref, scr, *, tr, tiles_per_seq, scale, transpose):
    it = pl.program_id(0) % tiles_per_seq
    keep_prev = jnp.where(it == 0, 0.0, 1.0)
    keep_next = jnp.where(it == tiles_per_seq - 1, 0.0, 1.0)
    scr[0:8, :] = zp_ref[...].astype(F32)[8:16, :] * keep_prev
    scr[8:8 + tr, :] = z_ref[...].astype(F32)
    scr[8 + tr:16 + tr, :] = zn_ref[...].astype(F32)[0:8, :] * keep_next
    acc = jnp.zeros((tr, scr.shape[1]), F32) + b_ref[...]
    xe = scr[...]
    for k in range(M_CONV):
        d = k - M_CONV // 2
        win = xe if d == 0 else pltpu.roll(xe, (-d) % (tr + 16), axis=0)
        acc = acc + w_ref[k:k + 1, :] * win[8:8 + tr, :]
    y = acc * _sigmoid(acc) * scale
    o_ref[...] = (y.T if transpose else y).astype(BF16)


def _conv(z, S, w, *, col0, scale, transpose):
    T = z.shape[0]
    tr = min(512, S)
    tcw = 512
    tiles_per_seq = S // tr
    hb = tr // CONV_HALO
    nhalo = T // CONV_HALO
    c0 = col0 * (M_WIDTH // tcw)
    kern = functools.partial(_conv_kernel, tr=tr, tiles_per_seq=tiles_per_seq, scale=scale, transpose=transpose)
    if transpose:
        out_spec = pl.BlockSpec((tcw, tr), lambda i, j: (j, i))
        out_shape = jax.ShapeDtypeStruct((M_WIDTH, T), BF16)
    else:
        out_spec = pl.BlockSpec((tr, tcw), lambda i, j: (i, j))
        out_shape = jax.ShapeDtypeStruct((T, M_WIDTH), BF16)
    return pl.pallas_call(
        kern,
        grid=(T // tr, M_WIDTH // tcw),
        in_specs=[
            pl.BlockSpec((tr, tcw), lambda i, j: (i, c0 + j)),
            pl.BlockSpec((CONV_HALO, tcw), lambda i, j: (jnp.maximum(i * hb - 1, 0), c0 + j)),
            pl.BlockSpec((CONV_HALO, tcw), lambda i, j: (jnp.minimum((i + 1) * hb, nhalo - 1), c0 + j)),
            pl.BlockSpec((M_CONV, tcw), lambda i, j: (0, c0 + j)),
            pl.BlockSpec((1, tcw), lambda i, j: (0, c0 + j)),
        ],
        out_specs=out_spec,
        out_shape=out_shape,
        scratch_shapes=[pltpu.VMEM((tr + 16, tcw), F32)],
        compiler_params=_cparams(("parallel", "parallel")),
        name="conv_silu_t" if transpose else "conv_silu",
    )(z, z, z, w["conv_w"], w["conv_b"])


def _log_sigmoid(x):
    return -(jnp.maximum(-x, 0.0) + jnp.log1p(jnp.exp(-jnp.abs(x))))


M_STATE_COLS = M_HEAD_DIM + LANES


def _mlstm_gates(d, g_ref, gt_ref):
    L = M_CHUNK
    r = lax.broadcasted_iota(I32, (L, L), 0)
    c = lax.broadcasted_iota(I32, (L, L), 1)
    if d == 0:
        mask = c <= r
    else:
        mask = c >= r
    tri_col = jnp.where(mask, 1.0, 0.0).astype(BF16)
    tri_row = jnp.where(r <= c if d == 0 else r >= c, 1.0, 0.0).astype(BF16)
    g = g_ref[...]
    gt = gt_ref[...]
    b_row_all = sum(_dot(p, tri_row) for p in _split3(_log_sigmoid(gt)))
    sel_r = lax.broadcasted_iota(I32, (4 * M_HEADS, M_HEADS * LANES), 0)
    sel_h = lax.broadcasted_iota(I32, (4 * M_HEADS, M_HEADS * LANES), 1) // LANES
    pick_i = jnp.where(sel_r == d * 2 * M_HEADS + sel_h, 1.0, 0.0).astype(BF16)
    pick_f = jnp.where(sel_r == d * 2 * M_HEADS + M_HEADS + sel_h, 1.0, 0.0).astype(BF16)
    i_bc_all = sum(_dot(p, pick_i) for p in _split3(g))
    b_bc_all = sum(_dot(tri_col, _dot(p, pick_f).astype(BF16)) for p in _split3(_log_sigmoid(g)))
    return mask, i_bc_all, b_bc_all, gt, b_row_all


def _mlstm_kernel(qf, ktf, vf, gf, gtf, qb, ktb, vb, gb, gtb, of, ob, c_scr, m_scr):
    @pl.when(pl.program_id(1) == 0)
    def _():
        c_scr[...] = jnp.zeros(c_scr.shape, F32)
        m_scr[...] = jnp.zeros(m_scr.shape, F32)

    L = M_CHUNK
    gates = (_mlstm_gates(0, gf, gtf), _mlstm_gates(1, gb, gtb))
    refs = ((qf, ktf, vf, of), (qb, ktb, vb, ob))
    units = [(d, h) for d in range(2) for h in range(M_HEADS)]
    sls = [slice(h * M_HEAD_DIM, (h + 1) * M_HEAD_DIM) for _, h in units]
    ones_blk = jnp.ones((L, LANES), BF16)

    qs = [refs[d][0][:, sl] for (d, _), sl in zip(units, sls)]
    kts = [refs[d][1][sl, :] for (d, _), sl in zip(units, sls)]
    vs = [refs[d][2][:, sl] for (d, _), sl in zip(units, sls)]
    csts = [c_scr[u] for u in range(len(units))]
    qk = [_dot(q, kt) for q, kt in zip(qs, kts)]
    qc = [_dot(q, cst.astype(BF16)) for q, cst in zip(qs, csts)]
    i_bc = [gates[d][1][:, h * LANES:(h + 1) * LANES] for d, h in units]
    b_bc = [gates[d][2][:, h * LANES:(h + 1) * LANES] for d, h in units]
    i_row = [gates[d][3][d * 2 * M_HEADS + h:d * 2 * M_HEADS + h + 1, :] for d, h in units]
    b_row = [gates[d][4][d * 2 * M_HEADS + M_HEADS + h:d * 2 * M_HEADS + M_HEADS + h + 1, :] for d, h in units]
    b_last = [b[(L - 1 if d == 0 else 0):(L if d == 0 else 1), :] for (d, _), b in zip(units, b_bc)]
    m_prev = [m_scr[u][0:1, :] for u in range(len(units))]
    gk = [bl - b + i for bl, b, i in zip(b_last, b_bc, i_bc)]
    m_new = [jnp.maximum(bl + mp, jnp.max(x, axis=0, keepdims=True)) for bl, mp, x in zip(b_last, m_prev, gk)]
    decay = [jnp.exp(bl + mp - mn) for bl, mp, mn in zip(b_last, m_prev, m_new)]
    wk = [jnp.exp(x - mn) for x, mn in zip(gk, m_new)]
    wv = [jnp.concatenate([(jnp.concatenate([x, x], axis=1) * v.astype(F32)).astype(BF16), x.astype(BF16)], axis=1)
          for x, v in zip(wk, vs)]
    upd = [_dot(kt, x) for kt, x in zip(kts, wv)]
    a = [b + mp for b, mp in zip(b_bc, m_prev)]
    dm = [jnp.where(gates[d][0], b - br + ir, -jnp.inf) for (d, _), b, br, ir in zip(units, b_bc, b_row, i_row)]
    m_t = [jnp.maximum(x, jnp.max(y, axis=1, keepdims=True)) for x, y in zip(a, dm)]
    w_inter = [jnp.exp(x - mt) for x, mt in zip(a, m_t)]
    s = [x * jnp.exp(y - mt) for x, y, mt in zip(qk, dm, m_t)]
    sv = [_dot(x.astype(BF16), jnp.concatenate([v, ones_blk], axis=1)) for x, v in zip(s, vs)]
    for u, (d, _) in enumerate(units):
        ne = sv[u] + jnp.concatenate([w_inter[u]] * 3, axis=1) * qc[u]
        inv = 1.0 / jnp.maximum(jnp.abs(ne[:, M_HEAD_DIM:]), jnp.exp(-m_t[u]))
        refs[d][3][:, sls[u]] = ne[:, :M_HEAD_DIM] * jnp.concatenate([inv, inv], axis=1)
        c_scr[u] = jnp.concatenate([decay[u]] * 3, axis=1) * csts[u] + upd[u]
        m_scr[u] = jnp.broadcast_to(m_new[u], (8, LANES))


def _mlstm(q, kt, z, gate, gatet, B, S):
    T = B * S
    L = M_CHUNK
    nc = S // L
    fwd = lambda b, c: b * nc + c
    bwd = lambda b, c: b * nc + nc - 1 - c

    def specs(pos):
        return [
            pl.BlockSpec((L, M_WIDTH), lambda b, c: (pos(b, c), 0)),
            pl.BlockSpec((M_WIDTH, L), lambda b, c: (0, pos(b, c))),
            pl.BlockSpec((L, M_WIDTH), lambda b, c: (pos(b, c), 2)),
            pl.BlockSpec((L, 16), lambda b, c: (pos(b, c), 0)),
            pl.BlockSpec((16, L), lambda b, c: (0, pos(b, c))),
        ]

    nstate = 2 * M_HEADS
    return pl.pallas_call(
        _mlstm_kernel,
        grid=(B, nc),
        in_specs=specs(fwd) + specs(bwd),
        out_specs=[
            pl.BlockSpec((L, M_WIDTH), lambda b, c: (fwd(b, c), 0)),
            pl.BlockSpec((L, M_WIDTH), lambda b, c: (bwd(b, c), 0)),
        ],
        out_shape=[jax.ShapeDtypeStruct((T, M_WIDTH), F32)] * 2,
        scratch_shapes=[
            pltpu.VMEM((nstate, M_HEAD_DIM, M_STATE_COLS), F32),
            pltpu.VMEM((nstate, 8, LANES), F32),
        ],
        compiler_params=_cparams(("parallel", "arbitrary")),
        name="mlstm",
    )(q, kt, z, gate, gatet, q, kt, z, gate, gatet)


ATT_TILE = 512


def _flash_tiles(S):
    tk = min(ATT_TILE, S)
    nk = S // tk
    tq = min(1024 if nk <= 8 else 512, S)
    unroll = 4 * ATT_SLOTS if nk >= 16 else 2 * ATT_SLOTS
    return tq, tk, unroll


ATT_SLOTS = 2
ATT_LOGIT_SCALE = (A_NOPE + A_ROPE) ** -0.5 * 1.4426950408889634


def _mla_proj_kernel(c_ref, kr_ref, cost_ref, sint_ref, gq_ref, gkv_ref, wqat_ref, wqrt_ref, wk_ref, wvt_ref,
                     qt_ref, k_ref, vt_ref):
    cq = c_ref[:, :A_Q_RANK].astype(F32)
    ckv = c_ref[:, A_Q_RANK:].astype(F32)
    cqn = _rms(cq, gq_ref[...]).astype(BF16)
    ckvn = _rms(ckv, gkv_ref[...]).astype(BF16)
    qat = _dot_nt(wqat_ref[...], cqn)
    qrt = _dot_nt(wqrt_ref[...], cqn)
    kn = _dot(ckvn, wk_ref[...])
    vt = _dot_nt(wvt_ref[...], ckvn)
    cost = cost_ref[...]
    sint = sint_ref[...]
    kr = kr_ref[...]
    for h in range(A_HEADS):
        o = h * A_QK_PAD
        qt_ref[0, h, :LANES, :] = (qat[o:o + LANES, :] * ATT_LOGIT_SCALE).astype(BF16)
        qt_ref[0, h, LANES:, :] = ((qat[o + LANES:o + 2 * LANES, :] * cost
                                    + qrt[h * LANES:(h + 1) * LANES, :] * sint) * ATT_LOGIT_SCALE).astype(BF16)
        k_ref[0, h, :, :LANES] = kn[:, h * A_NOPE:(h + 1) * A_NOPE].astype(BF16)
        k_ref[0, h, :, LANES:] = kr
        vt_ref[0, h, 0] = vt[h * A_V:(h + 1) * A_V, :].astype(BF16)


def _mla_proj(c, kr, B, S, w):
    tm = min(ATT_TILE, S)
    nseq = S // tm
    row = lambda i: (i, 0)
    const = lambda i: (0, 0)
    seq_t = lambda i: (0, i % nseq)
    return pl.pallas_call(
        _mla_proj_kernel,
        grid=(B * S // tm,),
        in_specs=[
            pl.BlockSpec((tm, 512), row),
            pl.BlockSpec((tm, LANES), row),
            pl.BlockSpec((LANES, tm), seq_t),
            pl.BlockSpec((LANES, tm), seq_t),
            pl.BlockSpec((1, A_Q_RANK), const),
            pl.BlockSpec((1, A_KV_RANK), const),
            pl.BlockSpec((A_HEADS * A_QK_PAD, A_Q_RANK), const),
            pl.BlockSpec((A_HEADS * LANES, A_Q_RANK), const),
            pl.BlockSpec((A_KV_RANK, A_HEADS * A_NOPE), const),
            pl.BlockSpec((A_HEADS * A_V, A_KV_RANK), const),
        ],
        out_specs=[
            pl.BlockSpec((1, A_HEADS, A_QK_PAD, tm), lambda i: (i // nseq, 0, 0, i % nseq)),
            pl.BlockSpec((1, A_HEADS, tm, A_QK_PAD), lambda i: (i // nseq, 0, i % nseq, 0)),
            pl.BlockSpec((1, A_HEADS, 1, A_V, tm), lambda i: (i // nseq, 0, i % nseq, 0, 0)),
        ],
        out_shape=[
            jax.ShapeDtypeStruct((B, A_HEADS, A_QK_PAD, S), BF16),
            jax.ShapeDtypeStruct((B, A_HEADS, S, A_QK_PAD), BF16),
            jax.ShapeDtypeStruct((B, A_HEADS, nseq, A_V, tm), BF16),
        ],
        compiler_params=_cparams(("parallel",)),
        name="mla_proj",
    )(c, kr, w["cos_t"], w["sin_t"], w["g_cq"], w["g_ckv"], w["w_qat"], w["w_qrt"], w["w_uk"], w["w_uvt"])


def _flash_kernel(q_ref, k_ref, vt_ref, o_ref, s_scr, *, tk, nk, unroll):
    qt = q_ref[0, 0]
    tq = qt.shape[1]

    def scores(j, slot):
        start = pl.multiple_of(j * tk, tk)
        s_scr[slot] = _dot(k_ref[0, 0, pl.ds(start, tk), :], qt)

    def accumulate(j, slot, carry):
        m, l, acc = carry
        s = s_scr[slot]
        m_new = jnp.maximum(m, jnp.max(s, axis=0, keepdims=True))
        p = jnp.exp2(s - m_new)
        alpha = jnp.exp2(m - m_new)
        l = alpha * l + jnp.sum(p, axis=0, keepdims=True)
        acc = alpha * acc + _dot(vt_ref[0, 0, j], p.astype(BF16))
        return m_new, l, acc

    nslot = s_scr.shape[0]
    ahead = nslot - 1

    def body(jj, carry):
        for t in range(unroll):
            j = unroll * jj + t
            scores(jnp.minimum(j + ahead, nk - 1), (t + ahead) % nslot)
            carry = accumulate(j, t % nslot, carry)
        return carry

    carry = (jnp.full((1, tq), -jnp.inf, F32), jnp.zeros((1, tq), F32), jnp.zeros((A_V, tq), F32))
    for j in range(min(ahead, nk)):
        scores(j, j % nslot)
    n_loop = nk // unroll
    if n_loop > 0:
        carry = lax.fori_loop(0, n_loop, body, carry)
    for j in range(unroll * n_loop, nk):
        if j + ahead < nk:
            scores(j + ahead, (j + ahead) % nslot)
        carry = accumulate(j, j % nslot, carry)
    _, l, acc = carry
    o_ref[...] = (acc / l).T.astype(BF16)


def _flash(qt, k, vt):
    B, H, S, _ = k.shape
    tq, tk, unroll = _flash_tiles(S)
    nq = S // tq
    nk = S // tk
    kern = functools.partial(_flash_kernel, tk=tk, nk=nk, unroll=unroll)
    return pl.pallas_call(
        kern,
        grid=(B, H, nq),
        in_specs=[
            pl.BlockSpec((1, 1, A_QK_PAD, tq), lambda b, h, i: (b, h, 0, i)),
            pl.BlockSpec((1, 1, S, A_QK_PAD), lambda b, h, i: (b, h, 0, 0)),
            pl.BlockSpec((1, 1, nk, A_V, tk), lambda b, h, i: (b, h, 0, 0, 0)),
        ],
        out_specs=pl.BlockSpec((tq, A_V), lambda b, h, i: (b * nq + i, h)),
        out_shape=jax.ShapeDtypeStruct((B * S, A_HEADS * A_V), BF16),
        scratch_shapes=[pltpu.VMEM((ATT_SLOTS, tk, tq), F32)],
        compiler_params=_cparams(("parallel", "parallel", "arbitrary")),
        name="mla_flash",
    )(qt, k, vt)


def _mixer_out_kernel(hf_ref, hb_ref, om_ref, ga_ref, gb_ref, att_ref, x_ref, mg_ref,
                      wa_ref, wb_ref, wo_ref, o_ref):
    hs = hf_ref[...] + hb_ref[...]
    mg = mg_ref[...]
    parts = []
    for h in range(M_HEADS):
        sl = slice(h * M_HEAD_DIM, (h + 1) * M_HEAD_DIM)
        parts.append(_rms(hs[:, sl], mg[:, sl]))
    hn = jnp.concatenate(parts, axis=1) * _sigmoid(om_ref[...].astype(F32))
    y_a = _dot(hn.astype(BF16), wa_ref[...])
    y_b = _dot(att_ref[...], wb_ref[...])
    merged = _sigmoid(ga_ref[...].astype(F32)) * y_a + _sigmoid(gb_ref[...].astype(F32)) * y_b
    o_ref[...] = x_ref[...] + _dot(merged.astype(BF16), wo_ref[...])


def _mixer_out(hf, hb, z, att, x, w):
    T = x.shape[0]
    tm = min(512, T)
    row = lambda i: (i, 0)
    const = lambda i: (0, 0)
    wspec = pl.BlockSpec((D_MODEL, D_MODEL), const)
    return pl.pallas_call(
        _mixer_out_kernel,
        grid=(T // tm,),
        in_specs=[
            pl.BlockSpec((tm, M_WIDTH), row),
            pl.BlockSpec((tm, M_WIDTH), row),
            pl.BlockSpec((tm, M_WIDTH), lambda i: (i, 3)),
            pl.BlockSpec((tm, D_MODEL), lambda i: (i, 4)),
            pl.BlockSpec((tm, D_MODEL), lambda i: (i, 5)),
            pl.BlockSpec((tm, D_MODEL), row),
            pl.BlockSpec((tm, D_MODEL), row),
            pl.BlockSpec((1, M_WIDTH), const),
            wspec, wspec, wspec,
        ],
        out_specs=pl.BlockSpec((tm, D_MODEL), row),
        out_shape=jax.ShapeDtypeStruct((T, D_MODEL), F32),
        compiler_params=_cparams(("parallel",)),
        name="mixer_out",
    )(hf, hb, z, z, z, att, x, w["mh_norm_g"], w["w_br_a"], w["w_br_b"], w["w_out"])


def _mem_kernel(m_ref, g_ref, wk_ref, wv_ref, k_ref, v_ref):
    mn = _rms(m_ref[...], g_ref[...]).astype(BF16)
    k_ref[...] = _dot(mn, wk_ref[...]).astype(BF16)
    v_ref[...] = _dot(mn, wv_ref[...]).astype(BF16)


def _mem_proj(mem, w):
    R = mem.shape[0]
    tm = 256
    row = lambda i: (i, 0)
    const = lambda i: (0, 0)
    wspec = pl.BlockSpec((D_MODEL, D_MODEL), const)
    return pl.pallas_call(
        _mem_kernel,
        grid=(R // tm,),
        in_specs=[pl.BlockSpec((tm, D_MODEL), row), pl.BlockSpec((1, D_MODEL), const), wspec, wspec],
        out_specs=[pl.BlockSpec((tm, D_MODEL), row)] * 2,
        out_shape=[jax.ShapeDtypeStruct((R, D_MODEL), BF16)] * 2,
        compiler_params=_cparams(("parallel",)),
        name="mem_proj",
    )(mem, w["norm_mem_g"], w["w_xk"], w["w_xv"])


def _cross_router_kernel(x_ref, kx_ref, vx_ref, gx_ref, gf_ref, wq_ref, wo_ref, wr_ref, wrt_ref,
                         br_ref, brt_ref, x2_ref, xn_ref, aff_ref, afft_ref):
    x1 = x_ref[...]
    xn = _rms(x1, gx_ref[...]).astype(BF16)
    q = _dot(xn, wq_ref[...])
    sls = [slice(h * X_HEAD_DIM, (h + 1) * X_HEAD_DIM) for h in range(X_HEADS)]
    ss = [_dot_nt(q[:, sl].astype(BF16), kx_ref[:, sl]) * (X_HEAD_DIM ** -0.5) for sl in sls]
    es = [jnp.exp(s - jnp.max(s, axis=1, keepdims=True)) for s in ss]
    ps = [e / jnp.sum(e, axis=1, keepdims=True) for e in es]
    o = jnp.concatenate([_dot(p.astype(BF16), vx_ref[:, sl]) for p, sl in zip(ps, sls)], axis=1)
    x2 = x1 + _dot(o.astype(BF16), wo_ref[...])
    x2_ref[...] = x2
    xf = _rms(x2, gf_ref[...])
    hi = xf.astype(BF16)
    xn_ref[...] = hi
    lo = (xf - hi.astype(F32)).astype(BF16)
    wr = wr_ref[...]
    wr_hi = wr.astype(BF16)
    wr_lo = (wr - wr_hi.astype(F32)).astype(BF16)
    logit = _dot(hi, wr_hi) + _dot(lo, wr_hi) + _dot(hi, wr_lo) + br_ref[...]
    e = jnp.exp(logit - jnp.max(logit, axis=1, keepdims=True))
    aff_ref[...] = e / jnp.sum(e, axis=1, keepdims=True)
    wrt = wrt_ref[...]
    wrt_hi = wrt.astype(BF16)
    wrt_lo = (wrt - wrt_hi.astype(F32)).astype(BF16)
    logit_t = _dot_nt(wrt_hi, hi) + _dot_nt(wrt_hi, lo) + _dot_nt(wrt_lo, hi) + brt_ref[...]
    et = jnp.exp(logit_t - jnp.max(logit_t, axis=0, keepdims=True))
    afft_ref[...] = et / jnp.sum(et, axis=0, keepdims=True)


def _cross_router(x1, kx, vx, S, n_mem, w):
    T = x1.shape[0]
    tm = min(512, S)
    nseq = S // tm
    row = lambda i: (i, 0)
    const = lambda i: (0, 0)
    wspec = pl.BlockSpec((D_MODEL, D_MODEL), const)
    memspec = pl.BlockSpec((n_mem, D_MODEL), lambda i: (i // nseq, 0))
    return pl.pallas_call(
        _cross_router_kernel,
        grid=(T // tm,),
        in_specs=[
            pl.BlockSpec((tm, D_MODEL), row), memspec, memspec,
            pl.BlockSpec((1, D_MODEL), const), pl.BlockSpec((1, D_MODEL), const),
            wspec, wspec,
            pl.BlockSpec((D_MODEL, N_EXPERTS), const), pl.BlockSpec((N_EXPERTS, D_MODEL), const),
            pl.BlockSpec((1, N_EXPERTS), const), pl.BlockSpec((N_EXPERTS, 1), const),
        ],
        out_specs=[
            pl.BlockSpec((tm, D_MODEL), row),
            pl.BlockSpec((tm, D_MODEL), row),
            pl.BlockSpec((tm, N_EXPERTS), row),
            pl.BlockSpec((N_EXPERTS, tm), lambda i: (0, i)),
        ],
        out_shape=[
            jax.ShapeDtypeStruct((T, D_MODEL), F32),
            jax.ShapeDtypeStruct((T, D_MODEL), BF16),
            jax.ShapeDtypeStruct((T, N_EXPERTS), F32),
            jax.ShapeDtypeStruct((N_EXPERTS, T), F32),
        ],
        compiler_params=_cparams(("parallel",)),
        name="cross_router",
    )(x1, kx, vx, w["norm_x_g"], w["norm_ffn_g"], w["w_xq"], w["w_xo"], w["w_router"], w["w_router_t"],
      w["b_router"], w["b_router_t"])


def _excl_cumsum(mask_f, strict_lane, strict_blk):
    nb = mask_f.shape[0]
    within = _dot(mask_f.astype(BF16), strict_lane)
    tot = jnp.sum(mask_f, axis=1, keepdims=True)
    bstart = _dot(strict_blk, jnp.broadcast_to(tot, (nb, LANES)).astype(BF16))
    return within + bstart, bstart


def _select_kernel(aff_ref, pos_ref, bst_ref, *, cap):
    a = aff_ref[0]
    nb = a.shape[0]
    bits = pltpu.bitcast(a, I32)

    def radix(i, prefix):
        cand = prefix | jnp.left_shift(jnp.int32(1), 30 - i)
        cnt = jnp.sum(jnp.where(bits >= cand, 1.0, 0.0), axis=(0, 1), keepdims=True)
        return jnp.where(cnt >= cap, cand, prefix)

    thr = lax.fori_loop(0, 31, radix, jnp.zeros((1, 1), I32))
    gt = bits > thr
    eq = bits == thr
    need = cap - jnp.sum(jnp.where(gt, 1.0, 0.0), axis=(0, 1), keepdims=True)
    li = lax.broadcasted_iota(I32, (LANES, LANES), 0)
    lj = lax.broadcasted_iota(I32, (LANES, LANES), 1)
    strict_lane = jnp.where(li < lj, 1.0, 0.0).astype(BF16)
    bi = lax.broadcasted_iota(I32, (nb, nb), 0)
    bj = lax.broadcasted_iota(I32, (nb, nb), 1)
    strict_blk = jnp.where(bj < bi, 1.0, 0.0).astype(BF16)
    rank, _ = _excl_cumsum(jnp.where(eq, 1.0, 0.0), strict_lane, strict_blk)
    sel = gt | (eq & (rank < need))
    pos, bstart = _excl_cumsum(jnp.where(sel, 1.0, 0.0), strict_lane, strict_blk)
    pos_ref[0] = jnp.where(sel, pos.astype(I32), -1)
    col = jnp.broadcast_to(bstart[:, 0:1], (nb, nb))
    bst_ref[0] = jnp.sum(jnp.where(bi == bj, col, 0.0), axis=0, keepdims=True).astype(I32)


def _select(aff_t, cap):
    E, T = aff_t.shape
    nb = T // LANES
    kern = functools.partial(_select_kernel, cap=cap)
    return pl.pallas_call(
        kern,
        grid=(E,),
        in_specs=[pl.BlockSpec((1, nb, LANES), lambda e: (e, 0, 0))],
        out_specs=[
            pl.BlockSpec((1, nb, LANES), lambda e: (e, 0, 0)),
            pl.BlockSpec((1, 1, nb), lambda e: (e, 0, 0)),
        ],
        out_shape=[
            jax.ShapeDtypeStruct((E, nb, LANES), I32),
            jax.ShapeDtypeStruct((E, 1, nb), I32),
        ],
        compiler_params=_cparams(("parallel",)),
        name="ec_select",
    )(aff_t.reshape(E, nb, LANES))


def _cells(bst, cap, T, tu, tc):
    E = bst.shape[0]
    ns = cap // tc
    start = bst[:, ::tu // LANES]
    end = jnp.concatenate([start[:, 1:], jnp.full((E, 1), cap, I32)], axis=1)
    cnt = end - start
    s_lo = jnp.minimum(start // tc, ns - 1)
    s_hi = jnp.where(cnt > 0, (end - 1) // tc, s_lo)
    return jnp.where(cnt > 0, s_hi - s_lo + 1, 0), s_lo


def _enumerate_cells(nc_flat, slo_flat, steps):
    off_end = jnp.cumsum(nc_flat, axis=-1)
    off = off_end - nc_flat
    total = off_end[..., -1:]
    k = jnp.arange(steps, dtype=I32)
    kk = jnp.minimum(k, total - 1)
    grp = jnp.sum((off_end[..., None, :] <= kk[..., :, None]).astype(I32), axis=-1)
    s = jnp.take_along_axis(slo_flat, grp, axis=-1) + kk - jnp.take_along_axis(off, grp, axis=-1)
    return grp, s, (k < total).astype(I32)


def _gather_steps(T, cap, tu, tc):
    return T // tu + MOE_GATHER_GROUP * (cap // tc)


def _gather_tables(bst, cap, T, tu, tc):
    E = bst.shape[0]
    G = MOE_GATHER_GROUP
    nu = T // tu
    steps = _gather_steps(T, cap, tu, tc)
    ncell, s_lo = _cells(bst, cap, T, tu, tc)
    own = nu + cap // tc
    _, o_s, o_valid = _enumerate_cells(ncell, s_lo, own)
    prev_s = jnp.concatenate([jnp.full((E, 1), -1, I32), o_s[:, :-1]], axis=1)
    o_first = o_valid * (o_s != prev_s).astype(I32)
    off = jnp.cumsum(ncell, axis=1) - ncell
    n_u = jnp.max(ncell.reshape(E // G, G, nu), axis=1)
    t_u, t_k, t_valid = _enumerate_cells(n_u, jnp.zeros_like(n_u), steps)
    u_e = jnp.repeat(t_u, G, axis=0)
    k_e = jnp.repeat(t_k, G, axis=0)
    nc_e = jnp.take_along_axis(ncell, u_e, axis=1)
    off_e = jnp.take_along_axis(off, u_e, axis=1)
    has = (k_e < nc_e).astype(I32) * jnp.repeat(t_valid, G, axis=0)
    cid = jnp.clip(off_e + jnp.minimum(k_e, nc_e - 1), 0, own - 1)
    s_e = jnp.take_along_axis(o_s, cid, axis=1)
    first_e = has * jnp.take_along_axis(o_first, cid, axis=1)
    per_expert = lambda a: a.reshape(E // G, G, steps).transpose(0, 2, 1).reshape(-1)
    return t_u.reshape(-1), per_expert(s_e), per_expert(has), per_expert(first_e)


def _combine_tables(bst, cap, T, tu, tc):
    E = bst.shape[0]
    nu = T // tu
    ns = cap // tc
    G = MOE_COMBINE_GROUP
    ncell, s_lo = _cells(bst, cap, T, tu, tc)
    ncell_c = ncell.at[0].set(jnp.maximum(ncell[0], 1))
    pc = E * (nu + ns)
    c_grp, c_s, _ = _enumerate_cells(ncell_c.T.reshape(-1), s_lo.T.reshape(-1), pc)
    c_e = c_grp % E
    cells_u = jnp.sum(ncell_c, axis=0)
    cell_off = jnp.cumsum(cells_u) - cells_u
    groups_u = (cells_u + G - 1) // G
    grp_end = jnp.cumsum(groups_u)
    n_steps = pc // G + nu
    k = jnp.arange(n_steps, dtype=I32)
    kk = jnp.minimum(k, grp_end[-1] - 1)
    t_u = jnp.sum((grp_end[None, :] <= kk[:, None]).astype(I32), axis=1)
    j = kk - (grp_end - groups_u)[t_u]
    t_valid = (k < grp_end[-1]).astype(I32)
    local = j[:, None] * G + jnp.arange(G, dtype=I32)[None, :]
    cell_ok = (local < cells_u[t_u][:, None]).astype(I32) * t_valid[:, None]
    cid = jnp.minimum(cell_off[t_u][:, None] + jnp.minimum(local, cells_u[t_u][:, None] - 1), pc - 1)
    t_first = t_valid * (j == 0).astype(I32)
    t_last = t_valid * (j == groups_u[t_u] - 1).astype(I32)
    return (t_u, c_e[cid].reshape(-1), c_s[cid].reshape(-1), cell_ok.reshape(-1), t_valid, t_first, t_last)


def _gather_kernel(u_tab, s_tab, has_tab, first_tab, pos_ref, x_ref, *o_refs, steps, tc):
    G = MOE_GATHER_GROUP
    base = (pl.program_id(0) * steps + pl.program_id(1)) * G
    tu = pos_ref.shape[-1]
    for g, o_ref in enumerate(o_refs):
        slot = lax.broadcasted_iota(I32, (tc, tu), 0) + s_tab[base + g] * tc
        onehot = jnp.where(pos_ref[g] == slot, 1.0, 0.0).astype(BF16)

        @pl.when(first_tab[base + g] == 1)
        def _():
            o_ref[0] = _dot(onehot, x_ref[...]).astype(BF16)

        @pl.when((has_tab[base + g] == 1) & (first_tab[base + g] == 0))
        def _():
            o_ref[0] = o_ref[0] + _dot(onehot, x_ref[...]).astype(BF16)


def _moe_gather(xn, pos_row, tabs, cap, tu, tc):
    T = xn.shape[0]
    E = N_EXPERTS
    G = MOE_GATHER_GROUP
    steps = _gather_steps(T, cap, tu, tc)
    kern = functools.partial(_gather_kernel, steps=steps, tc=tc)

    def out_spec(g):
        return pl.BlockSpec((1, tc, D_MODEL), lambda p, k, u, s, *_: (p, s[(p * steps + k) * G + g], 0))

    grid_spec = pltpu.PrefetchScalarGridSpec(
        num_scalar_prefetch=4,
        grid=(E // G, steps),
        in_specs=[
            pl.BlockSpec((G, 1, tu), lambda p, k, u, *_: (p, 0, u[p * steps + k])),
            pl.BlockSpec((tu, D_MODEL), lambda p, k, u, *_: (u[p * steps + k], 0)),
        ],
        out_specs=[out_spec(g) for g in range(G)],
    )
    return pl.pallas_call(
        kern,
        grid_spec=grid_spec,
        out_shape=[jax.ShapeDtypeStruct((E // G, cap, D_MODEL), BF16)] * G,
        compiler_params=_cparams(("parallel", "arbitrary")),
        name="moe_gather",
    )(*tabs, pos_row, xn)


def _ffn_kernel(*refs):
    G = MOE_GATHER_GROUP
    x_refs = refs[:G]
    wg_ref, wu_ref, wd_ref, o_ref, wg_b, wu_b, wd_b = refs[G:]
    e = pl.program_id(0)

    @pl.when(pl.program_id(1) == 0)
    def _():
        wg_b[...] = wg_ref[0].astype(BF16)
        wu_b[...] = wu_ref[0].astype(BF16)
        wd_b[...] = wd_ref[0].astype(BF16)

    x = x_refs[0][0]
    for g in range(1, G):
        x = jnp.where(e % G == g, x_refs[g][0], x)
    gate = _dot(x, wg_b[...])
    up = _dot(x, wu_b[...])
    h = (gate * _sigmoid(gate) * up).astype(BF16)
    o_ref[0] = _dot(h, wd_b[...]).astype(BF16)


def _moe_ffn(xes, w):
    G = MOE_GATHER_GROUP
    _, cap, _ = xes[0].shape
    E = N_EXPERTS
    tf = min(512, cap)
    wspec = pl.BlockSpec((1, D_MODEL, E_FF), lambda e, i: (e, 0, 0))

    def x_spec(g):
        return pl.BlockSpec((1, tf, D_MODEL), lambda e, i: (e // G, jnp.where(e % G == g, i, 0), 0))

    return pl.pallas_call(
        _ffn_kernel,
        grid=(E, cap // tf),
        in_specs=[x_spec(g) for g in range(G)]
        + [wspec, wspec, pl.BlockSpec((1, E_FF, D_MODEL), lambda e, i: (e, 0, 0))],
        out_specs=pl.BlockSpec((1, tf, D_MODEL), lambda e, i: (e, i, 0)),
        out_shape=jax.ShapeDtypeStruct((E, cap, D_MODEL), BF16),
        scratch_shapes=[pltpu.VMEM((D_MODEL, E_FF), BF16), pltpu.VMEM((D_MODEL, E_FF), BF16),
                        pltpu.VMEM((E_FF, D_MODEL), BF16)],
        compiler_params=_cparams(("parallel", "arbitrary")),
        name="moe_ffn",
    )(*xes, w["w_e_gate"], w["w_e_up"], w["w_e_down"])


def _combine_kernel(u_tab, e_tab, s_tab, ok_tab, valid_tab, first_tab, last_tab, pos_ref, aff_ref, *rest, tc):
    G = MOE_COMBINE_GROUP
    ye_refs = rest[:G]
    x_ref, g_ref, o_ref, acc = rest[G:]
    step = pl.program_id(0)
    tu = pos_ref.shape[0]

    @pl.when(first_tab[step] == 1)
    def _():
        acc[...] = jnp.zeros(acc.shape, F32)

    @pl.when(valid_tab[step] == 1)
    def _():
        pos = pos_ref[...].astype(F32)
        aff = aff_ref[...]
        lane_e = lax.broadcasted_iota(I32, pos.shape, 1)
        lane_r = lax.broadcasted_iota(I32, (tu, tc), 1).astype(F32)
        hits = []
        for g in range(G):
            c = step * G + g
            mine = lane_e == e_tab[c]
            slot = jnp.sum(jnp.where(mine, pos, 0.0), axis=1, keepdims=True)
            gate = jnp.sum(jnp.where(mine, aff, 0.0), axis=1, keepdims=True)
            base = jnp.where(ok_tab[c] == 1, s_tab[c] * tc, -2 * tc).astype(F32)
            hits.append(jnp.where(slot - base == lane_r, gate, 0.0).astype(BF16))
        ye = jnp.concatenate([r[0] for r in ye_refs], axis=0)
        acc[...] = acc[...] + _dot(jnp.concatenate(hits, axis=1), ye)

    @pl.when(last_tab[step] == 1)
    def _():
        o_ref[...] = _rms(x_ref[...] + acc[...], g_ref[...])


def _moe_combine(ye, pos_col, aff, x2, tabs, final_g, cap, tu, tc):
    T = x2.shape[0]
    E = N_EXPERTS
    G = MOE_COMBINE_GROUP
    steps = E * (T // tu + cap // tc) // G + T // tu
    kern = functools.partial(_combine_kernel, tc=tc)

    def ye_spec(g):
        return pl.BlockSpec((1, tc, D_MODEL), lambda k, u, e, s, *_: (e[k * G + g], s[k * G + g], 0))

    tile = pl.BlockSpec((tu, D_MODEL), lambda k, u, *_: (u[k], 0))
    per_expert = pl.BlockSpec((tu, N_EXPERTS), lambda k, u, *_: (u[k], 0))
    grid_spec = pltpu.PrefetchScalarGridSpec(
        num_scalar_prefetch=7,
        grid=(steps,),
        in_specs=[per_expert, per_expert]
        + [ye_spec(g) for g in range(G)]
        + [tile, pl.BlockSpec((1, D_MODEL), lambda k, *_: (0, 0))],
        out_specs=tile,
        scratch_shapes=[pltpu.VMEM((tu, D_MODEL), F32)],
    )
    return pl.pallas_call(
        kern,
        grid_spec=grid_spec,
        out_shape=jax.ShapeDtypeStruct((T, D_MODEL), F32),
        compiler_params=_cparams(("arbitrary",)),
        name="moe_combine",
    )(*tabs, pos_col, aff, *([ye] * G), x2, final_g)


def _rope_tables(S):
    pos = jnp.arange(S, dtype=F32)
    inv = ROPE_BASE ** (-jnp.arange(0, A_ROPE, 2, dtype=F32) / A_ROPE)
    ang = pos[:, None] * inv[None, :]
    pad = jnp.zeros((S, LANES - A_ROPE), F32)
    cos = jnp.concatenate([jnp.cos(ang), jnp.cos(ang), pad], axis=1)
    sin = jnp.concatenate([jnp.sin(ang), jnp.sin(ang), pad], axis=1)
    return cos, sin


def _rotate_half_cols(w):
    half = A_ROPE // 2
    return jnp.concatenate([-w[..., half:], w[..., :half]], axis=-1)


def _prep_weights(norm_mix_g, w_in, b_gates, conv_w, conv_b, mh_norm_g, g_cq, g_ckv, w_uq, w_ukv,
                  w_br_a, w_br_b, w_out, norm_x_g, norm_mem_g, w_xq, w_xk, w_xv, w_xo, norm_ffn_g,
                  w_router, b_router, w_e_gate, w_e_up, w_e_down, final_norm_g):
    l = 0
    wi = w_in[l]
    o = 0
    cols = {}
    for name, n in (("qm", M_WIDTH), ("km", M_WIDTH), ("vm", M_WIDTH), ("om", M_WIDTH), ("gates", 4 * M_HEADS),
                    ("cq", A_Q_RANK), ("ckv", A_KV_RANK), ("kr", A_ROPE), ("ga", D_MODEL), ("gb", D_MODEL)):
        cols[name] = wi[:, o:o + n]
        o += n
    zpad = jnp.zeros((D_MODEL, LANES - A_ROPE), F32)
    w_kr = jnp.concatenate([cols["kr"], zpad, _rotate_half_cols(cols["kr"]), zpad], axis=1)
    uq = w_uq[l].reshape(A_Q_RANK, A_HEADS, A_NOPE + A_ROPE)
    uq_rope = uq[:, :, A_NOPE:]
    hpad = jnp.zeros((A_Q_RANK, A_HEADS, LANES - A_ROPE), F32)
    w_qa = jnp.concatenate([uq, hpad], axis=2).reshape(A_Q_RANK, A_HEADS * A_QK_PAD)
    w_qr = jnp.concatenate([_rotate_half_cols(uq_rope), hpad], axis=2).reshape(A_Q_RANK, A_HEADS * LANES)
    ukv = w_ukv[l].reshape(A_KV_RANK, A_HEADS, A_NOPE + A_V)
    row = lambda v: v.reshape(1, -1).astype(F32)
    return {
        "norm_mix_g": row(norm_mix_g[l]),
        "w_big": jnp.concatenate([cols[n] for n in ("qm", "km", "vm", "om", "ga", "gb")], axis=1).astype(BF16),
        "w_c": jnp.concatenate([cols["cq"], cols["ckv"]], axis=1).astype(BF16),
        "w_kr": w_kr.astype(BF16),
        "w_g": cols["gates"].astype(BF16),
        "w_gt": cols["gates"].T.astype(BF16),
        "b_g": row(b_gates[l]),
        "b_gt": b_gates[l].reshape(-1, 1).astype(F32),
        "conv_w": conv_w[l],
        "conv_b": row(conv_b[l]),
        "mh_norm_g": row(mh_norm_g[l]),
        "g_cq": row(g_cq[l]),
        "g_ckv": row(g_ckv[l]),
        "w_qat": w_qa.T.astype(BF16),
        "w_qrt": w_qr.T.astype(BF16),
        "w_uk": ukv[:, :, :A_NOPE].reshape(A_KV_RANK, A_HEADS * A_NOPE).astype(BF16),
        "w_uvt": ukv[:, :, A_NOPE:].reshape(A_KV_RANK, A_HEADS * A_V).T.astype(BF16),
        "w_br_a": w_br_a[l].astype(BF16),
        "w_br_b": w_br_b[l].astype(BF16),
        "w_out": w_out[l].astype(BF16),
        "norm_x_g": row(norm_x_g[l]),
        "norm_mem_g": row(norm_mem_g[l]),
        "w_xq": w_xq[l].astype(BF16),
        "w_xk": w_xk[l].astype(BF16),
        "w_xv": w_xv[l].astype(BF16),
        "w_xo": w_xo[l].astype(BF16),
        "norm_ffn_g": row(norm_ffn_g[l]),
        "w_router": w_router[l],
        "w_router_t": w_router[l].T,
        "b_router": row(b_router[l]),
        "b_router_t": b_router[l].reshape(-1, 1).astype(F32),
        "w_e_gate": w_e_gate[l],
        "w_e_up": w_e_up[l],
        "w_e_down": w_e_down[l],
        "final_norm_g": row(final_norm_g),
    }


def _trunk(x, mem, w):
    B, S, _ = x.shape
    T = B * S
    n_mem = mem.shape[1]
    w = dict(w)
    w["cos"], w["sin"] = _rope_tables(S)
    w["cos_t"], w["sin_t"] = w["cos"].T, w["sin"].T
    x2d = x.reshape(T, D_MODEL)

    z, c, kr, gate, gatet = _inproj(x2d, S, w)
    qm = _conv(z, S, w, col0=0, scale=M_HEAD_DIM ** -0.5, transpose=False)
    kmt = _conv(z, S, w, col0=1, scale=1.0, transpose=True)
    hf, hb = _mlstm(qm, kmt, z, gate, gatet, B, S)
    qc, kc, vc = _mla_proj(c, kr, B, S, w)
    att = _flash(qc, kc, vc)
    x1 = _mixer_out(hf, hb, z, att, x2d, w)

    kx, vx = _mem_proj(mem.reshape(B * n_mem, D_MODEL), w)
    x2, xn, aff, aff_t = _cross_router(x1, kx, vx, S, n_mem, w)

    cap = max(1, EC_FACTOR * T // N_EXPERTS)
    pos, bst = _select(aff_t, cap)
    tu = min(MOE_TOKEN_TILE, T)
    tc = min(MOE_SLOT_TILE, cap)
    tuc = min(MOE_COMBINE_TOKEN_TILE, T)
    tcc = min(MOE_COMBINE_SLOT_TILE, cap)
    bst = bst.reshape(N_EXPERTS, -1)
    xe = _moe_gather(xn, pos.reshape(N_EXPERTS, 1, T), _gather_tables(bst, cap, T, tu, tc), cap, tu, tc)
    ye = _moe_ffn(xe, w)
    y = _moe_combine(ye, pos.reshape(N_EXPERTS, T).T, aff, x2, _combine_tables(bst, cap, T, tuc, tcc),
                     w["final_norm_g"], cap, tuc, tcc)
    return y.reshape(B, S, D_MODEL)


def kernel(x_prompt, x_sample, mem_prompt, mem_sample, norm_mix_g, w_in, b_gates, conv_w, conv_b, mh_norm_g, g_cq, g_ckv, w_uq, w_ukv, w_br_a, w_br_b, w_out, norm_x_g, norm_mem_g, w_xq, w_xk, w_xv, w_xo, norm_ffn_g, w_router, b_router, w_e_gate, w_e_up, w_e_down, final_norm_g):
    w = _prep_weights(norm_mix_g, w_in, b_gates, conv_w, conv_b, mh_norm_g, g_cq, g_ckv, w_uq, w_ukv,
                      w_br_a, w_br_b, w_out, norm_x_g, norm_mem_g, w_xq, w_xk, w_xv, w_xo, norm_ffn_g,
                      w_router, b_router, w_e_gate, w_e_up, w_e_down, final_norm_g)
    return (_trunk(x_prompt, mem_prompt, w), _trunk(x_sample, mem_sample, w))
```

```python
import functools

import jax
import jax.numpy as jnp
from jax import lax
from jax.experimental import pallas as pl
from jax.experimental.pallas import tpu as pltpu

F32 = jnp.float32
BF16 = jnp.bfloat16
I32 = jnp.int32

D_MODEL = 1024
M_WIDTH = 1024
M_HEADS = 4
M_HEAD_DIM = M_WIDTH // M_HEADS
M_CHUNK = 128
M_CONV = 5
A_HEADS = 8
A_NOPE = 128
A_ROPE = 64
A_V = 128
A_Q_RANK = 256
A_KV_RANK = 256
A_QK_PAD = 256
ROPE_BASE = 10000.0
X_HEADS = 4
X_HEAD_DIM = D_MODEL // X_HEADS
N_EXPERTS = 16
EC_FACTOR = 2
E_FF = 1024
NORM_EPS = 1e-6
LANES = 128
BIG_COLS = 6 * 1024
VMEM_LIMIT = 56 * 1024 * 1024
MOE_TOKEN_TILE = 1024
MOE_COMBINE_TOKEN_TILE = 512
MOE_SLOT_TILE = 256
MOE_GATHER_GROUP = 4
MOE_COMBINE_SLOT_TILE = 128
MOE_COMBINE_GROUP = 8


def _cparams(sem):
    return pltpu.CompilerParams(dimension_semantics=sem, vmem_limit_bytes=VMEM_LIMIT)


def _dot(a, b):
    return jnp.dot(a, b, preferred_element_type=F32)


def _dot_nt(a, b):
    return lax.dot_general(a, b, (((1,), (1,)), ((), ())), preferred_element_type=F32)


def _dot_tn(a, b):
    return lax.dot_general(a, b, (((0,), (0,)), ((), ())), preferred_element_type=F32)


def _rms(x, g):
    return x * lax.rsqrt(jnp.mean(x * x, axis=-1, keepdims=True) + NORM_EPS) * g


def _sigmoid(x):
    return 1.0 / (1.0 + jnp.exp(-x))


def _split3(x):
    hi = x.astype(BF16)
    r = x - hi.astype(F32)
    mid = r.astype(BF16)
    lo = (r - mid.astype(F32)).astype(BF16)
    return hi, mid, lo


def _inproj_kernel(x_ref, g_ref, wbig_ref, wc_ref, wkr_ref, wg_ref, wgt_ref, bg_ref, bgt_ref,
                   cos_ref, sin_ref,
                   z_ref, c_ref, kr_ref, gate_ref, gatet_ref, xn_scr):
    j = pl.program_id(1)

    @pl.when(j == 0)
    def _():
        xn = _rms(x_ref[...], g_ref[...]).astype(BF16)
        xn_scr[...] = xn
        c_ref[...] = _dot(xn, wc_ref[...]).astype(BF16)
        kr = _dot(xn, wkr_ref[...])
        kr_ref[...] = (kr[:, :LANES] * cos_ref[...] + kr[:, LANES:] * sin_ref[...]).astype(BF16)
        gate_ref[...] = _dot(xn, wg_ref[...]) + bg_ref[...]
        gatet_ref[...] = _dot_nt(wgt_ref[...], xn) + bgt_ref[...]

    z_ref[...] = _dot(xn_scr[...], wbig_ref[...]).astype(BF16)


def _inproj(x, S, w):
    T = x.shape[0]
    tm = min(1024, S)
    tn = 3072
    nseq = S // tm
    row = lambda i, j: (i, 0)
    const = lambda i, j: (0, 0)
    return pl.pallas_call(
        _inproj_kernel,
        grid=(T // tm, BIG_COLS // tn),
        in_specs=[
            pl.BlockSpec((tm, D_MODEL), row),
            pl.BlockSpec((1, D_MODEL), const),
            pl.BlockSpec((D_MODEL, tn), lambda i, j: (0, j)),
            pl.BlockSpec((D_MODEL, 512), const),
            pl.BlockSpec((D_MODEL, 256), const),
            pl.BlockSpec((D_MODEL, 16), const),
            pl.BlockSpec((16, D_MODEL), const),
            pl.BlockSpec((1, 16), const),
            pl.BlockSpec((16, 1), const),
            pl.BlockSpec((tm, LANES), lambda i, j: (i % nseq, 0)),
            pl.BlockSpec((tm, LANES), lambda i, j: (i % nseq, 0)),
        ],
        out_specs=[
            pl.BlockSpec((tm, tn), lambda i, j: (i, j)),
            pl.BlockSpec((tm, 512), row),
            pl.BlockSpec((tm, LANES), row),
            pl.BlockSpec((tm, 16), row),
            pl.BlockSpec((16, tm), lambda i, j: (0, i)),
        ],
        out_shape=[
            jax.ShapeDtypeStruct((T, BIG_COLS), BF16),
            jax.ShapeDtypeStruct((T, 512), BF16),
            jax.ShapeDtypeStruct((T, LANES), BF16),
            jax.ShapeDtypeStruct((T, 16), F32),
            jax.ShapeDtypeStruct((16, T), F32),
        ],
        scratch_shapes=[pltpu.VMEM((tm, D_MODEL), BF16)],
        compiler_params=_cparams(("parallel", "arbitrary")),
        name="inproj",
    )(x, w["norm_mix_g"], w["w_big"], w["w_c"], w["w_kr"], w["w_g"], w["w_gt"], w["b_g"], w["b_gt"],
      w["cos"], w["sin"])


CONV_HALO = 16


def _conv_kernel(z_ref, zp_ref, zn_ref, w_ref, b_ref, o_ref, scr, *, tr, tiles_per_seq, scale, transpose):
    it = pl.program_id(0) % tiles_per_seq
    keep_prev = jnp.where(it == 0, 0.0, 1.0)
    keep_next = jnp.where(it == tiles_per_seq - 1, 0.0, 1.0)
    scr[0:8, :] = zp_ref[...].astype(F32)[8:16, :] * keep_prev
    scr[8:8 + tr, :] = z_ref[...].astype(F32)
    scr[8 + tr:16 + tr, :] = zn_ref[...].astype(F32)[0:8, :] * keep_next
    acc = jnp.zeros((tr, scr.shape[1]), F32) + b_ref[...]
    xe = scr[...]
    for k in range(M_CONV):
        d = k - M_CONV // 2
        win = xe if d == 0 else pltpu.roll(xe, (-d) % (tr + 16), axis=0)
        acc = acc + w_ref[k:k + 1, :] * win[8:8 + tr, :]
    y = acc * _sigmoid(acc) * scale
    o_ref[...] = (y.T if transpose else y).astype(BF16)


def _conv(z, S, w, *, col0, scale, transpose):
    T = z.shape[0]
    tr = min(512, S)
    tcw = 1024
    tiles_per_seq = S // tr
    hb = tr // CONV_HALO
    nhalo = T // CONV_HALO
    c0 = col0 * (M_WIDTH // tcw)
    kern = functools.partial(_conv_kernel, tr=tr, tiles_per_seq=tiles_per_seq, scale=scale, transpose=transpose)
    if transpose:
        out_spec = pl.BlockSpec((tcw, tr), lambda i, j: (j, i))
        out_shape = jax.ShapeDtypeStruct((M_WIDTH, T), BF16)
    else:
        out_spec = pl.BlockSpec((tr, tcw), lambda i, j: (i, j))
        out_shape = jax.ShapeDtypeStruct((T, M_WIDTH), BF16)
    return pl.pallas_call(
        kern,
        grid=(T // tr, M_WIDTH // tcw),
        in_specs=[
            pl.BlockSpec((tr, tcw), lambda i, j: (i, c0 + j)),
            pl.BlockSpec((CONV_HALO, tcw), lambda i, j: (jnp.maximum(i * hb - 1, 0), c0 + j)),
            pl.BlockSpec((CONV_HALO, tcw), lambda i, j: (jnp.minimum((i + 1) * hb, nhalo - 1), c0 + j)),
            pl.BlockSpec((M_CONV, tcw), lambda i, j: (0, c0 + j)),
            pl.BlockSpec((1, tcw), lambda i, j: (0, c0 + j)),
        ],
        out_specs=out_spec,
        out_shape=out_shape,
        scratch_shapes=[pltpu.VMEM((tr + 16, tcw), F32)],
        compiler_params=_cparams(("parallel", "parallel")),
        name="conv_silu_t" if transpose else "conv_silu",
    )(z, z, z, w["conv_w"], w["conv_b"])


def _log_sigmoid(x):
    return -(jnp.maximum(-x, 0.0) + jnp.log1p(jnp.exp(-jnp.abs(x))))


M_STATE_COLS = M_HEAD_DIM + LANES


def _mlstm_gates(d, g_ref, gt_ref):
    L = M_CHUNK
    r = lax.broadcasted_iota(I32, (L, L), 0)
    c = lax.broadcasted_iota(I32, (L, L), 1)
    if d == 0:
        mask = c <= r
    else:
        mask = c >= r
    tri_col = jnp.where(mask, 1.0, 0.0).astype(BF16)
    tri_row = jnp.where(r <= c if d == 0 else r >= c, 1.0, 0.0).astype(BF16)
    g = g_ref[...]
    gt = gt_ref[...]
    b_row_all = sum(_dot(p, tri_row) for p in _split3(_log_sigmoid(gt)))
    sel_r = lax.broadcasted_iota(I32, (4 * M_HEADS, M_HEADS * LANES), 0)
    sel_h = lax.broadcasted_iota(I32, (4 * M_HEADS, M_HEADS * LANES), 1) // LANES
    pick_i = jnp.where(sel_r == d * 2 * M_HEADS + sel_h, 1.0, 0.0).astype(BF16)
    pick_f = jnp.where(sel_r == d * 2 * M_HEADS + M_HEADS + sel_h, 1.0, 0.0).astype(BF16)
    i_bc_all = sum(_dot(p, pick_i) for p in _split3(g))
    b_bc_all = sum(_dot(tri_col, _dot(p, pick_f).astype(BF16)) for p in _split3(_log_sigmoid(g)))
    return mask, i_bc_all, b_bc_all, gt, b_row_all


def _mlstm_kernel(qf, ktf, vf, gf, gtf, qb, ktb, vb, gb, gtb, of, ob, c_scr, m_scr):
    @pl.when(pl.program_id(1) == 0)
    def _():
        c_scr[...] = jnp.zeros(c_scr.shape, F32)
        m_scr[...] = jnp.zeros(m_scr.shape, F32)

    L = M_CHUNK
    gates = (_mlstm_gates(0, gf, gtf), _mlstm_gates(1, gb, gtb))
    refs = ((qf, ktf, vf, of), (qb, ktb, vb, ob))
    units = [(d, h) for d in range(2) for h in range(M_HEADS)]
    sls = [slice(h * M_HEAD_DIM, (h + 1) * M_HEAD_DIM) for _, h in units]
    ones_blk = jnp.ones((L, LANES), BF16)

    qs = [refs[d][0][:, sl] for (d, _), sl in zip(units, sls)]
    kts = [refs[d][1][sl, :] for (d, _), sl in zip(units, sls)]
    vs = [refs[d][2][:, sl] for (d, _), sl in zip(units, sls)]
    csts = [c_scr[u] for u in range(len(units))]
    qk = [_dot(q, kt) for q, kt in zip(qs, kts)]
    qc = [_dot(q, cst.astype(BF16)) for q, cst in zip(qs, csts)]
    i_bc = [gates[d][1][:, h * LANES:(h + 1) * LANES] for d, h in units]
    b_bc = [gates[d][2][:, h * LANES:(h + 1) * LANES] for d, h in units]
    i_row = [gates[d][3][d * 2 * M_HEADS + h:d * 2 * M_HEADS + h + 1, :] for d, h in units]
    b_row = [gates[d][4][d * 2 * M_HEADS + M_HEADS + h:d * 2 * M_HEADS + M_HEADS + h + 1, :] for d, h in units]
    b_last = [b[(L - 1 if d == 0 else 0):(L if d == 0 else 1), :] for (d, _), b in zip(units, b_bc)]
    m_prev = [m_scr[u][0:1, :] for u in range(len(units))]
    gk = [bl - b + i for bl, b, i in zip(b_last, b_bc, i_bc)]
    m_new = [jnp.maximum(bl + mp, jnp.max(x, axis=0, keepdims=True)) for bl, mp, x in zip(b_last, m_prev, gk)]
    decay = [jnp.exp(bl + mp - mn) for bl, mp, mn in zip(b_last, m_prev, m_new)]
    wk = [jnp.exp(x - mn) for x, mn in zip(gk, m_new)]
    wv = [jnp.concatenate([(jnp.concatenate([x, x], axis=1) * v.astype(F32)).astype(BF16), x.astype(BF16)], axis=1)
          for x, v in zip(wk, vs)]
    upd = [_dot(kt, x) for kt, x in zip(kts, wv)]
    a = [b + mp for b, mp in zip(b_bc, m_prev)]
    dm = [jnp.where(gates[d][0], b - br + ir, -jnp.inf) for (d, _), b, br, ir in zip(units, b_bc, b_row, i_row)]
    m_t = [jnp.maximum(x, jnp.max(y, axis=1, keepdims=True)) for x, y in zip(a, dm)]
    w_inter = [jnp.exp(x - mt) for x, mt in zip(a, m_t)]
    s = [x * jnp.exp(y - mt) for x, y, mt in zip(qk, dm, m_t)]
    sv = [_dot(x.astype(BF16), jnp.concatenate([v, ones_blk], axis=1)) for x, v in zip(s, vs)]
    for u, (d, _) in enumerate(units):
        ne = sv[u] + jnp.concatenate([w_inter[u]] * 3, axis=1) * qc[u]
        inv = 1.0 / jnp.maximum(jnp.abs(ne[:, M_HEAD_DIM:]), jnp.exp(-m_t[u]))
        refs[d][3][:, sls[u]] = ne[:, :M_HEAD_DIM] * jnp.concatenate([inv, inv], axis=1)
        c_scr[u] = jnp.concatenate([decay[u]] * 3, axis=1) * csts[u] + upd[u]
        m_scr[u] = jnp.broadcast_to(m_new[u], (8, LANES))


def _mlstm(q, kt, z, gate, gatet, B, S):
    T = B * S
    L = M_CHUNK
    nc = S // L
    fwd = lambda b, c: b * nc + c
    bwd = lambda b, c: b * nc + nc - 1 - c

    def specs(pos):
        return [
            pl.BlockSpec((L, M_WIDTH), lambda b, c: (pos(b, c), 0)),
            pl.BlockSpec((M_WIDTH, L), lambda b, c: (0, pos(b, c))),
            pl.BlockSpec((L, M_WIDTH), lambda b, c: (pos(b, c), 2)),
            pl.BlockSpec((L, 16), lambda b, c: (pos(b, c), 0)),
            pl.BlockSpec((16, L), lambda b, c: (0, pos(b, c))),
        ]

    nstate = 2 * M_HEADS
    return pl.pallas_call(
        _mlstm_kernel,
        grid=(B, nc),
        in_specs=specs(fwd) + specs(bwd),
        out_specs=[
            pl.BlockSpec((L, M_WIDTH), lambda b, c: (fwd(b, c), 0)),
            pl.BlockSpec((L, M_WIDTH), lambda b, c: (bwd(b, c), 0)),
        ],
        out_shape=[jax.ShapeDtypeStruct((T, M_WIDTH), F32)] * 2,
        scratch_shapes=[
            pltpu.VMEM((nstate, M_HEAD_DIM, M_STATE_COLS), F32),
            pltpu.VMEM((nstate, 8, LANES), F32),
        ],
        compiler_params=_cparams(("parallel", "arbitrary")),
        name="mlstm",
    )(q, kt, z, gate, gatet, q, kt, z, gate, gatet)


ATT_TILE = 512


def _flash_tiles(S):
    tk = min(ATT_TILE, S)
    nk = S // tk
    tq = min(1024 if nk <= 8 else 512, S)
    unroll = 4 * ATT_SLOTS if nk >= 16 else 2 * ATT_SLOTS
    return tq, tk, unroll


ATT_SLOTS = 2
ATT_LOGIT_SCALE = (A_NOPE + A_ROPE) ** -0.5 * 1.4426950408889634


def _mla_proj_kernel(c_ref, kr_ref, cost_ref, sint_ref, gq_ref, gkv_ref, wqat_ref, wqrt_ref, wk_ref, wvt_ref,
                     qt_ref, k_ref, vt_ref):
    cq = c_ref[:, :A_Q_RANK].astype(F32)
    ckv = c_ref[:, A_Q_RANK:].astype(F32)
    cqn = _rms(cq, gq_ref[...]).astype(BF16)
    ckvn = _rms(ckv, gkv_ref[...]).astype(BF16)
    qat = _dot_nt(wqat_ref[...], cqn)
    qrt = _dot_nt(wqrt_ref[...], cqn)
    kn = _dot(ckvn, wk_ref[...])
    vt = _dot_nt(wvt_ref[...], ckvn)
    cost = cost_ref[...]
    sint = sint_ref[...]
    kr = kr_ref[...]
    for h in range(A_HEADS):
        o = h * A_QK_PAD
        qt_ref[0, h, :LANES, :] = (qat[o:o + LANES, :] * ATT_LOGIT_SCALE).astype(BF16)
        qt_ref[0, h, LANES:, :] = ((qat[o + LANES:o + 2 * LANES, :] * cost
                                    + qrt[h * LANES:(h + 1) * LANES, :] * sint) * ATT_LOGIT_SCALE).astype(BF16)
        k_ref[0, h, :, :LANES] = kn[:, h * A_NOPE:(h + 1) * A_NOPE].astype(BF16)
        k_ref[0, h, :, LANES:] = kr
        vt_ref[0, h, 0] = vt[h * A_V:(h + 1) * A_V, :].astype(BF16)


def _mla_proj(c, kr, B, S, w):
    tm = min(ATT_TILE, S)
    nseq = S // tm
    row = lambda i: (i, 0)
    const = lambda i: (0, 0)
    seq_t = lambda i: (0, i % nseq)
    return pl.pallas_call(
        _mla_proj_kernel,
        grid=(B * S // tm,),
        in_specs=[
            pl.BlockSpec((tm, 512), row),
            pl.BlockSpec((tm, LANES), row),
            pl.BlockSpec((LANES, tm), seq_t),
            pl.BlockSpec((LANES, tm), seq_t),
            pl.BlockSpec((1, A_Q_RANK), const),
            pl.BlockSpec((1, A_KV_RANK), const),
            pl.BlockSpec((A_HEADS * A_QK_PAD, A_Q_RANK), const),
            pl.BlockSpec((A_HEADS * LANES, A_Q_RANK), const),
            pl.BlockSpec((A_KV_RANK, A_HEADS * A_NOPE), const),
            pl.BlockSpec((A_HEADS * A_V, A_KV_RANK), const),
        ],
        out_specs=[
            pl.BlockSpec((1, A_HEADS, A_QK_PAD, tm), lambda i: (i // nseq, 0, 0, i % nseq)),
            pl.BlockSpec((1, A_HEADS, tm, A_QK_PAD), lambda i: (i // nseq, 0, i % nseq, 0)),
            pl.BlockSpec((1, A_HEADS, 1, A_V, tm), lambda i: (i // nseq, 0, i % nseq, 0, 0)),
        ],
        out_shape=[
            jax.ShapeDtypeStruct((B, A_HEADS, A_QK_PAD, S), BF16),
            jax.ShapeDtypeStruct((B, A_HEADS, S, A_QK_PAD), BF16),
            jax.ShapeDtypeStruct((B, A_HEADS, nseq, A_V, tm), BF16),
        ],
        compiler_params=_cparams(("parallel",)),
        name="mla_proj",
    )(c, kr, w["cos_t"], w["sin_t"], w["g_cq"], w["g_ckv"], w["w_qat"], w["w_qrt"], w["w_uk"], w["w_uvt"])


def _flash_kernel(q_ref, k_ref, vt_ref, o_ref, s_scr, *, tk, nk, unroll):
    qt = q_ref[0, 0]
    tq = qt.shape[1]

    def scores(j, slot):
        start = pl.multiple_of(j * tk, tk)
        s_scr[slot] = _dot(k_ref[0, 0, pl.ds(start, tk), :], qt)

    def accumulate(j, slot, carry):
        m, l, acc = carry
        s = s_scr[slot]
        m_new = jnp.maximum(m, jnp.max(s, axis=0, keepdims=True))
        p = jnp.exp2(s - m_new)
        alpha = jnp.exp2(m - m_new)
        l = alpha * l + jnp.sum(p, axis=0, keepdims=True)
        acc = alpha * acc + _dot(vt_ref[0, 0, j], p.astype(BF16))
        return m_new, l, acc

    nslot = s_scr.shape[0]
    ahead = nslot - 1

    def body(jj, carry):
        for t in range(unroll):
            j = unroll * jj + t
            scores(jnp.minimum(j + ahead, nk - 1), (t + ahead) % nslot)
            carry = accumulate(j, t % nslot, carry)
        return carry

    carry = (jnp.full((1, tq), -jnp.inf, F32), jnp.zeros((1, tq), F32), jnp.zeros((A_V, tq), F32))
    for j in range(min(ahead, nk)):
        scores(j, j % nslot)
    n_loop = nk // unroll
    if n_loop > 0:
        carry = lax.fori_loop(0, n_loop, body, carry)
    for j in range(unroll * n_loop, nk):
        if j + ahead < nk:
            scores(j + ahead, (j + ahead) % nslot)
        carry = accumulate(j, j % nslot, carry)
    _, l, acc = carry
    o_ref[...] = (acc / l).T.astype(BF16)


def _flash(qt, k, vt):
    B, H, S, _ = k.shape
    tq, tk, unroll = _flash_tiles(S)
    nq = S // tq
    nk = S // tk
    kern = functools.partial(_flash_kernel, tk=tk, nk=nk, unroll=unroll)
    return pl.pallas_call(
        kern,
        grid=(B, H, nq),
        in_specs=[
            pl.BlockSpec((1, 1, A_QK_PAD, tq), lambda b, h, i: (b, h, 0, i)),
            pl.BlockSpec((1, 1, S, A_QK_PAD), lambda b, h, i: (b, h, 0, 0)),
            pl.BlockSpec((1, 1, nk, A_V, tk), lambda b, h, i: (b, h, 0, 0, 0)),
        ],
        out_specs=pl.BlockSpec((tq, A_V), lambda b, h, i: (b * nq + i, h)),
        out_shape=jax.ShapeDtypeStruct((B * S, A_HEADS * A_V), BF16),
        scratch_shapes=[pltpu.VMEM((ATT_SLOTS, tk, tq), F32)],
        compiler_params=_cparams(("parallel", "parallel", "arbitrary")),
        name="mla_flash",
    )(qt, k, vt)


def _mixer_out_kernel(hf_ref, hb_ref, om_ref, ga_ref, gb_ref, att_ref, x_ref, mg_ref,
                      wa_ref, wb_ref, wo_ref, o_ref):
    hs = hf_ref[...] + hb_ref[...]
    mg = mg_ref[...]
    parts = []
    for h in range(M_HEADS):
        sl = slice(h * M_HEAD_DIM, (h + 1) * M_HEAD_DIM)
        parts.append(_rms(hs[:, sl], mg[:, sl]))
    hn = jnp.concatenate(parts, axis=1) * _sigmoid(om_ref[...].astype(F32))
    y_a = _dot(hn.astype(BF16), wa_ref[...])
    y_b = _dot(att_ref[...], wb_ref[...])
    merged = _sigmoid(ga_ref[...].astype(F32)) * y_a + _sigmoid(gb_ref[...].astype(F32)) * y_b
    o_ref[...] = x_ref[...] + _dot(merged.astype(BF16), wo_ref[...])


def _mixer_out(hf, hb, z, att, x, w):
    T = x.shape[0]
    tm = min(512, T)
    row = lambda i: (i, 0)
    const = lambda i: (0, 0)
    wspec = pl.BlockSpec((D_MODEL, D_MODEL), const)
    return pl.pallas_call(
        _mixer_out_kernel,
        grid=(T // tm,),
        in_specs=[
            pl.BlockSpec((tm, M_WIDTH), row),
            pl.BlockSpec((tm, M_WIDTH), row),
            pl.BlockSpec((tm, M_WIDTH), lambda i: (i, 3)),
            pl.BlockSpec((tm, D_MODEL), lambda i: (i, 4)),
            pl.BlockSpec((tm, D_MODEL), lambda i: (i, 5)),
            pl.BlockSpec((tm, D_MODEL), row),
            pl.BlockSpec((tm, D_MODEL), row),
            pl.BlockSpec((1, M_WIDTH), const),
            wspec, wspec, wspec,
        ],
        out_specs=pl.BlockSpec((tm, D_MODEL), row),
        out_shape=jax.ShapeDtypeStruct((T, D_MODEL), F32),
        compiler_params=_cparams(("parallel",)),
        name="mixer_out",
    )(hf, hb, z, z, z, att, x, w["mh_norm_g"], w["w_br_a"], w["w_br_b"], w["w_out"])


def _mem_kernel(m_ref, g_ref, wk_ref, wv_ref, k_ref, v_ref):
    mn = _rms(m_ref[...], g_ref[...]).astype(BF16)
    k_ref[...] = _dot(mn, wk_ref[...]).astype(BF16)
    v_ref[...] = _dot(mn, wv_ref[...]).astype(BF16)


def _mem_proj(mem, w):
    R = mem.shape[0]
    tm = 256
    row = lambda i: (i, 0)
    const = lambda i: (0, 0)
    wspec = pl.BlockSpec((D_MODEL, D_MODEL), const)
    return pl.pallas_call(
        _mem_kernel,
        grid=(R // tm,),
        in_specs=[pl.BlockSpec((tm, D_MODEL), row), pl.BlockSpec((1, D_MODEL), const), wspec, wspec],
        out_specs=[pl.BlockSpec((tm, D_MODEL), row)] * 2,
        out_shape=[jax.ShapeDtypeStruct((R, D_MODEL), BF16)] * 2,
        compiler_params=_cparams(("parallel",)),
        name="mem_proj",
    )(mem, w["norm_mem_g"], w["w_xk"], w["w_xv"])


def _cross_router_kernel(x_ref, kx_ref, vx_ref, gx_ref, gf_ref, wq_ref, wo_ref, wr_ref, wrt_ref,
                         br_ref, brt_ref, x2_ref, xn_ref, aff_ref, afft_ref):
    x1 = x_ref[...]
    xn = _rms(x1, gx_ref[...]).astype(BF16)
    q = _dot(xn, wq_ref[...])
    sls = [slice(h * X_HEAD_DIM, (h + 1) * X_HEAD_DIM) for h in range(X_HEADS)]
    ss = [_dot_nt(q[:, sl].astype(BF16), kx_ref[:, sl]) * (X_HEAD_DIM ** -0.5) for sl in sls]
    es = [jnp.exp(s - jnp.max(s, axis=1, keepdims=True)) for s in ss]
    ps = [e / jnp.sum(e, axis=1, keepdims=True) for e in es]
    o = jnp.concatenate([_dot(p.astype(BF16), vx_ref[:, sl]) for p, sl in zip(ps, sls)], axis=1)
    x2 = x1 + _dot(o.astype(BF16), wo_ref[...])
    x2_ref[...] = x2
    xf = _rms(x2, gf_ref[...])
    hi = xf.astype(BF16)
    xn_ref[...] = hi
    lo = (xf - hi.astype(F32)).astype(BF16)
    wr = wr_ref[...]
    wr_hi = wr.astype(BF16)
    wr_lo = (wr - wr_hi.astype(F32)).astype(BF16)
    logit = _dot(hi, wr_hi) + _dot(lo, wr_hi) + _dot(hi, wr_lo) + br_ref[...]
    e = jnp.exp(logit - jnp.max(logit, axis=1, keepdims=True))
    aff_ref[...] = e / jnp.sum(e, axis=1, keepdims=True)
    wrt = wrt_ref[...]
    wrt_hi = wrt.astype(BF16)
    wrt_lo = (wrt - wrt_hi.astype(F32)).astype(BF16)
    logit_t = _dot_nt(wrt_hi, hi) + _dot_nt(wrt_hi, lo) + _dot_nt(wrt_lo, hi) + brt_ref[...]
    et = jnp.exp(logit_t - jnp.max(logit_t, axis=0, keepdims=True))
    afft_ref[...] = et / jnp.sum(et, axis=0, keepdims=True)


def _cross_router(x1, kx, vx, S, n_mem, w):
    T = x1.shape[0]
    tm = min(512, S)
    nseq = S // tm
    row = lambda i: (i, 0)
    const = lambda i: (0, 0)
    wspec = pl.BlockSpec((D_MODEL, D_MODEL), const)
    memspec = pl.BlockSpec((n_mem, D_MODEL), lambda i: (i // nseq, 0))
    return pl.pallas_call(
        _cross_router_kernel,
        grid=(T // tm,),
        in_specs=[
            pl.BlockSpec((tm, D_MODEL), row), memspec, memspec,
            pl.BlockSpec((1, D_MODEL), const), pl.BlockSpec((1, D_MODEL), const),
            wspec, wspec,
            pl.BlockSpec((D_MODEL, N_EXPERTS), const), pl.BlockSpec((N_EXPERTS, D_MODEL), const),
            pl.BlockSpec((1, N_EXPERTS), const), pl.BlockSpec((N_EXPERTS, 1), const),
        ],
        out_specs=[
            pl.BlockSpec((tm, D_MODEL), row),
            pl.BlockSpec((tm, D_MODEL), row),
            pl.BlockSpec((tm, N_EXPERTS), row),
            pl.BlockSpec((N_EXPERTS, tm), lambda i: (0, i)),
        ],
        out_shape=[
            jax.ShapeDtypeStruct((T, D_MODEL), F32),
            jax.ShapeDtypeStruct((T, D_MODEL), BF16),
            jax.ShapeDtypeStruct((T, N_EXPERTS), F32),
            jax.ShapeDtypeStruct((N_EXPERTS, T), F32),
        ],
        compiler_params=_cparams(("parallel",)),
        name="cross_router",
    )(x1, kx, vx, w["norm_x_g"], w["norm_ffn_g"], w["w_xq"], w["w_xo"], w["w_router"], w["w_router_t"],
      w["b_router"], w["b_router_t"])


def _excl_cumsum(mask_f, strict_lane, strict_blk):
    nb = mask_f.shape[0]
    within = _dot(mask_f.astype(BF16), strict_lane)
    tot = jnp.sum(mask_f, axis=1, keepdims=True)
    bstart = _dot(strict_blk, jnp.broadcast_to(tot, (nb, LANES)).astype(BF16))
    return within + bstart, bstart


def _select_kernel(aff_ref, pos_ref, bst_ref, *, cap):
    a = aff_ref[0]
    nb = a.shape[0]
    bits = pltpu.bitcast(a, I32)

    def enough(cand):
        return jnp.sum(jnp.where(bits >= cand, 1.0, 0.0), axis=(0, 1), keepdims=True) >= cap

    def radix(i, prefix):
        hi = jnp.left_shift(jnp.int32(1), 29 - 2 * i)
        lo = jnp.left_shift(jnp.int32(1), 28 - 2 * i)
        return jnp.where(enough(prefix | hi | lo), prefix | hi | lo,
                         jnp.where(enough(prefix | hi), prefix | hi,
                                   jnp.where(enough(prefix | lo), prefix | lo, prefix)))

    top = jnp.full((1, 1), 1 << 30, I32)
    thr = lax.fori_loop(0, 15, radix, jnp.where(enough(top), top, 0))
    gt = bits > thr
    eq = bits == thr
    need = cap - jnp.sum(jnp.where(gt, 1.0, 0.0), axis=(0, 1), keepdims=True)
    li = lax.broadcasted_iota(I32, (LANES, LANES), 0)
    lj = lax.broadcasted_iota(I32, (LANES, LANES), 1)
    strict_lane = jnp.where(li < lj, 1.0, 0.0).astype(BF16)
    bi = lax.broadcasted_iota(I32, (nb, nb), 0)
    bj = lax.broadcasted_iota(I32, (nb, nb), 1)
    strict_blk = jnp.where(bj < bi, 1.0, 0.0).astype(BF16)
    rank, _ = _excl_cumsum(jnp.where(eq, 1.0, 0.0), strict_lane, strict_blk)
    sel = gt | (eq & (rank < need))
    pos, bstart = _excl_cumsum(jnp.where(sel, 1.0, 0.0), strict_lane, strict_blk)
    pos_ref[0] = jnp.where(sel, pos.astype(I32), -1)
    col = jnp.broadcast_to(bstart[:, 0:1], (nb, nb))
    bst_ref[0] = jnp.sum(jnp.where(bi == bj, col, 0.0), axis=0, keepdims=True).astype(I32)


def _select(aff_t, cap):
    E, T = aff_t.shape
    nb = T // LANES
    kern = functools.partial(_select_kernel, cap=cap)
    return pl.pallas_call(
        kern,
        grid=(E,),
        in_specs=[pl.BlockSpec((1, nb, LANES), lambda e: (e, 0, 0))],
        out_specs=[
            pl.BlockSpec((1, nb, LANES), lambda e: (e, 0, 0)),
            pl.BlockSpec((1, 1, nb), lambda e: (e, 0, 0)),
        ],
        out_shape=[
            jax.ShapeDtypeStruct((E, nb, LANES), I32),
            jax.ShapeDtypeStruct((E, 1, nb), I32),
        ],
        compiler_params=_cparams(("parallel",)),
        name="ec_select",
    )(aff_t.reshape(E, nb, LANES))


def _cells(bst, cap, T, tu, tc):
    E = bst.shape[0]
    ns = cap // tc
    start = bst[:, ::tu // LANES]
    end = jnp.concatenate([start[:, 1:], jnp.full((E, 1), cap, I32)], axis=1)
    cnt = end - start
    s_lo = jnp.minimum(start // tc, ns - 1)
    s_hi = jnp.where(cnt > 0, (end - 1) // tc, s_lo)
    return jnp.where(cnt > 0, s_hi - s_lo + 1, 0), s_lo


def _take(x, idx):
    hit = idx[..., :, None] == jnp.arange(x.shape[-1], dtype=I32)
    return jnp.sum(jnp.where(hit, x[..., None, :], 0), axis=-1)


def _enumerate_cells(nc_flat, slo_flat, steps):
    off_end = jnp.cumsum(nc_flat, axis=-1)
    off = off_end - nc_flat
    total = off_end[..., -1:]
    k = jnp.arange(steps, dtype=I32)
    kk = jnp.minimum(k, total - 1)
    grp = jnp.sum((off_end[..., None, :] <= kk[..., :, None]).astype(I32), axis=-1)
    s = _take(slo_flat - off, grp) + kk
    return grp, s, (k < total).astype(I32)


def _gather_steps(T, cap, tu, tc):
    return T // tu + MOE_GATHER_GROUP * (cap // tc)


def _gather_tables(bst, cap, T, tu, tc):
    E = bst.shape[0]
    G = MOE_GATHER_GROUP
    nu = T // tu
    steps = _gather_steps(T, cap, tu, tc)
    ncell, s_lo = _cells(bst, cap, T, tu, tc)
    own = nu + cap // tc
    _, o_s, o_valid = _enumerate_cells(ncell, s_lo, own)
    prev_s = jnp.concatenate([jnp.full((E, 1), -1, I32), o_s[:, :-1]], axis=1)
    o_first = o_valid * (o_s != prev_s).astype(I32)
    off = jnp.cumsum(ncell, axis=1) - ncell
    n_u = jnp.max(ncell.reshape(E // G, G, nu), axis=1)
    t_u, t_k, t_valid = _enumerate_cells(n_u, jnp.zeros_like(n_u), steps)
    u_e = jnp.repeat(t_u, G, axis=0)
    k_e = jnp.repeat(t_k, G, axis=0)
    nc_e = _take(ncell, u_e)
    off_e = _take(off, u_e)
    has = (k_e < nc_e).astype(I32) * jnp.repeat(t_valid, G, axis=0)
    cid = jnp.clip(off_e + jnp.minimum(k_e, nc_e - 1), 0, own - 1)
    s_e = _take(o_s, cid)
    first_e = has * _take(o_first, cid)
    per_expert = lambda a: a.reshape(E // G, G, steps).transpose(0, 2, 1).reshape(-1)
    return t_u.reshape(-1), per_expert(s_e), per_expert(has), per_expert(first_e)


def _combine_tables(bst, cap, T, tu, tc):
    E = bst.shape[0]
    nu = T // tu
    ns = cap // tc
    G = MOE_COMBINE_GROUP
    ncell, s_lo = _cells(bst, cap, T, tu, tc)
    ncell_c = ncell.at[0].set(jnp.maximum(ncell[0], 1))
    pc = E * (nu + ns)
    c_grp, c_s, _ = _enumerate_cells(ncell_c.T.reshape(-1), s_lo.T.reshape(-1), pc)
    c_e = c_grp % E
    cells_u = jnp.sum(ncell_c, axis=0)
    cell_off = jnp.cumsum(cells_u) - cells_u
    groups_u = (cells_u + G - 1) // G
    grp_end = jnp.cumsum(groups_u)
    n_steps = pc // G + nu
    k = jnp.arange(n_steps, dtype=I32)
    kk = jnp.minimum(k, grp_end[-1] - 1)
    t_u = jnp.sum((grp_end[None, :] <= kk[:, None]).astype(I32), axis=1)
    j = kk - _take(grp_end - groups_u, t_u)
    t_valid = (k < grp_end[-1]).astype(I32)
    local = j[:, None] * G + jnp.arange(G, dtype=I32)[None, :]
    cells_t = _take(cells_u, t_u)[:, None]
    cell_ok = (local < cells_t).astype(I32) * t_valid[:, None]
    cid = jnp.minimum(_take(cell_off, t_u)[:, None] + jnp.minimum(local, cells_t - 1), pc - 1).reshape(-1)
    t_first = t_valid * (j == 0).astype(I32)
    t_last = t_valid * (j == _take(groups_u, t_u) - 1).astype(I32)
    return (t_u, _take(c_e, cid), _take(c_s, cid), cell_ok.reshape(-1), t_valid, t_first, t_last)


def _gather_kernel(u_tab, s_tab, has_tab, first_tab, pos_ref, x_ref, *o_refs, steps, tc):
    G = MOE_GATHER_GROUP
    base = (pl.program_id(0) * steps + pl.program_id(1)) * G
    tu = pos_ref.shape[-1]
    for g, o_ref in enumerate(o_refs):
        slot = lax.broadcasted_iota(I32, (tc, tu), 0) + s_tab[base + g] * tc
        onehot = jnp.where(pos_ref[g] == slot, 1.0, 0.0).astype(BF16)

        @pl.when(first_tab[base + g] == 1)
        def _():
            o_ref[0] = _dot(onehot, x_ref[...]).astype(BF16)

        @pl.when((has_tab[base + g] == 1) & (first_tab[base + g] == 0))
        def _():
            o_ref[0] = o_ref[0] + _dot(onehot, x_ref[...]).astype(BF16)


def _moe_gather(xn, pos_row, tabs, cap, tu, tc):
    T = xn.shape[0]
    E = N_EXPERTS
    G = MOE_GATHER_GROUP
    steps = _gather_steps(T, cap, tu, tc)
    kern = functools.partial(_gather_kernel, steps=steps, tc=tc)

    def out_spec(g):
        return pl.BlockSpec((1, tc, D_MODEL), lambda p, k, u, s, *_: (p, s[(p * steps + k) * G + g], 0))

    grid_spec = pltpu.PrefetchScalarGridSpec(
        num_scalar_prefetch=4,
        grid=(E // G, steps),
        in_specs=[
            pl.BlockSpec((G, 1, tu), lambda p, k, u, *_: (p, 0, u[p * steps + k])),
            pl.BlockSpec((tu, D_MODEL), lambda p, k, u, *_: (u[p * steps + k], 0)),
        ],
        out_specs=[out_spec(g) for g in range(G)],
    )
    return pl.pallas_call(
        kern,
        grid_spec=grid_spec,
        out_shape=[jax.ShapeDtypeStruct((E // G, cap, D_MODEL), BF16)] * G,
        compiler_params=_cparams(("parallel", "arbitrary")),
        name="moe_gather",
    )(*tabs, pos_row, xn)


def _ffn_kernel(*refs):
    G = MOE_GATHER_GROUP
    x_refs = refs[:G]
    wg_ref, wu_ref, wd_ref, o_ref, wg_b, wu_b, wd_b = refs[G:]
    e = pl.program_id(0)

    @pl.when(pl.program_id(1) == 0)
    def _():
        wg_b[...] = wg_ref[0].astype(BF16)
        wu_b[...] = wu_ref[0].astype(BF16)
        wd_b[...] = wd_ref[0].astype(BF16)

    x = x_refs[0][0]
    for g in range(1, G):
        x = jnp.where(e % G == g, x_refs[g][0], x)
    gate = _dot(x, wg_b[...])
    up = _dot(x, wu_b[...])
    h = (gate * _sigmoid(gate) * up).astype(BF16)
    o_ref[0] = _dot(h, wd_b[...]).astype(BF16)


def _moe_ffn(xes, w):
    G = MOE_GATHER_GROUP
    _, cap, _ = xes[0].shape
    E = N_EXPERTS
    tf = min(512, cap)
    wspec = pl.BlockSpec((1, D_MODEL, E_FF), lambda e, i: (e, 0, 0))

    def x_spec(g):
        return pl.BlockSpec((1, tf, D_MODEL), lambda e, i: (e // G, jnp.where(e % G == g, i, 0), 0))

    return pl.pallas_call(
        _ffn_kernel,
        grid=(E, cap // tf),
        in_specs=[x_spec(g) for g in range(G)]
        + [wspec, wspec, pl.BlockSpec((1, E_FF, D_MODEL), lambda e, i: (e, 0, 0))],
        out_specs=pl.BlockSpec((1, tf, D_MODEL), lambda e, i: (e, i, 0)),
        out_shape=jax.ShapeDtypeStruct((E, cap, D_MODEL), BF16),
        scratch_shapes=[pltpu.VMEM((D_MODEL, E_FF), BF16), pltpu.VMEM((D_MODEL, E_FF), BF16),
                        pltpu.VMEM((E_FF, D_MODEL), BF16)],
        compiler_params=_cparams(("parallel", "arbitrary")),
        name="moe_ffn",
    )(*xes, w["w_e_gate"], w["w_e_up"], w["w_e_down"])


def _combine_kernel(u_tab, e_tab, s_tab, ok_tab, valid_tab, first_tab, last_tab, pos_ref, aff_ref, *rest, tc):
    G = MOE_COMBINE_GROUP
    ye_refs = rest[:G]
    x_ref, g_ref, o_ref, acc = rest[G:]
    step = pl.program_id(0)
    tu = pos_ref.shape[0]

    @pl.when(first_tab[step] == 1)
    def _():
        acc[...] = jnp.zeros(acc.shape, F32)

    @pl.when(valid_tab[step] == 1)
    def _():
        pos = pos_ref[...].astype(F32)
        aff = aff_ref[...]
        lane_e = lax.broadcasted_iota(I32, pos.shape, 1)
        lane_r = lax.broadcasted_iota(I32, (tu, tc), 1).astype(F32)
        hits = []
        for g in range(G):
            c = step * G + g
            mine = lane_e == e_tab[c]
            slot = jnp.sum(jnp.where(mine, pos, 0.0), axis=1, keepdims=True)
            gate = jnp.sum(jnp.where(mine, aff, 0.0), axis=1, keepdims=True)
            base = jnp.where(ok_tab[c] == 1, s_tab[c] * tc, -2 * tc).astype(F32)
            hits.append(jnp.where(slot - base == lane_r, gate, 0.0).astype(BF16))
        ye = jnp.concatenate([r[0] for r in ye_refs], axis=0)
        acc[...] = acc[...] + _dot(jnp.concatenate(hits, axis=1), ye)

    @pl.when(last_tab[step] == 1)
    def _():
        o_ref[...] = _rms(x_ref[...] + acc[...], g_ref[...])


def _moe_combine(ye, pos_col, aff, x2, tabs, final_g, cap, tu, tc):
    T = x2.shape[0]
    E = N_EXPERTS
    G = MOE_COMBINE_GROUP
    steps = E * (T // tu + cap // tc) // G + T // tu
    kern = functools.partial(_combine_kernel, tc=tc)

    def ye_spec(g):
        return pl.BlockSpec((1, tc, D_MODEL), lambda k, u, e, s, *_: (e[k * G + g], s[k * G + g], 0))

    tile = pl.BlockSpec((tu, D_MODEL), lambda k, u, *_: (u[k], 0))
    per_expert = pl.BlockSpec((tu, N_EXPERTS), lambda k, u, *_: (u[k], 0))
    grid_spec = pltpu.PrefetchScalarGridSpec(
        num_scalar_prefetch=7,
        grid=(steps,),
        in_specs=[per_expert, per_expert]
        + [ye_spec(g) for g in range(G)]
        + [tile, pl.BlockSpec((1, D_MODEL), lambda k, *_: (0, 0))],
        out_specs=tile,
        scratch_shapes=[pltpu.VMEM((tu, D_MODEL), F32)],
    )
    return pl.pallas_call(
        kern,
        grid_spec=grid_spec,
        out_shape=jax.ShapeDtypeStruct((T, D_MODEL), F32),
        compiler_params=_cparams(("arbitrary",)),
        name="moe_combine",
    )(*tabs, pos_col, aff, *([ye] * G), x2, final_g)


def _rope_tables(S):
    pos = jnp.arange(S, dtype=F32)
    inv = ROPE_BASE ** (-jnp.arange(0, A_ROPE, 2, dtype=F32) / A_ROPE)
    ang = pos[:, None] * inv[None, :]
    pad = jnp.zeros((S, LANES - A_ROPE), F32)
    cos = jnp.concatenate([jnp.cos(ang), jnp.cos(ang), pad], axis=1)
    sin = jnp.concatenate([jnp.sin(ang), jnp.sin(ang), pad], axis=1)
    return cos, sin


def _rotate_half_cols(w):
    half = A_ROPE // 2
    return jnp.concatenate([-w[..., half:], w[..., :half]], axis=-1)


def _prep_weights(norm_mix_g, w_in, b_gates, conv_w, conv_b, mh_norm_g, g_cq, g_ckv, w_uq, w_ukv,
                  w_br_a, w_br_b, w_out, norm_x_g, norm_mem_g, w_xq, w_xk, w_xv, w_xo, norm_ffn_g,
                  w_router, b_router, w_e_gate, w_e_up, w_e_down, final_norm_g):
    l = 0
    wi = w_in[l]
    o = 0
    cols = {}
    for name, n in (("qm", M_WIDTH), ("km", M_WIDTH), ("vm", M_WIDTH), ("om", M_WIDTH), ("gates", 4 * M_HEADS),
                    ("cq", A_Q_RANK), ("ckv", A_KV_RANK), ("kr", A_ROPE), ("ga", D_MODEL), ("gb", D_MODEL)):
        cols[name] = wi[:, o:o + n]
        o += n
    zpad = jnp.zeros((D_MODEL, LANES - A_ROPE), F32)
    w_kr = jnp.concatenate([cols["kr"], zpad, _rotate_half_cols(cols["kr"]), zpad], axis=1)
    uq = w_uq[l].reshape(A_Q_RANK, A_HEADS, A_NOPE + A_ROPE)
    uq_rope = uq[:, :, A_NOPE:]
    hpad = jnp.zeros((A_Q_RANK, A_HEADS, LANES - A_ROPE), F32)
    w_qa = jnp.concatenate([uq, hpad], axis=2).reshape(A_Q_RANK, A_HEADS * A_QK_PAD)
    w_qr = jnp.concatenate([_rotate_half_cols(uq_rope), hpad], axis=2).reshape(A_Q_RANK, A_HEADS * LANES)
    ukv = w_ukv[l].reshape(A_KV_RANK, A_HEADS, A_NOPE + A_V)
    row = lambda v: v.reshape(1, -1).astype(F32)
    return {
        "norm_mix_g": row(norm_mix_g[l]),
        "w_big": jnp.concatenate([cols[n] for n in ("qm", "km", "vm", "om", "ga", "gb")], axis=1).astype(BF16),
        "w_c": jnp.concatenate([cols["cq"], cols["ckv"]], axis=1).astype(BF16),
        "w_kr": w_kr.astype(BF16),
        "w_g": cols["gates"].astype(BF16),
        "w_gt": cols["gates"].T.astype(BF16),
        "b_g": row(b_gates[l]),
        "b_gt": b_gates[l].reshape(-1, 1).astype(F32),
        "conv_w": conv_w[l],
        "conv_b": row(conv_b[l]),
        "mh_norm_g": row(mh_norm_g[l]),
        "g_cq": row(g_cq[l]),
        "g_ckv": row(g_ckv[l]),
        "w_qat": w_qa.T.astype(BF16),
        "w_qrt": w_qr.T.astype(BF16),
        "w_uk": ukv[:, :, :A_NOPE].reshape(A_KV_RANK, A_HEADS * A_NOPE).astype(BF16),
        "w_uvt": ukv[:, :, A_NOPE:].reshape(A_KV_RANK, A_HEADS * A_V).T.astype(BF16),
        "w_br_a": w_br_a[l].astype(BF16),
        "w_br_b": w_br_b[l].astype(BF16),
        "w_out": w_out[l].astype(BF16),
        "norm_x_g": row(norm_x_g[l]),
        "norm_mem_g": row(norm_mem_g[l]),
        "w_xq": w_xq[l].astype(BF16),
        "w_xk": w_xk[l].astype(BF16),
        "w_xv": w_xv[l].astype(BF16),
        "w_xo": w_xo[l].astype(BF16),
        "norm_ffn_g": row(norm_ffn_g[l]),
        "w_router": w_router[l],
        "w_router_t": w_router[l].T,
        "b_router": row(b_router[l]),
        "b_router_t": b_router[l].reshape(-1, 1).astype(F32),
        "w_e_gate": w_e_gate[l],
        "w_e_up": w_e_up[l],
        "w_e_down": w_e_down[l],
        "final_norm_g": row(final_norm_g),
    }


def _trunk(x, mem, w):
    B, S, _ = x.shape
    T = B * S
    n_mem = mem.shape[1]
    w = dict(w)
    w["cos"], w["sin"] = _rope_tables(S)
    w["cos_t"], w["sin_t"] = w["cos"].T, w["sin"].T
    x2d = x.reshape(T, D_MODEL)

    z, c, kr, gate, gatet = _inproj(x2d, S, w)
    qm = _conv(z, S, w, col0=0, scale=M_HEAD_DIM ** -0.5, transpose=False)
    kmt = _conv(z, S, w, col0=1, scale=1.0, transpose=True)
    hf, hb = _mlstm(qm, kmt, z, gate, gatet, B, S)
    qc, kc, vc = _mla_proj(c, kr, B, S, w)
    att = _flash(qc, kc, vc)
    x1 = _mixer_out(hf, hb, z, att, x2d, w)

    kx, vx = _mem_proj(mem.reshape(B * n_mem, D_MODEL), w)
    x2, xn, aff, aff_t = _cross_router(x1, kx, vx, S, n_mem, w)

    cap = max(1, EC_FACTOR * T // N_EXPERTS)
    pos, bst = _select(aff_t, cap)
    tu = min(MOE_TOKEN_TILE, T)
    tc = min(MOE_SLOT_TILE, cap)
    tuc = min(MOE_COMBINE_TOKEN_TILE, T)
    tcc = min(MOE_COMBINE_SLOT_TILE, cap)
    bst = bst.reshape(N_EXPERTS, -1)
    xe = _moe_gather(xn, pos.reshape(N_EXPERTS, 1, T), _gather_tables(bst, cap, T, tu, tc), cap, tu, tc)
    ye = _moe_ffn(xe, w)
    y = _moe_combine(ye, pos.reshape(N_EXPERTS, T).T, aff, x2, _combine_tables(bst, cap, T, tuc, tcc),
                     w["final_norm_g"], cap, tuc, tcc)
    return y.reshape(B, S, D_MODEL)


def kernel(x_prompt, x_sample, mem_prompt, mem_sample, norm_mix_g, w_in, b_gates, conv_w, conv_b, mh_norm_g, g_cq, g_ckv, w_uq, w_ukv, w_br_a, w_br_b, w_out, norm_x_g, norm_mem_g, w_xq, w_xk, w_xv, w_xo, norm_ffn_g, w_router, b_router, w_e_gate, w_e_up, w_e_down, final_norm_g):
    w = _prep_weights(norm_mix_g, w_in, b_gates, conv_w, conv_b, mh_norm_g, g_cq, g_ckv, w_uq, w_ukv,
                      w_br_a, w_br_b, w_out, norm_x_g, norm_mem_g, w_xq, w_xk, w_xv, w_xo, norm_ffn_g,
                      w_router, b_router, w_e_gate, w_e_up, w_e_down, final_norm_g)
    return (_trunk(x_prompt, mem_prompt, w), _trunk(x_sample, mem_sample, w))
```

```python
import functools

import jax
import jax.numpy as jnp
from jax import lax
from jax.experimental import pallas as pl
from jax.experimental.pallas import tpu as pltpu

F32 = jnp.float32
BF16 = jnp.bfloat16
I32 = jnp.int32

D_MODEL = 1024
M_WIDTH = 1024
M_HEADS = 4
M_HEAD_DIM = M_WIDTH // M_HEADS
M_CHUNK = 128
M_CONV = 5
A_HEADS = 8
A_NOPE = 128
A_ROPE = 64
A_V = 128
A_Q_RANK = 256
A_KV_RANK = 256
A_QK_PAD = 256
ROPE_BASE = 10000.0
X_HEADS = 4
X_HEAD_DIM = D_MODEL // X_HEADS
N_EXPERTS = 16
EC_FACTOR = 2
E_FF = 1024
NORM_EPS = 1e-6
LANES = 128
BIG_COLS = 6 * 1024
VMEM_LIMIT = 56 * 1024 * 1024
MOE_TOKEN_TILE = 1024
MOE_COMBINE_TOKEN_TILE = 512
MOE_SLOT_TILE = 256
MOE_GATHER_GROUP = 4
MOE_COMBINE_SLOT_TILE = 128
MOE_COMBINE_GROUP = 8


def _cparams(sem):
    return pltpu.CompilerParams(dimension_semantics=sem, vmem_limit_bytes=VMEM_LIMIT)


def _dot(a, b):
    return jnp.dot(a, b, preferred_element_type=F32)


def _dot_nt(a, b):
    return lax.dot_general(a, b, (((1,), (1,)), ((), ())), preferred_element_type=F32)


def _dot_tn(a, b):
    return lax.dot_general(a, b, (((0,), (0,)), ((), ())), preferred_element_type=F32)


def _rms(x, g):
    return x * lax.rsqrt(jnp.mean(x * x, axis=-1, keepdims=True) + NORM_EPS) * g


def _sigmoid(x):
    return 1.0 / (1.0 + jnp.exp(-x))


def _split3(x):
    hi = x.astype(BF16)
    r = x - hi.astype(F32)
    mid = r.astype(BF16)
    lo = (r - mid.astype(F32)).astype(BF16)
    return hi, mid, lo


def _inproj_kernel(x_ref, g_ref, wbig_ref, wc_ref, wkr_ref, wg_ref, wgt_ref, bg_ref, bgt_ref,
                   cos_ref, sin_ref,
                   z_ref, c_ref, kr_ref, gate_ref, gatet_ref, xn_scr):
    j = pl.program_id(1)

    @pl.when(j == 0)
    def _():
        xn = _rms(x_ref[...], g_ref[...]).astype(BF16)
        xn_scr[...] = xn
        c_ref[...] = _dot(xn, wc_ref[...]).astype(BF16)
        kr = _dot(xn, wkr_ref[...])
        kr_ref[...] = (kr[:, :LANES] * cos_ref[...] + kr[:, LANES:] * sin_ref[...]).astype(BF16)
        gate_ref[...] = _dot(xn, wg_ref[...]) + bg_ref[...]
        gatet_ref[...] = _dot_nt(wgt_ref[...], xn) + bgt_ref[...]

    z_ref[...] = _dot(xn_scr[...], wbig_ref[...]).astype(BF16)


def _inproj(x, S, w):
    T = x.shape[0]
    tm = min(1024, S)
    tn = 3072
    nseq = S // tm
    row = lambda i, j: (i, 0)
    const = lambda i, j: (0, 0)
    return pl.pallas_call(
        _inproj_kernel,
        grid=(T // tm, BIG_COLS // tn),
        in_specs=[
            pl.BlockSpec((tm, D_MODEL), row),
            pl.BlockSpec((1, D_MODEL), const),
            pl.BlockSpec((D_MODEL, tn), lambda i, j: (0, j)),
            pl.BlockSpec((D_MODEL, 512), const),
            pl.BlockSpec((D_MODEL, 256), const),
            pl.BlockSpec((D_MODEL, 16), const),
            pl.BlockSpec((16, D_MODEL), const),
            pl.BlockSpec((1, 16), const),
            pl.BlockSpec((16, 1), const),
            pl.BlockSpec((tm, LANES), lambda i, j: (i % nseq, 0)),
            pl.BlockSpec((tm, LANES), lambda i, j: (i % nseq, 0)),
        ],
        out_specs=[
            pl.BlockSpec((tm, tn), lambda i, j: (i, j)),
            pl.BlockSpec((tm, 512), row),
            pl.BlockSpec((tm, LANES), row),
            pl.BlockSpec((tm, 16), row),
            pl.BlockSpec((16, tm), lambda i, j: (0, i)),
        ],
        out_shape=[
            jax.ShapeDtypeStruct((T, BIG_COLS), BF16),
            jax.ShapeDtypeStruct((T, 512), BF16),
            jax.ShapeDtypeStruct((T, LANES), BF16),
            jax.ShapeDtypeStruct((T, 16), F32),
            jax.ShapeDtypeStruct((16, T), F32),
        ],
        scratch_shapes=[pltpu.VMEM((tm, D_MODEL), BF16)],
        compiler_params=_cparams(("parallel", "arbitrary")),
        name="inproj",
    )(x, w["norm_mix_g"], w["w_big"], w["w_c"], w["w_kr"], w["w_g"], w["w_gt"], w["b_g"], w["b_gt"],
      w["cos"], w["sin"])


CONV_HALO = 16


def _conv_kernel(z_ref, zp_ref, zn_ref, w_ref, b_ref, o_ref, scr, *, tr, tiles_per_seq, scale, transpose):
    it = pl.program_id(0) % tiles_per_seq
    keep_prev = jnp.where(it == 0, 0.0, 1.0)
    keep_next = jnp.where(it == tiles_per_seq - 1, 0.0, 1.0)
    scr[0:8, :] = zp_ref[...].astype(F32)[8:16, :] * keep_prev
    scr[8:8 + tr, :] = z_ref[...].astype(F32)
    scr[8 + tr:16 + tr, :] = zn_ref[...].astype(F32)[0:8, :] * keep_next
    acc = jnp.zeros((tr, scr.shape[1]), F32) + b_ref[...]
    xe = scr[...]
    for k in range(M_CONV):
        d = k - M_CONV // 2
        win = xe if d == 0 else pltpu.roll(xe, (-d) % (tr + 16), axis=0)
        acc = acc + w_ref[k:k + 1, :] * win[8:8 + tr, :]
    y = acc * _sigmoid(acc) * scale
    o_ref[...] = (y.T if transpose else y).astype(BF16)


def _conv(z, S, w, *, col0, scale, transpose):
    T = z.shape[0]
    tr = min(512, S)
    tcw = 1024
    tiles_per_seq = S // tr
    hb = tr // CONV_HALO
    nhalo = T // CONV_HALO
    c0 = col0 * (M_WIDTH // tcw)
    kern = functools.partial(_conv_kernel, tr=tr, tiles_per_seq=tiles_per_seq, scale=scale, transpose=transpose)
    if transpose:
        out_spec = pl.BlockSpec((tcw, tr), lambda i, j: (j, i))
        out_shape = jax.ShapeDtypeStruct((M_WIDTH, T), BF16)
    else:
        out_spec = pl.BlockSpec((tr, tcw), lambda i, j: (i, j))
        out_shape = jax.ShapeDtypeStruct((T, M_WIDTH), BF16)
    return pl.pallas_call(
        kern,
        grid=(T // tr, M_WIDTH // tcw),
        in_specs=[
            pl.BlockSpec((tr, tcw), lambda i, j: (i, c0 + j)),
            pl.BlockSpec((CONV_HALO, tcw), lambda i, j: (jnp.maximum(i * hb - 1, 0), c0 + j)),
            pl.BlockSpec((CONV_HALO, tcw), lambda i, j: (jnp.minimum((i + 1) * hb, nhalo - 1), c0 + j)),
            pl.BlockSpec((M_CONV, tcw), lambda i, j: (0, c0 + j)),
            pl.BlockSpec((1, tcw), lambda i, j: (0, c0 + j)),
        ],
        out_specs=out_spec,
        out_shape=out_shape,
        scratch_shapes=[pltpu.VMEM((tr + 16, tcw), F32)],
        compiler_params=_cparams(("parallel", "parallel")),
        name="conv_silu_t" if transpose else "conv_silu",
    )(z, z, z, w["conv_w"], w["conv_b"])


def _log_sigmoid(x):
    return -(jnp.maximum(-x, 0.0) + jnp.log1p(jnp.exp(-jnp.abs(x))))


M_STATE_COLS = M_HEAD_DIM + LANES


def _mlstm_gates(d, g_ref, gt_ref):
    L = M_CHUNK
    r = lax.broadcasted_iota(I32, (L, L), 0)
    c = lax.broadcasted_iota(I32, (L, L), 1)
    if d == 0:
        mask = c <= r
    else:
        mask = c >= r
    tri_col = jnp.where(mask, 1.0, 0.0).astype(BF16)
    tri_row = jnp.where(r <= c if d == 0 else r >= c, 1.0, 0.0).astype(BF16)
    g = g_ref[...]
    gt = gt_ref[...]
    b_row_all = sum(_dot(p, tri_row) for p in _split3(_log_sigmoid(gt)))
    sel_r = lax.broadcasted_iota(I32, (4 * M_HEADS, M_HEADS * LANES), 0)
    sel_h = lax.broadcasted_iota(I32, (4 * M_HEADS, M_HEADS * LANES), 1) // LANES
    pick_i = jnp.where(sel_r == d * 2 * M_HEADS + sel_h, 1.0, 0.0).astype(BF16)
    pick_f = jnp.where(sel_r == d * 2 * M_HEADS + M_HEADS + sel_h, 1.0, 0.0).astype(BF16)
    i_bc_all = sum(_dot(p, pick_i) for p in _split3(g))
    b_bc_all = sum(_dot(tri_col, _dot(p, pick_f).astype(BF16)) for p in _split3(_log_sigmoid(g)))
    return mask, i_bc_all, b_bc_all, gt, b_row_all


def _mlstm_kernel(qf, ktf, vf, gf, gtf, qb, ktb, vb, gb, gtb, of, ob, c_scr, m_scr):
    @pl.when(pl.program_id(1) == 0)
    def _():
        c_scr[...] = jnp.zeros(c_scr.shape, F32)
        m_scr[...] = jnp.zeros(m_scr.shape, F32)

    L = M_CHUNK
    gates = (_mlstm_gates(0, gf, gtf), _mlstm_gates(1, gb, gtb))
    refs = ((qf, ktf, vf, of), (qb, ktb, vb, ob))
    units = [(d, h) for d in range(2) for h in range(M_HEADS)]
    sls = [slice(h * M_HEAD_DIM, (h + 1) * M_HEAD_DIM) for _, h in units]
    ones_blk = jnp.ones((L, LANES), BF16)

    qs = [refs[d][0][:, sl] for (d, _), sl in zip(units, sls)]
    kts = [refs[d][1][sl, :] for (d, _), sl in zip(units, sls)]
    vs = [refs[d][2][:, sl] for (d, _), sl in zip(units, sls)]
    csts = [c_scr[u] for u in range(len(units))]
    qk = [_dot(q, kt) for q, kt in zip(qs, kts)]
    qc = [_dot(q, cst.astype(BF16)) for q, cst in zip(qs, csts)]
    i_bc = [gates[d][1][:, h * LANES:(h + 1) * LANES] for d, h in units]
    b_bc = [gates[d][2][:, h * LANES:(h + 1) * LANES] for d, h in units]
    i_row = [gates[d][3][d * 2 * M_HEADS + h:d * 2 * M_HEADS + h + 1, :] for d, h in units]
    b_row = [gates[d][4][d * 2 * M_HEADS + M_HEADS + h:d * 2 * M_HEADS + M_HEADS + h + 1, :] for d, h in units]
    b_last = [b[(L - 1 if d == 0 else 0):(L if d == 0 else 1), :] for (d, _), b in zip(units, b_bc)]
    m_prev = [m_scr[u][0:1, :] for u in range(len(units))]
    a = [b + mp for b, mp in zip(b_bc, m_prev)]
    dm = [jnp.where(gates[d][0], b - br + ir, -jnp.inf) for (d, _), b, br, ir in zip(units, b_bc, b_row, i_row)]
    m_t = [jnp.maximum(x, jnp.max(y, axis=1, keepdims=True)) for x, y in zip(a, dm)]
    w_inter = [jnp.exp(x - mt) for x, mt in zip(a, m_t)]
    s = [x * jnp.exp(y - mt) for x, y, mt in zip(qk, dm, m_t)]
    sv = [_dot(x.astype(BF16), jnp.concatenate([v, ones_blk], axis=1)) for x, v in zip(s, vs)]
    gk = [bl - b + i for bl, b, i in zip(b_last, b_bc, i_bc)]
    m_new = [jnp.maximum(bl + mp, jnp.max(x, axis=0, keepdims=True)) for bl, mp, x in zip(b_last, m_prev, gk)]
    decay = [jnp.exp(bl + mp - mn) for bl, mp, mn in zip(b_last, m_prev, m_new)]
    wk = [jnp.exp(x - mn) for x, mn in zip(gk, m_new)]
    wv = [jnp.concatenate([(jnp.concatenate([x, x], axis=1) * v.astype(F32)).astype(BF16), x.astype(BF16)], axis=1)
          for x, v in zip(wk, vs)]
    upd = [_dot(kt, x) for kt, x in zip(kts, wv)]
    for u, (d, _) in enumerate(units):
        ne = sv[u] + jnp.concatenate([w_inter[u]] * 3, axis=1) * qc[u]
        inv = 1.0 / jnp.maximum(jnp.abs(ne[:, M_HEAD_DIM:]), jnp.exp(-m_t[u]))
        refs[d][3][:, sls[u]] = ne[:, :M_HEAD_DIM] * jnp.concatenate([inv, inv], axis=1)
        c_scr[u] = jnp.concatenate([decay[u]] * 3, axis=1) * csts[u] + upd[u]
        m_scr[u] = jnp.broadcast_to(m_new[u], (8, LANES))


def _mlstm(q, kt, z, gate, gatet, B, S):
    T = B * S
    L = M_CHUNK
    nc = S // L
    fwd = lambda b, c: b * nc + c
    bwd = lambda b, c: b * nc + nc - 1 - c

    def specs(pos):
        return [
            pl.BlockSpec((L, M_WIDTH), lambda b, c: (pos(b, c), 0)),
            pl.BlockSpec((M_WIDTH, L), lambda b, c: (0, pos(b, c))),
            pl.BlockSpec((L, M_WIDTH), lambda b, c: (pos(b, c), 2)),
            pl.BlockSpec((L, 16), lambda b, c: (pos(b, c), 0)),
            pl.BlockSpec((16, L), lambda b, c: (0, pos(b, c))),
        ]

    nstate = 2 * M_HEADS
    return pl.pallas_call(
        _mlstm_kernel,
        grid=(B, nc),
        in_specs=specs(fwd) + specs(bwd),
        out_specs=[
            pl.BlockSpec((L, M_WIDTH), lambda b, c: (fwd(b, c), 0)),
            pl.BlockSpec((L, M_WIDTH), lambda b, c: (bwd(b, c), 0)),
        ],
        out_shape=[jax.ShapeDtypeStruct((T, M_WIDTH), F32)] * 2,
        scratch_shapes=[
            pltpu.VMEM((nstate, M_HEAD_DIM, M_STATE_COLS), F32),
            pltpu.VMEM((nstate, 8, LANES), F32),
        ],
        compiler_params=_cparams(("parallel", "arbitrary")),
        name="mlstm",
    )(q, kt, z, gate, gatet, q, kt, z, gate, gatet)


ATT_TILE = 512


def _flash_tiles(S):
    tk = min(ATT_TILE, S)
    nk = S // tk
    tq = min(1024 if nk <= 8 else 512, S)
    unroll = 8 * ATT_SLOTS if nk >= 32 else 2 * ATT_SLOTS
    return tq, tk, unroll


ATT_SLOTS = 2
ATT_LOGIT_SCALE = (A_NOPE + A_ROPE) ** -0.5 * 1.4426950408889634


def _mla_proj_kernel(c_ref, kr_ref, cost_ref, sint_ref, gq_ref, gkv_ref, wqat_ref, wqrt_ref, wk_ref, wvt_ref,
                     qt_ref, k_ref, vt_ref):
    cq = c_ref[:, :A_Q_RANK].astype(F32)
    ckv = c_ref[:, A_Q_RANK:].astype(F32)
    cqn = _rms(cq, gq_ref[...]).astype(BF16)
    ckvn = _rms(ckv, gkv_ref[...]).astype(BF16)
    qat = _dot_nt(wqat_ref[...], cqn)
    qrt = _dot_nt(wqrt_ref[...], cqn)
    kn = _dot(ckvn, wk_ref[...])
    vt = _dot_nt(wvt_ref[...], ckvn)
    cost = cost_ref[...]
    sint = sint_ref[...]
    kr = kr_ref[...]
    for h in range(A_HEADS):
        o = h * A_QK_PAD
        qt_ref[0, h, :LANES, :] = (qat[o:o + LANES, :] * ATT_LOGIT_SCALE).astype(BF16)
        qt_ref[0, h, LANES:, :] = ((qat[o + LANES:o + 2 * LANES, :] * cost
                                    + qrt[h * LANES:(h + 1) * LANES, :] * sint) * ATT_LOGIT_SCALE).astype(BF16)
        k_ref[0, h, :, :LANES] = kn[:, h * A_NOPE:(h + 1) * A_NOPE].astype(BF16)
        k_ref[0, h, :, LANES:] = kr
        vt_ref[0, h, 0] = vt[h * A_V:(h + 1) * A_V, :].astype(BF16)


def _mla_proj(c, kr, B, S, w):
    tm = min(ATT_TILE, S)
    nseq = S // tm
    row = lambda i: (i, 0)
    const = lambda i: (0, 0)
    seq_t = lambda i: (0, i % nseq)
    return pl.pallas_call(
        _mla_proj_kernel,
        grid=(B * S // tm,),
        in_specs=[
            pl.BlockSpec((tm, 512), row),
            pl.BlockSpec((tm, LANES), row),
            pl.BlockSpec((LANES, tm), seq_t),
            pl.BlockSpec((LANES, tm), seq_t),
            pl.BlockSpec((1, A_Q_RANK), const),
            pl.BlockSpec((1, A_KV_RANK), const),
            pl.BlockSpec((A_HEADS * A_QK_PAD, A_Q_RANK), const),
            pl.BlockSpec((A_HEADS * LANES, A_Q_RANK), const),
            pl.BlockSpec((A_KV_RANK, A_HEADS * A_NOPE), const),
            pl.BlockSpec((A_HEADS * A_V, A_KV_RANK), const),
        ],
        out_specs=[
            pl.BlockSpec((1, A_HEADS, A_QK_PAD, tm), lambda i: (i // nseq, 0, 0, i % nseq)),
            pl.BlockSpec((1, A_HEADS, tm, A_QK_PAD), lambda i: (i // nseq, 0, i % nseq, 0)),
            pl.BlockSpec((1, A_HEADS, 1, A_V, tm), lambda i: (i // nseq, 0, i % nseq, 0, 0)),
        ],
        out_shape=[
            jax.ShapeDtypeStruct((B, A_HEADS, A_QK_PAD, S), BF16),
            jax.ShapeDtypeStruct((B, A_HEADS, S, A_QK_PAD), BF16),
            jax.ShapeDtypeStruct((B, A_HEADS, nseq, A_V, tm), BF16),
        ],
        compiler_params=_cparams(("parallel",)),
        name="mla_proj",
    )(c, kr, w["cos_t"], w["sin_t"], w["g_cq"], w["g_ckv"], w["w_qat"], w["w_qrt"], w["w_uk"], w["w_uvt"])


def _flash_kernel(q_ref, k_ref, vt_ref, o_ref, s_scr, *, tk, nk, unroll):
    qt = q_ref[0, 0]
    tq = qt.shape[1]

    def scores(j, slot):
        start = pl.multiple_of(j * tk, tk)
        s_scr[slot] = _dot(k_ref[0, 0, pl.ds(start, tk), :], qt)

    def accumulate(j, slot, carry):
        m, l, acc = carry
        s = s_scr[slot]
        m_new = jnp.maximum(m, jnp.max(s, axis=0, keepdims=True))
        p = jnp.exp2(s - m_new)
        alpha = jnp.exp2(m - m_new)
        l = alpha * l + jnp.sum(p, axis=0, keepdims=True)
        acc = alpha * acc + _dot(vt_ref[0, 0, j], p.astype(BF16))
        return m_new, l, acc

    nslot = s_scr.shape[0]
    ahead = nslot - 1

    def body(jj, carry):
        for t in range(unroll):
            j = unroll * jj + t
            scores(jnp.minimum(j + ahead, nk - 1), (t + ahead) % nslot)
            carry = accumulate(j, t % nslot, carry)
        return carry

    carry = (jnp.full((1, tq), -jnp.inf, F32), jnp.zeros((1, tq), F32), jnp.zeros((A_V, tq), F32))
    for j in range(min(ahead, nk)):
        scores(j, j % nslot)
    n_loop = nk // unroll
    if n_loop > 0:
        carry = lax.fori_loop(0, n_loop, body, carry)
    for j in range(unroll * n_loop, nk):
        if j + ahead < nk:
            scores(j + ahead, (j + ahead) % nslot)
        carry = accumulate(j, j % nslot, carry)
    _, l, acc = carry
    o_ref[...] = (acc / l).T.astype(BF16)


def _flash(qt, k, vt):
    B, H, S, _ = k.shape
    tq, tk, unroll = _flash_tiles(S)
    nq = S // tq
    nk = S // tk
    kern = functools.partial(_flash_kernel, tk=tk, nk=nk, unroll=unroll)
    return pl.pallas_call(
        kern,
        grid=(B, H, nq),
        in_specs=[
            pl.BlockSpec((1, 1, A_QK_PAD, tq), lambda b, h, i: (b, h, 0, i)),
            pl.BlockSpec((1, 1, S, A_QK_PAD), lambda b, h, i: (b, h, 0, 0)),
            pl.BlockSpec((1, 1, nk, A_V, tk), lambda b, h, i: (b, h, 0, 0, 0)),
        ],
        out_specs=pl.BlockSpec((tq, A_V), lambda b, h, i: (b * nq + i, h)),
        out_shape=jax.ShapeDtypeStruct((B * S, A_HEADS * A_V), BF16),
        scratch_shapes=[pltpu.VMEM((ATT_SLOTS, tk, tq), F32)],
        compiler_params=_cparams(("parallel", "parallel", "arbitrary")),
        name="mla_flash",
    )(qt, k, vt)


def _mixer_out_kernel(hf_ref, hb_ref, om_ref, ga_ref, gb_ref, att_ref, x_ref, mg_ref,
                      wa_ref, wb_ref, wo_ref, o_ref):
    hs = hf_ref[...] + hb_ref[...]
    mg = mg_ref[...]
    parts = []
    for h in range(M_HEADS):
        sl = slice(h * M_HEAD_DIM, (h + 1) * M_HEAD_DIM)
        parts.append(_rms(hs[:, sl], mg[:, sl]))
    hn = jnp.concatenate(parts, axis=1) * _sigmoid(om_ref[...].astype(F32))
    y_a = _dot(hn.astype(BF16), wa_ref[...])
    y_b = _dot(att_ref[...], wb_ref[...])
    merged = _sigmoid(ga_ref[...].astype(F32)) * y_a + _sigmoid(gb_ref[...].astype(F32)) * y_b
    o_ref[...] = x_ref[...] + _dot(merged.astype(BF16), wo_ref[...])


def _mixer_out(hf, hb, z, att, x, w):
    T = x.shape[0]
    tm = min(512, T)
    row = lambda i: (i, 0)
    const = lambda i: (0, 0)
    wspec = pl.BlockSpec((D_MODEL, D_MODEL), const)
    return pl.pallas_call(
        _mixer_out_kernel,
        grid=(T // tm,),
        in_specs=[
            pl.BlockSpec((tm, M_WIDTH), row),
            pl.BlockSpec((tm, M_WIDTH), row),
            pl.BlockSpec((tm, M_WIDTH), lambda i: (i, 3)),
            pl.BlockSpec((tm, D_MODEL), lambda i: (i, 4)),
            pl.BlockSpec((tm, D_MODEL), lambda i: (i, 5)),
            pl.BlockSpec((tm, D_MODEL), row),
            pl.BlockSpec((tm, D_MODEL), row),
            pl.BlockSpec((1, M_WIDTH), const),
            wspec, wspec, wspec,
        ],
        out_specs=pl.BlockSpec((tm, D_MODEL), row),
        out_shape=jax.ShapeDtypeStruct((T, D_MODEL), F32),
        compiler_params=_cparams(("parallel",)),
        name="mixer_out",
    )(hf, hb, z, z, z, att, x, w["mh_norm_g"], w["w_br_a"], w["w_br_b"], w["w_out"])


def _mem_kernel(m_ref, g_ref, wk_ref, wv_ref, k_ref, v_ref):
    mn = _rms(m_ref[...], g_ref[...]).astype(BF16)
    k_ref[...] = _dot(mn, wk_ref[...]).astype(BF16)
    v_ref[...] = _dot(mn, wv_ref[...]).astype(BF16)


def _mem_proj(mem, w):
    R = mem.shape[0]
    tm = 256
    row = lambda i: (i, 0)
    const = lambda i: (0, 0)
    wspec = pl.BlockSpec((D_MODEL, D_MODEL), const)
    return pl.pallas_call(
        _mem_kernel,
        grid=(R // tm,),
        in_specs=[pl.BlockSpec((tm, D_MODEL), row), pl.BlockSpec((1, D_MODEL), const), wspec, wspec],
        out_specs=[pl.BlockSpec((tm, D_MODEL), row)] * 2,
        out_shape=[jax.ShapeDtypeStruct((R, D_MODEL), BF16)] * 2,
        compiler_params=_cparams(("parallel",)),
        name="mem_proj",
    )(mem, w["norm_mem_g"], w["w_xk"], w["w_xv"])


def _cross_router_kernel(x_ref, kx_ref, vx_ref, gx_ref, gf_ref, wq_ref, wo_ref, wr_ref, wrt_ref,
                         br_ref, brt_ref, x2_ref, xn_ref, aff_ref, afft_ref):
    x1 = x_ref[...]
    xn = _rms(x1, gx_ref[...]).astype(BF16)
    q = _dot(xn, wq_ref[...])
    sls = [slice(h * X_HEAD_DIM, (h + 1) * X_HEAD_DIM) for h in range(X_HEADS)]
    ss = [_dot_nt(q[:, sl].astype(BF16), kx_ref[:, sl]) * (X_HEAD_DIM ** -0.5) for sl in sls]
    es = [jnp.exp(s - jnp.max(s, axis=1, keepdims=True)) for s in ss]
    ps = [e / jnp.sum(e, axis=1, keepdims=True) for e in es]
    o = jnp.concatenate([_dot(p.astype(BF16), vx_ref[:, sl]) for p, sl in zip(ps, sls)], axis=1)
    x2 = x1 + _dot(o.astype(BF16), wo_ref[...])
    x2_ref[...] = x2
    xf = _rms(x2, gf_ref[...])
    hi = xf.astype(BF16)
    xn_ref[...] = hi
    lo = (xf - hi.astype(F32)).astype(BF16)
    wr = wr_ref[...]
    wr_hi = wr.astype(BF16)
    wr_lo = (wr - wr_hi.astype(F32)).astype(BF16)
    logit = _dot(hi, wr_hi) + _dot(lo, wr_hi) + _dot(hi, wr_lo) + br_ref[...]
    e = jnp.exp(logit - jnp.max(logit, axis=1, keepdims=True))
    aff_ref[...] = e / jnp.sum(e, axis=1, keepdims=True)
    wrt = wrt_ref[...]
    wrt_hi = wrt.astype(BF16)
    wrt_lo = (wrt - wrt_hi.astype(F32)).astype(BF16)
    logit_t = _dot_nt(wrt_hi, hi) + _dot_nt(wrt_hi, lo) + _dot_nt(wrt_lo, hi) + brt_ref[...]
    et = jnp.exp(logit_t - jnp.max(logit_t, axis=0, keepdims=True))
    afft_ref[...] = et / jnp.sum(et, axis=0, keepdims=True)


def _cross_router(x1, kx, vx, S, n_mem, w):
    T = x1.shape[0]
    tm = min(512, S)
    nseq = S // tm
    row = lambda i: (i, 0)
    const = lambda i: (0, 0)
    wspec = pl.BlockSpec((D_MODEL, D_MODEL), const)
    memspec = pl.BlockSpec((n_mem, D_MODEL), lambda i: (i // nseq, 0))
    return pl.pallas_call(
        _cross_router_kernel,
        grid=(T // tm,),
        in_specs=[
            pl.BlockSpec((tm, D_MODEL), row), memspec, memspec,
            pl.BlockSpec((1, D_MODEL), const), pl.BlockSpec((1, D_MODEL), const),
            wspec, wspec,
            pl.BlockSpec((D_MODEL, N_EXPERTS), const), pl.BlockSpec((N_EXPERTS, D_MODEL), const),
            pl.BlockSpec((1, N_EXPERTS), const), pl.BlockSpec((N_EXPERTS, 1), const),
        ],
        out_specs=[
            pl.BlockSpec((tm, D_MODEL), row),
            pl.BlockSpec((tm, D_MODEL), row),
            pl.BlockSpec((tm, N_EXPERTS), row),
            pl.BlockSpec((N_EXPERTS, tm), lambda i: (0, i)),
        ],
        out_shape=[
            jax.ShapeDtypeStruct((T, D_MODEL), F32),
            jax.ShapeDtypeStruct((T, D_MODEL), BF16),
            jax.ShapeDtypeStruct((T, N_EXPERTS), F32),
            jax.ShapeDtypeStruct((N_EXPERTS, T), F32),
        ],
        compiler_params=_cparams(("parallel",)),
        name="cross_router",
    )(x1, kx, vx, w["norm_x_g"], w["norm_ffn_g"], w["w_xq"], w["w_xo"], w["w_router"], w["w_router_t"],
      w["b_router"], w["b_router_t"])


def _excl_cumsum(mask_f, strict_lane, strict_blk):
    nb = mask_f.shape[0]
    within = _dot(mask_f.astype(BF16), strict_lane)
    tot = jnp.sum(mask_f, axis=1, keepdims=True)
    bstart = _dot(strict_blk, jnp.broadcast_to(tot, (nb, LANES)).astype(BF16))
    return within + bstart, bstart


def _select_kernel(aff_ref, pos_ref, bst_ref, *, cap):
    a = aff_ref[0]
    nb = a.shape[0]
    bits = pltpu.bitcast(a, I32)

    def enough(cand):
        return jnp.sum(jnp.where(bits >= cand, 1.0, 0.0), axis=(0, 1), keepdims=True) >= cap

    def radix(i, prefix):
        hi = jnp.left_shift(jnp.int32(1), 29 - 2 * i)
        lo = jnp.left_shift(jnp.int32(1), 28 - 2 * i)
        return jnp.where(enough(prefix | hi | lo), prefix | hi | lo,
                         jnp.where(enough(prefix | hi), prefix | hi,
                                   jnp.where(enough(prefix | lo), prefix | lo, prefix)))

    top = jnp.full((1, 1), 1 << 30, I32)
    thr = lax.fori_loop(0, 15, radix, jnp.where(enough(top), top, 0))
    gt = bits > thr
    eq = bits == thr
    need = cap - jnp.sum(jnp.where(gt, 1.0, 0.0), axis=(0, 1), keepdims=True)
    li = lax.broadcasted_iota(I32, (LANES, LANES), 0)
    lj = lax.broadcasted_iota(I32, (LANES, LANES), 1)
    strict_lane = jnp.where(li < lj, 1.0, 0.0).astype(BF16)
    bi = lax.broadcasted_iota(I32, (nb, nb), 0)
    bj = lax.broadcasted_iota(I32, (nb, nb), 1)
    strict_blk = jnp.where(bj < bi, 1.0, 0.0).astype(BF16)
    rank, _ = _excl_cumsum(jnp.where(eq, 1.0, 0.0), strict_lane, strict_blk)
    sel = gt | (eq & (rank < need))
    pos, bstart = _excl_cumsum(jnp.where(sel, 1.0, 0.0), strict_lane, strict_blk)
    pos_ref[0] = jnp.where(sel, pos.astype(I32), -1)
    col = jnp.broadcast_to(bstart[:, 0:1], (nb, nb))
    bst_ref[0] = jnp.sum(jnp.where(bi == bj, col, 0.0), axis=0, keepdims=True).astype(I32)


def _select(aff_t, cap):
    E, T = aff_t.shape
    nb = T // LANES
    kern = functools.partial(_select_kernel, cap=cap)
    return pl.pallas_call(
        kern,
        grid=(E,),
        in_specs=[pl.BlockSpec((1, nb, LANES), lambda e: (e, 0, 0))],
        out_specs=[
            pl.BlockSpec((1, nb, LANES), lambda e: (e, 0, 0)),
            pl.BlockSpec((1, 1, nb), lambda e: (e, 0, 0)),
        ],
        out_shape=[
            jax.ShapeDtypeStruct((E, nb, LANES), I32),
            jax.ShapeDtypeStruct((E, 1, nb), I32),
        ],
        compiler_params=_cparams(("parallel",)),
        name="ec_select",
    )(aff_t.reshape(E, nb, LANES))


def _cells(bst, cap, T, tu, tc):
    E = bst.shape[0]
    ns = cap // tc
    start = bst[:, ::tu // LANES]
    end = jnp.concatenate([start[:, 1:], jnp.full((E, 1), cap, I32)], axis=1)
    cnt = end - start
    s_lo = jnp.minimum(start // tc, ns - 1)
    s_hi = jnp.where(cnt > 0, (end - 1) // tc, s_lo)
    return jnp.where(cnt > 0, s_hi - s_lo + 1, 0), s_lo


def _take(x, idx):
    hit = idx[..., :, None] == jnp.arange(x.shape[-1], dtype=I32)
    return jnp.sum(jnp.where(hit, x[..., None, :], 0), axis=-1)


def _enumerate_cells(nc_flat, slo_flat, steps):
    off_end = jnp.cumsum(nc_flat, axis=-1)
    off = off_end - nc_flat
    total = off_end[..., -1:]
    k = jnp.arange(steps, dtype=I32)
    kk = jnp.minimum(k, total - 1)
    grp = jnp.sum((off_end[..., None, :] <= kk[..., :, None]).astype(I32), axis=-1)
    s = _take(slo_flat - off, grp) + kk
    return grp, s, (k < total).astype(I32)


def _gather_steps(T, cap, tu, tc):
    return T // tu + MOE_GATHER_GROUP * (cap // tc)


def _gather_tables(bst, cap, T, tu, tc):
    E = bst.shape[0]
    G = MOE_GATHER_GROUP
    nu = T // tu
    steps = _gather_steps(T, cap, tu, tc)
    ncell, s_lo = _cells(bst, cap, T, tu, tc)
    own = nu + cap // tc
    _, o_s, o_valid = _enumerate_cells(ncell, s_lo, own)
    prev_s = jnp.concatenate([jnp.full((E, 1), -1, I32), o_s[:, :-1]], axis=1)
    o_first = o_valid * (o_s != prev_s).astype(I32)
    off = jnp.cumsum(ncell, axis=1) - ncell
    n_u = jnp.max(ncell.reshape(E // G, G, nu), axis=1)
    t_u, t_k, t_valid = _enumerate_cells(n_u, jnp.zeros_like(n_u), steps)
    u_e = jnp.repeat(t_u, G, axis=0)
    k_e = jnp.repeat(t_k, G, axis=0)
    nc_e = _take(ncell, u_e)
    off_e = _take(off, u_e)
    has = (k_e < nc_e).astype(I32) * jnp.repeat(t_valid, G, axis=0)
    cid = jnp.clip(off_e + jnp.minimum(k_e, nc_e - 1), 0, own - 1)
    s_e = _take(o_s, cid)
    first_e = has * _take(o_first, cid)
    per_expert = lambda a: a.reshape(E // G, G, steps).transpose(0, 2, 1).reshape(-1)
    return t_u.reshape(-1), per_expert(s_e), per_expert(has), per_expert(first_e)


def _combine_tables(bst, cap, T, tu, tc):
    E = bst.shape[0]
    nu = T // tu
    ns = cap // tc
    G = MOE_COMBINE_GROUP
    ncell, s_lo = _cells(bst, cap, T, tu, tc)
    ncell_c = ncell.at[0].set(jnp.maximum(ncell[0], 1))
    pc = E * (nu + ns)
    c_grp, c_s, _ = _enumerate_cells(ncell_c.T.reshape(-1), s_lo.T.reshape(-1), pc)
    c_e = c_grp % E
    cells_u = jnp.sum(ncell_c, axis=0)
    cell_off = jnp.cumsum(cells_u) - cells_u
    groups_u = (cells_u + G - 1) // G
    grp_end = jnp.cumsum(groups_u)
    n_steps = pc // G + nu
    k = jnp.arange(n_steps, dtype=I32)
    kk = jnp.minimum(k, grp_end[-1] - 1)
    t_u = jnp.sum((grp_end[None, :] <= kk[:, None]).astype(I32), axis=1)
    j = kk - _take(grp_end - groups_u, t_u)
    t_valid = (k < grp_end[-1]).astype(I32)
    local = j[:, None] * G + jnp.arange(G, dtype=I32)[None, :]
    cells_t = _take(cells_u, t_u)[:, None]
    cell_ok = (local < cells_t).astype(I32) * t_valid[:, None]
    cid = jnp.minimum(_take(cell_off, t_u)[:, None] + jnp.minimum(local, cells_t - 1), pc - 1).reshape(-1)
    t_first = t_valid * (j == 0).astype(I32)
    t_last = t_valid * (j == _take(groups_u, t_u) - 1).astype(I32)
    return (t_u, _take(c_e, cid), _take(c_s, cid), cell_ok.reshape(-1), t_valid, t_first, t_last)


def _gather_kernel(u_tab, s_tab, has_tab, first_tab, pos_ref, x_ref, *o_refs, steps, tc):
    G = MOE_GATHER_GROUP
    base = (pl.program_id(0) * steps + pl.program_id(1)) * G
    tu = pos_ref.shape[-1]
    for g, o_ref in enumerate(o_refs):
        slot = lax.broadcasted_iota(I32, (tc, tu), 0) + s_tab[base + g] * tc
        onehot = jnp.where(pos_ref[g] == slot, 1.0, 0.0).astype(BF16)

        @pl.when(first_tab[base + g] == 1)
        def _():
            o_ref[0] = _dot(onehot, x_ref[...]).astype(BF16)

        @pl.when((has_tab[base + g] == 1) & (first_tab[base + g] == 0))
        def _():
            o_ref[0] = o_ref[0] + _dot(onehot, x_ref[...]).astype(BF16)


def _moe_gather(xn, pos_row, tabs, cap, tu, tc):
    T = xn.shape[0]
    E = N_EXPERTS
    G = MOE_GATHER_GROUP
    steps = _gather_steps(T, cap, tu, tc)
    kern = functools.partial(_gather_kernel, steps=steps, tc=tc)

    def out_spec(g):
        return pl.BlockSpec((1, tc, D_MODEL), lambda p, k, u, s, *_: (p, s[(p * steps + k) * G + g], 0))

    grid_spec = pltpu.PrefetchScalarGridSpec(
        num_scalar_prefetch=4,
        grid=(E // G, steps),
        in_specs=[
            pl.BlockSpec((G, 1, tu), lambda p, k, u, *_: (p, 0, u[p * steps + k])),
            pl.BlockSpec((tu, D_MODEL), lambda p, k, u, *_: (u[p * steps + k], 0)),
        ],
        out_specs=[out_spec(g) for g in range(G)],
    )
    return pl.pallas_call(
        kern,
        grid_spec=grid_spec,
        out_shape=[jax.ShapeDtypeStruct((E // G, cap, D_MODEL), BF16)] * G,
        compiler_params=_cparams(("parallel", "arbitrary")),
        name="moe_gather",
    )(*tabs, pos_row, xn)


def _ffn_kernel(*refs):
    G = MOE_GATHER_GROUP
    x_refs = refs[:G]
    wg_ref, wu_ref, wd_ref, o_ref, wg_b, wu_b, wd_b = refs[G:]
    e = pl.program_id(0)

    @pl.when(pl.program_id(1) == 0)
    def _():
        wg_b[...] = wg_ref[0].astype(BF16)
        wu_b[...] = wu_ref[0].astype(BF16)
        wd_b[...] = wd_ref[0].astype(BF16)

    x = x_refs[0][0]
    for g in range(1, G):
        x = jnp.where(e % G == g, x_refs[g][0], x)
    gate = _dot(x, wg_b[...])
    up = _dot(x, wu_b[...])
    h = (gate * _sigmoid(gate) * up).astype(BF16)
    o_ref[0] = _dot(h, wd_b[...]).astype(BF16)


def _moe_ffn(xes, w):
    G = MOE_GATHER_GROUP
    _, cap, _ = xes[0].shape
    E = N_EXPERTS
    tf = min(512, cap)
    wspec = pl.BlockSpec((1, D_MODEL, E_FF), lambda e, i: (e, 0, 0))

    def x_spec(g):
        return pl.BlockSpec((1, tf, D_MODEL), lambda e, i: (e // G, jnp.where(e % G == g, i, 0), 0))

    return pl.pallas_call(
        _ffn_kernel,
        grid=(E, cap // tf),
        in_specs=[x_spec(g) for g in range(G)]
        + [wspec, wspec, pl.BlockSpec((1, E_FF, D_MODEL), lambda e, i: (e, 0, 0))],
        out_specs=pl.BlockSpec((1, tf, D_MODEL), lambda e, i: (e, i, 0)),
        out_shape=jax.ShapeDtypeStruct((E, cap, D_MODEL), BF16),
        scratch_shapes=[pltpu.VMEM((D_MODEL, E_FF), BF16), pltpu.VMEM((D_MODEL, E_FF), BF16),
                        pltpu.VMEM((E_FF, D_MODEL), BF16)],
        compiler_params=_cparams(("parallel", "arbitrary")),
        name="moe_ffn",
    )(*xes, w["w_e_gate"], w["w_e_up"], w["w_e_down"])


def _combine_kernel(u_tab, e_tab, s_tab, ok_tab, valid_tab, first_tab, last_tab, pos_ref, aff_ref, *rest, tc):
    G = MOE_COMBINE_GROUP
    ye_refs = rest[:G]
    x_ref, g_ref, o_ref, acc = rest[G:]
    step = pl.program_id(0)
    tu = pos_ref.shape[0]

    @pl.when(first_tab[step] == 1)
    def _():
        acc[...] = jnp.zeros(acc.shape, F32)

    @pl.when(valid_tab[step] == 1)
    def _():
        pos = pos_ref[...].astype(F32)
        aff = aff_ref[...]
        lane_e = lax.broadcasted_iota(I32, pos.shape, 1)
        lane_r = lax.broadcasted_iota(I32, (tu, tc), 1).astype(F32)
        hits = []
        for g in range(G):
            c = step * G + g
            mine = lane_e == e_tab[c]
            slot = jnp.sum(jnp.where(mine, pos, 0.0), axis=1, keepdims=True)
            gate = jnp.sum(jnp.where(mine, aff, 0.0), axis=1, keepdims=True)
            base = jnp.where(ok_tab[c] == 1, s_tab[c] * tc, -2 * tc).astype(F32)
            hits.append(jnp.where(slot - base == lane_r, gate, 0.0).astype(BF16))
        ye = jnp.concatenate([r[0] for r in ye_refs], axis=0)
        acc[...] = acc[...] + _dot(jnp.concatenate(hits, axis=1), ye)

    @pl.when(last_tab[step] == 1)
    def _():
        o_ref[...] = _rms(x_ref[...] + acc[...], g_ref[...])


def _moe_combine(ye, pos_col, aff, x2, tabs, final_g, cap, tu, tc):
    T = x2.shape[0]
    E = N_EXPERTS
    G = MOE_COMBINE_GROUP
    steps = E * (T // tu + cap // tc) // G + T // tu
    kern = functools.partial(_combine_kernel, tc=tc)

    def ye_spec(g):
        return pl.BlockSpec((1, tc, D_MODEL), lambda k, u, e, s, *_: (e[k * G + g], s[k * G + g], 0))

    tile = pl.BlockSpec((tu, D_MODEL), lambda k, u, *_: (u[k], 0))
    per_expert = pl.BlockSpec((tu, N_EXPERTS), lambda k, u, *_: (u[k], 0))
    grid_spec = pltpu.PrefetchScalarGridSpec(
        num_scalar_prefetch=7,
        grid=(steps,),
        in_specs=[per_expert, per_expert]
        + [ye_spec(g) for g in range(G)]
        + [tile, pl.BlockSpec((1, D_MODEL), lambda k, *_: (0, 0))],
        out_specs=tile,
        scratch_shapes=[pltpu.VMEM((tu, D_MODEL), F32)],
    )
    return pl.pallas_call(
        kern,
        grid_spec=grid_spec,
        out_shape=jax.ShapeDtypeStruct((T, D_MODEL), F32),
        compiler_params=_cparams(("arbitrary",)),
        name="moe_combine",
    )(*tabs, pos_col, aff, *([ye] * G), x2, final_g)


def _rope_tables(S):
    pos = jnp.arange(S, dtype=F32)
    inv = ROPE_BASE ** (-jnp.arange(0, A_ROPE, 2, dtype=F32) / A_ROPE)
    ang = pos[:, None] * inv[None, :]
    pad = jnp.zeros((S, LANES - A_ROPE), F32)
    cos = jnp.concatenate([jnp.cos(ang), jnp.cos(ang), pad], axis=1)
    sin = jnp.concatenate([jnp.sin(ang), jnp.sin(ang), pad], axis=1)
    return cos, sin


def _rotate_half_cols(w):
    half = A_ROPE // 2
    return jnp.concatenate([-w[..., half:], w[..., :half]], axis=-1)


def _prep_weights(norm_mix_g, w_in, b_gates, conv_w, conv_b, mh_norm_g, g_cq, g_ckv, w_uq, w_ukv,
                  w_br_a, w_br_b, w_out, norm_x_g, norm_mem_g, w_xq, w_xk, w_xv, w_xo, norm_ffn_g,
                  w_router, b_router, w_e_gate, w_e_up, w_e_down, final_norm_g):
    l = 0
    wi = w_in[l]
    o = 0
    cols = {}
    for name, n in (("qm", M_WIDTH), ("km", M_WIDTH), ("vm", M_WIDTH), ("om", M_WIDTH), ("gates", 4 * M_HEADS),
                    ("cq", A_Q_RANK), ("ckv", A_KV_RANK), ("kr", A_ROPE), ("ga", D_MODEL), ("gb", D_MODEL)):
        cols[name] = wi[:, o:o + n]
        o += n
    zpad = jnp.zeros((D_MODEL, LANES - A_ROPE), F32)
    w_kr = jnp.concatenate([cols["kr"], zpad, _rotate_half_cols(cols["kr"]), zpad], axis=1)
    uq = w_uq[l].reshape(A_Q_RANK, A_HEADS, A_NOPE + A_ROPE)
    uq_rope = uq[:, :, A_NOPE:]
    hpad = jnp.zeros((A_Q_RANK, A_HEADS, LANES - A_ROPE), F32)
    w_qa = jnp.concatenate([uq, hpad], axis=2).reshape(A_Q_RANK, A_HEADS * A_QK_PAD)
    w_qr = jnp.concatenate([_rotate_half_cols(uq_rope), hpad], axis=2).reshape(A_Q_RANK, A_HEADS * LANES)
    ukv = w_ukv[l].reshape(A_KV_RANK, A_HEADS, A_NOPE + A_V)
    row = lambda v: v.reshape(1, -1).astype(F32)
    return {
        "norm_mix_g": row(norm_mix_g[l]),
        "w_big": jnp.concatenate([cols[n] for n in ("qm", "km", "vm", "om", "ga", "gb")], axis=1).astype(BF16),
        "w_c": jnp.concatenate([cols["cq"], cols["ckv"]], axis=1).astype(BF16),
        "w_kr": w_kr.astype(BF16),
        "w_g": cols["gates"].astype(BF16),
        "w_gt": cols["gates"].T.astype(BF16),
        "b_g": row(b_gates[l]),
        "b_gt": b_gates[l].reshape(-1, 1).astype(F32),
        "conv_w": conv_w[l],
        "conv_b": row(conv_b[l]),
        "mh_norm_g": row(mh_norm_g[l]),
        "g_cq": row(g_cq[l]),
        "g_ckv": row(g_ckv[l]),
        "w_qat": w_qa.T.astype(BF16),
        "w_qrt": w_qr.T.astype(BF16),
        "w_uk": ukv[:, :, :A_NOPE].reshape(A_KV_RANK, A_HEADS * A_NOPE).astype(BF16),
        "w_uvt": ukv[:, :, A_NOPE:].reshape(A_KV_RANK, A_HEADS * A_V).T.astype(BF16),
        "w_br_a": w_br_a[l].astype(BF16),
        "w_br_b": w_br_b[l].astype(BF16),
        "w_out": w_out[l].astype(BF16),
        "norm_x_g": row(norm_x_g[l]),
        "norm_mem_g": row(norm_mem_g[l]),
        "w_xq": w_xq[l].astype(BF16),
        "w_xk": w_xk[l].astype(BF16),
        "w_xv": w_xv[l].astype(BF16),
        "w_xo": w_xo[l].astype(BF16),
        "norm_ffn_g": row(norm_ffn_g[l]),
        "w_router": w_router[l],
        "w_router_t": w_router[l].T,
        "b_router": row(b_router[l]),
        "b_router_t": b_router[l].reshape(-1, 1).astype(F32),
        "w_e_gate": w_e_gate[l],
        "w_e_up": w_e_up[l],
        "w_e_down": w_e_down[l],
        "final_norm_g": row(final_norm_g),
    }


def _trunk(x, mem, w):
    B, S, _ = x.shape
    T = B * S
    n_mem = mem.shape[1]
    w = dict(w)
    w["cos"], w["sin"] = _rope_tables(S)
    w["cos_t"], w["sin_t"] = w["cos"].T, w["sin"].T
    x2d = x.reshape(T, D_MODEL)

    z, c, kr, gate, gatet = _inproj(x2d, S, w)
    qm = _conv(z, S, w, col0=0, scale=M_HEAD_DIM ** -0.5, transpose=False)
    kmt = _conv(z, S, w, col0=1, scale=1.0, transpose=True)
    hf, hb = _mlstm(qm, kmt, z, gate, gatet, B, S)
    qc, kc, vc = _mla_proj(c, kr, B, S, w)
    att = _flash(qc, kc, vc)
    x1 = _mixer_out(hf, hb, z, att, x2d, w)

    kx, vx = _mem_proj(mem.reshape(B * n_mem, D_MODEL), w)
    x2, xn, aff, aff_t = _cross_router(x1, kx, vx, S, n_mem, w)

    cap = max(1, EC_FACTOR * T // N_EXPERTS)
    pos, bst = _select(aff_t, cap)
    tu = min(MOE_TOKEN_TILE, T)
    tc = min(MOE_SLOT_TILE, cap)
    tuc = min(MOE_COMBINE_TOKEN_TILE, T)
    tcc = min(MOE_COMBINE_SLOT_TILE, cap)
    bst = bst.reshape(N_EXPERTS, -1)
    xe = _moe_gather(xn, pos.reshape(N_EXPERTS, 1, T), _gather_tables(bst, cap, T, tu, tc), cap, tu, tc)
    ye = _moe_ffn(xe, w)
    y = _moe_combine(ye, pos.reshape(N_EXPERTS, T).T, aff, x2, _combine_tables(bst, cap, T, tuc, tcc),
                     w["final_norm_g"], cap, tuc, tcc)
    return y.reshape(B, S, D_MODEL)


def kernel(x_prompt, x_sample, mem_prompt, mem_sample, norm_mix_g, w_in, b_gates, conv_w, conv_b, mh_norm_g, g_cq, g_ckv, w_uq, w_ukv, w_br_a, w_br_b, w_out, norm_x_g, norm_mem_g, w_xq, w_xk, w_xv, w_xo, norm_ffn_g, w_router, b_router, w_e_gate, w_e_up, w_e_down, final_norm_g):
    w = _prep_weights(norm_mix_g, w_in, b_gates, conv_w, conv_b, mh_norm_g, g_cq, g_ckv, w_uq, w_ukv,
                      w_br_a, w_br_b, w_out, norm_x_g, norm_mem_g, w_xq, w_xk, w_xv, w_xo, norm_ffn_g,
                      w_router, b_router, w_e_gate, w_e_up, w_e_down, final_norm_g)
    return (_trunk(x_prompt, mem_prompt, w), _trunk(x_sample, mem_sample, w))
```

```python
import functools

import jax
import jax.numpy as jnp
from jax import lax
from jax.experimental import pallas as pl
from jax.experimental.pallas import tpu as pltpu

F32 = jnp.float32
BF16 = jnp.bfloat16
I32 = jnp.int32

D_MODEL = 1024
M_WIDTH = 1024
M_HEADS = 4
M_HEAD_DIM = M_WIDTH // M_HEADS
M_CHUNK = 128
M_CONV = 5
A_HEADS = 8
A_NOPE = 128
A_ROPE = 64
A_V = 128
A_Q_RANK = 256
A_KV_RANK = 256
A_QK_PAD = 256
ROPE_BASE = 10000.0
X_HEADS = 4
X_HEAD_DIM = D_MODEL // X_HEADS
N_EXPERTS = 16
EC_FACTOR = 2
E_FF = 1024
NORM_EPS = 1e-6
LANES = 128
BIG_COLS = 6 * 1024
VMEM_LIMIT = 56 * 1024 * 1024
MOE_TOKEN_TILE = 1024
MOE_COMBINE_TOKEN_TILE = 512
MOE_SLOT_TILE = 256
MOE_GATHER_GROUP = 4
MOE_COMBINE_SLOT_TILE = 128
MOE_COMBINE_GROUP = 8


def _cparams(sem):
    return pltpu.CompilerParams(dimension_semantics=sem, vmem_limit_bytes=VMEM_LIMIT)


def _dot(a, b):
    return jnp.dot(a, b, preferred_element_type=F32)


def _dot_nt(a, b):
    return lax.dot_general(a, b, (((1,), (1,)), ((), ())), preferred_element_type=F32)


def _dot_tn(a, b):
    return lax.dot_general(a, b, (((0,), (0,)), ((), ())), preferred_element_type=F32)


def _rms(x, g):
    return x * lax.rsqrt(jnp.mean(x * x, axis=-1, keepdims=True) + NORM_EPS) * g


def _sigmoid(x):
    return 1.0 / (1.0 + jnp.exp(-x))


def _split3(x):
    hi = x.astype(BF16)
    r = x - hi.astype(F32)
    mid = r.astype(BF16)
    lo = (r - mid.astype(F32)).astype(BF16)
    return hi, mid, lo


def _inproj_kernel(x_ref, g_ref, wbig_ref, wc_ref, wkr_ref, wg_ref, wgt_ref, bg_ref, bgt_ref,
                   cos_ref, sin_ref,
                   z_ref, c_ref, kr_ref, gate_ref, gatet_ref, xn_scr):
    j = pl.program_id(1)

    @pl.when(j == 0)
    def _():
        xn = _rms(x_ref[...], g_ref[...]).astype(BF16)
        xn_scr[...] = xn
        c_ref[...] = _dot(xn, wc_ref[...]).astype(BF16)
        kr = _dot(xn, wkr_ref[...])
        kr_ref[...] = (kr[:, :LANES] * cos_ref[...] + kr[:, LANES:] * sin_ref[...]).astype(BF16)
        gate_ref[...] = _dot(xn, wg_ref[...]) + bg_ref[...]
        gatet_ref[...] = _dot_nt(wgt_ref[...], xn) + bgt_ref[...]

    z_ref[...] = _dot(xn_scr[...], wbig_ref[...]).astype(BF16)


def _inproj(x, S, w):
    T = x.shape[0]
    tm = min(1024, S)
    tn = 3072
    nseq = S // tm
    row = lambda i, j: (i, 0)
    const = lambda i, j: (0, 0)
    return pl.pallas_call(
        _inproj_kernel,
        grid=(T // tm, BIG_COLS // tn),
        in_specs=[
            pl.BlockSpec((tm, D_MODEL), row),
            pl.BlockSpec((1, D_MODEL), const),
            pl.BlockSpec((D_MODEL, tn), lambda i, j: (0, j)),
            pl.BlockSpec((D_MODEL, 512), const),
            pl.BlockSpec((D_MODEL, 256), const),
            pl.BlockSpec((D_MODEL, 16), const),
            pl.BlockSpec((16, D_MODEL), const),
            pl.BlockSpec((1, 16), const),
            pl.BlockSpec((16, 1), const),
            pl.BlockSpec((tm, LANES), lambda i, j: (i % nseq, 0)),
            pl.BlockSpec((tm, LANES), lambda i, j: (i % nseq, 0)),
        ],
        out_specs=[
            pl.BlockSpec((tm, tn), lambda i, j: (i, j)),
            pl.BlockSpec((tm, 512), row),
            pl.BlockSpec((tm, LANES), row),
            pl.BlockSpec((tm, 16), row),
            pl.BlockSpec((16, tm), lambda i, j: (0, i)),
        ],
        out_shape=[
            jax.ShapeDtypeStruct((T, BIG_COLS), BF16),
            jax.ShapeDtypeStruct((T, 512), BF16),
            jax.ShapeDtypeStruct((T, LANES), BF16),
            jax.ShapeDtypeStruct((T, 16), F32),
            jax.ShapeDtypeStruct((16, T), F32),
        ],
        scratch_shapes=[pltpu.VMEM((tm, D_MODEL), BF16)],
        compiler_params=_cparams(("parallel", "arbitrary")),
        name="inproj",
    )(x, w["norm_mix_g"], w["w_big"], w["w_c"], w["w_kr"], w["w_g"], w["w_gt"], w["b_g"], w["b_gt"],
      w["cos"], w["sin"])


CONV_HALO = 16


def _conv_kernel(z_ref, zp_ref, zn_ref, w_ref, b_ref, o_ref, scr, *, tr, tiles_per_seq, scale, transpose):
    it = pl.program_id(0) % tiles_per_seq
    keep_prev = jnp.where(it == 0, 0.0, 1.0)
    keep_next = jnp.where(it == tiles_per_seq - 1, 0.0, 1.0)
    scr[0:8, :] = zp_ref[...].astype(F32)[8:16, :] * keep_prev
    scr[8:8 + tr, :] = z_ref[...].astype(F32)
    scr[8 + tr:16 + tr, :] = zn_ref[...].astype(F32)[0:8, :] * keep_next
    acc = jnp.zeros((tr, scr.shape[1]), F32) + b_ref[...]
    xe = scr[...]
    for k in range(M_CONV):
        d = k - M_CONV // 2
        win = xe if d == 0 else pltpu.roll(xe, (-d) % (tr + 16), axis=0)
        acc = acc + w_ref[k:k + 1, :] * win[8:8 + tr, :]
    y = acc * _sigmoid(acc) * scale
    o_ref[...] = (y.T if transpose else y).astype(BF16)


def _conv(z, S, w, *, col0, scale, transpose):
    T = z.shape[0]
    tr = min(512, S)
    tcw = 1024
    tiles_per_seq = S // tr
    hb = tr // CONV_HALO
    nhalo = T // CONV_HALO
    c0 = col0 * (M_WIDTH // tcw)
    kern = functools.partial(_conv_kernel, tr=tr, tiles_per_seq=tiles_per_seq, scale=scale, transpose=transpose)
    if transpose:
        out_spec = pl.BlockSpec((tcw, tr), lambda i, j: (j, i))
        out_shape = jax.ShapeDtypeStruct((M_WIDTH, T), BF16)
    else:
        out_spec = pl.BlockSpec((tr, tcw), lambda i, j: (i, j))
        out_shape = jax.ShapeDtypeStruct((T, M_WIDTH), BF16)
    return pl.pallas_call(
        kern,
        grid=(T // tr, M_WIDTH // tcw),
        in_specs=[
            pl.BlockSpec((tr, tcw), lambda i, j: (i, c0 + j)),
            pl.BlockSpec((CONV_HALO, tcw), lambda i, j: (jnp.maximum(i * hb - 1, 0), c0 + j)),
            pl.BlockSpec((CONV_HALO, tcw), lambda i, j: (jnp.minimum((i + 1) * hb, nhalo - 1), c0 + j)),
            pl.BlockSpec((M_CONV, tcw), lambda i, j: (0, c0 + j)),
            pl.BlockSpec((1, tcw), lambda i, j: (0, c0 + j)),
        ],
        out_specs=out_spec,
        out_shape=out_shape,
        scratch_shapes=[pltpu.VMEM((tr + 16, tcw), F32)],
        compiler_params=_cparams(("parallel", "parallel")),
        name="conv_silu_t" if transpose else "conv_silu",
    )(z, z, z, w["conv_w"], w["conv_b"])


def _log_sigmoid(x):
    return -(jnp.maximum(-x, 0.0) + jnp.log1p(jnp.exp(-jnp.abs(x))))


M_STATE_COLS = M_HEAD_DIM + LANES


def _mlstm_gates(d, g_ref, gt_ref):
    L = M_CHUNK
    r = lax.broadcasted_iota(I32, (L, L), 0)
    c = lax.broadcasted_iota(I32, (L, L), 1)
    if d == 0:
        mask = c <= r
    else:
        mask = c >= r
    tri_col = jnp.where(mask, 1.0, 0.0).astype(BF16)
    tri_row = jnp.where(r <= c if d == 0 else r >= c, 1.0, 0.0).astype(BF16)
    g = g_ref[...]
    gt = gt_ref[...]
    b_row_all = sum(_dot(p, tri_row) for p in _split3(_log_sigmoid(gt)))
    sel_r = lax.broadcasted_iota(I32, (4 * M_HEADS, M_HEADS * LANES), 0)
    sel_h = lax.broadcasted_iota(I32, (4 * M_HEADS, M_HEADS * LANES), 1) // LANES
    pick_i = jnp.where(sel_r == d * 2 * M_HEADS + sel_h, 1.0, 0.0).astype(BF16)
    pick_f = jnp.where(sel_r == d * 2 * M_HEADS + M_HEADS + sel_h, 1.0, 0.0).astype(BF16)
    i_bc_all = sum(_dot(p, pick_i) for p in _split3(g))
    b_bc_all = sum(_dot(tri_col, _dot(p, pick_f).astype(BF16)) for p in _split3(_log_sigmoid(g)))
    return mask, i_bc_all, b_bc_all, gt, b_row_all


def _mlstm_kernel(qf, ktf, vf, gf, gtf, qb, ktb, vb, gb, gtb, of, ob, c_scr, m_scr):
    @pl.when(pl.program_id(1) == 0)
    def _():
        c_scr[...] = jnp.zeros(c_scr.shape, F32)
        m_scr[...] = jnp.zeros(m_scr.shape, F32)

    L = M_CHUNK
    gates = (_mlstm_gates(0, gf, gtf), _mlstm_gates(1, gb, gtb))
    refs = ((qf, ktf, vf, of), (qb, ktb, vb, ob))
    units = [(d, h) for d in range(2) for h in range(M_HEADS)]
    sls = [slice(h * M_HEAD_DIM, (h + 1) * M_HEAD_DIM) for _, h in units]
    ones_blk = jnp.ones((L, LANES), BF16)

    qs = [refs[d][0][:, sl] for (d, _), sl in zip(units, sls)]
    kts = [refs[d][1][sl, :] for (d, _), sl in zip(units, sls)]
    vs = [refs[d][2][:, sl] for (d, _), sl in zip(units, sls)]
    csts = [c_scr[u] for u in range(len(units))]
    qk = [_dot(q, kt) for q, kt in zip(qs, kts)]
    qc = [_dot(q, cst.astype(BF16)) for q, cst in zip(qs, csts)]
    i_bc = [gates[d][1][:, h * LANES:(h + 1) * LANES] for d, h in units]
    b_bc = [gates[d][2][:, h * LANES:(h + 1) * LANES] for d, h in units]
    i_row = [gates[d][3][d * 2 * M_HEADS + h:d * 2 * M_HEADS + h + 1, :] for d, h in units]
    b_row = [gates[d][4][d * 2 * M_HEADS + M_HEADS + h:d * 2 * M_HEADS + M_HEADS + h + 1, :] for d, h in units]
    b_last = [b[(L - 1 if d == 0 else 0):(L if d == 0 else 1), :] for (d, _), b in zip(units, b_bc)]
    m_prev = [m_scr[u][0:1, :] for u in range(len(units))]
    a = [b + mp for b, mp in zip(b_bc, m_prev)]
    dm = [jnp.where(gates[d][0], b - br + ir, -jnp.inf) for (d, _), b, br, ir in zip(units, b_bc, b_row, i_row)]
    m_t = [jnp.maximum(x, jnp.max(y, axis=1, keepdims=True)) for x, y in zip(a, dm)]
    w_inter = [jnp.exp(x - mt) for x, mt in zip(a, m_t)]
    s = [x * jnp.exp(y - mt) for x, y, mt in zip(qk, dm, m_t)]
    sv = [_dot(x.astype(BF16), jnp.concatenate([v, ones_blk], axis=1)) for x, v in zip(s, vs)]
    gk = [bl - b + i for bl, b, i in zip(b_last, b_bc, i_bc)]
    m_new = [jnp.maximum(bl + mp, jnp.max(x, axis=0, keepdims=True)) for bl, mp, x in zip(b_last, m_prev, gk)]
    decay = [jnp.exp(bl + mp - mn) for bl, mp, mn in zip(b_last, m_prev, m_new)]
    wk = [jnp.exp(x - mn) for x, mn in zip(gk, m_new)]
    wv = [jnp.concatenate([(jnp.concatenate([x, x], axis=1) * v.astype(F32)).astype(BF16), x.astype(BF16)], axis=1)
          for x, v in zip(wk, vs)]
    upd = [_dot(kt, x) for kt, x in zip(kts, wv)]
    for u, (d, _) in enumerate(units):
        ne = sv[u] + jnp.concatenate([w_inter[u]] * 3, axis=1) * qc[u]
        inv = 1.0 / jnp.maximum(jnp.abs(ne[:, M_HEAD_DIM:]), jnp.exp(-m_t[u]))
        refs[d][3][:, sls[u]] = ne[:, :M_HEAD_DIM] * jnp.concatenate([inv, inv], axis=1)
        c_scr[u] = jnp.concatenate([decay[u]] * 3, axis=1) * csts[u] + upd[u]
        m_scr[u] = jnp.broadcast_to(m_new[u], (8, LANES))


def _mlstm(q, kt, z, gate, gatet, B, S):
    T = B * S
    L = M_CHUNK
    nc = S // L
    fwd = lambda b, c: b * nc + c
    bwd = lambda b, c: b * nc + nc - 1 - c

    def specs(pos):
        return [
            pl.BlockSpec((L, M_WIDTH), lambda b, c: (pos(b, c), 0)),
            pl.BlockSpec((M_WIDTH, L), lambda b, c: (0, pos(b, c))),
            pl.BlockSpec((L, M_WIDTH), lambda b, c: (pos(b, c), 2)),
            pl.BlockSpec((L, 16), lambda b, c: (pos(b, c), 0)),
            pl.BlockSpec((16, L), lambda b, c: (0, pos(b, c))),
        ]

    nstate = 2 * M_HEADS
    return pl.pallas_call(
        _mlstm_kernel,
        grid=(B, nc),
        in_specs=specs(fwd) + specs(bwd),
        out_specs=[
            pl.BlockSpec((L, M_WIDTH), lambda b, c: (fwd(b, c), 0)),
            pl.BlockSpec((L, M_WIDTH), lambda b, c: (bwd(b, c), 0)),
        ],
        out_shape=[jax.ShapeDtypeStruct((T, M_WIDTH), F32)] * 2,
        scratch_shapes=[
            pltpu.VMEM((nstate, M_HEAD_DIM, M_STATE_COLS), F32),
            pltpu.VMEM((nstate, 8, LANES), F32),
        ],
        compiler_params=_cparams(("parallel", "arbitrary")),
        name="mlstm",
    )(q, kt, z, gate, gatet, q, kt, z, gate, gatet)


ATT_TILE = 512


def _flash_tiles(S):
    tk = min(ATT_TILE, S)
    nk = S // tk
    tq = min(1024 if nk <= 8 else 512, S)
    unroll = 8 * ATT_SLOTS if nk >= 32 else 2 * ATT_SLOTS
    return tq, tk, unroll


ATT_SLOTS = 2
ATT_LOGIT_SCALE = (A_NOPE + A_ROPE) ** -0.5 * 1.4426950408889634


def _mla_proj_kernel(c_ref, kr_ref, cost_ref, sint_ref, gq_ref, gkv_ref, wqat_ref, wqrt_ref, wk_ref, wvt_ref,
                     qt_ref, k_ref, vt_ref):
    cq = c_ref[:, :A_Q_RANK].astype(F32)
    ckv = c_ref[:, A_Q_RANK:].astype(F32)
    cqn = _rms(cq, gq_ref[...]).astype(BF16)
    ckvn = _rms(ckv, gkv_ref[...]).astype(BF16)
    qat = _dot_nt(wqat_ref[...], cqn)
    qrt = _dot_nt(wqrt_ref[...], cqn)
    kn = _dot(ckvn, wk_ref[...])
    vt = _dot_nt(wvt_ref[...], ckvn)
    cost = cost_ref[...]
    sint = sint_ref[...]
    kr = kr_ref[...]
    for h in range(A_HEADS):
        o = h * A_QK_PAD
        qt_ref[0, h, :LANES, :] = (qat[o:o + LANES, :] * ATT_LOGIT_SCALE).astype(BF16)
        qt_ref[0, h, LANES:, :] = ((qat[o + LANES:o + 2 * LANES, :] * cost
                                    + qrt[h * LANES:(h + 1) * LANES, :] * sint) * ATT_LOGIT_SCALE).astype(BF16)
        k_ref[0, h, :, :LANES] = kn[:, h * A_NOPE:(h + 1) * A_NOPE].astype(BF16)
        k_ref[0, h, :, LANES:] = kr
        vt_ref[0, h, 0] = vt[h * A_V:(h + 1) * A_V, :].astype(BF16)


def _mla_proj(c, kr, B, S, w):
    tm = min(ATT_TILE, S)
    nseq = S // tm
    row = lambda i: (i, 0)
    const = lambda i: (0, 0)
    seq_t = lambda i: (0, i % nseq)
    return pl.pallas_call(
        _mla_proj_kernel,
        grid=(B * S // tm,),
        in_specs=[
            pl.BlockSpec((tm, 512), row),
            pl.BlockSpec((tm, LANES), row),
            pl.BlockSpec((LANES, tm), seq_t),
            pl.BlockSpec((LANES, tm), seq_t),
            pl.BlockSpec((1, A_Q_RANK), const),
            pl.BlockSpec((1, A_KV_RANK), const),
            pl.BlockSpec((A_HEADS * A_QK_PAD, A_Q_RANK), const),
            pl.BlockSpec((A_HEADS * LANES, A_Q_RANK), const),
            pl.BlockSpec((A_KV_RANK, A_HEADS * A_NOPE), const),
            pl.BlockSpec((A_HEADS * A_V, A_KV_RANK), const),
        ],
        out_specs=[
            pl.BlockSpec((1, A_HEADS, A_QK_PAD, tm), lambda i: (i // nseq, 0, 0, i % nseq)),
            pl.BlockSpec((1, A_HEADS, tm, A_QK_PAD), lambda i: (i // nseq, 0, i % nseq, 0)),
            pl.BlockSpec((1, A_HEADS, 1, A_V, tm), lambda i: (i // nseq, 0, i % nseq, 0, 0)),
        ],
        out_shape=[
            jax.ShapeDtypeStruct((B, A_HEADS, A_QK_PAD, S), BF16),
            jax.ShapeDtypeStruct((B, A_HEADS, S, A_QK_PAD), BF16),
            jax.ShapeDtypeStruct((B, A_HEADS, nseq, A_V, tm), BF16),
        ],
        compiler_params=_cparams(("parallel",)),
        name="mla_proj",
    )(c, kr, w["cos_t"], w["sin_t"], w["g_cq"], w["g_ckv"], w["w_qat"], w["w_qrt"], w["w_uk"], w["w_uvt"])


def _flash_kernel(q_ref, k_ref, vt_ref, o_ref, s_scr, *, tk, nk, unroll):
    qt = q_ref[0, 0]
    tq = qt.shape[1]

    def scores(j, slot):
        start = pl.multiple_of(j * tk, tk)
        s_scr[slot] = _dot(k_ref[0, 0, pl.ds(start, tk), :], qt)

    def accumulate(j, slot, carry):
        m, l, acc = carry
        s = s_scr[slot]
        m_new = jnp.maximum(m, jnp.max(s, axis=0, keepdims=True))
        p = jnp.exp2(s - m_new)
        alpha = jnp.exp2(m - m_new)
        l = alpha * l + jnp.sum(p, axis=0, keepdims=True)
        acc = alpha * acc + _dot(vt_ref[0, 0, j], p.astype(BF16))
        return m_new, l, acc

    nslot = s_scr.shape[0]
    ahead = nslot - 1

    def body(jj, carry):
        for t in range(unroll):
            j = unroll * jj + t
            scores(jnp.minimum(j + ahead, nk - 1), (t + ahead) % nslot)
            carry = accumulate(j, t % nslot, carry)
        return carry

    carry = (jnp.full((1, tq), -jnp.inf, F32), jnp.zeros((1, tq), F32), jnp.zeros((A_V, tq), F32))
    for j in range(min(ahead, nk)):
        scores(j, j % nslot)
    n_loop = nk // unroll
    if n_loop > 0:
        carry = lax.fori_loop(0, n_loop, body, carry)
    for j in range(unroll * n_loop, nk):
        if j + ahead < nk:
            scores(j + ahead, (j + ahead) % nslot)
        carry = accumulate(j, j % nslot, carry)
    _, l, acc = carry
    o_ref[...] = (acc / l).T.astype(BF16)


def _flash(qt, k, vt):
    B, H, S, _ = k.shape
    tq, tk, unroll = _flash_tiles(S)
    nq = S // tq
    nk = S // tk
    kern = functools.partial(_flash_kernel, tk=tk, nk=nk, unroll=unroll)
    return pl.pallas_call(
        kern,
        grid=(B, H, nq),
        in_specs=[
            pl.BlockSpec((1, 1, A_QK_PAD, tq), lambda b, h, i: (b, h, 0, i)),
            pl.BlockSpec((1, 1, S, A_QK_PAD), lambda b, h, i: (b, h, 0, 0)),
            pl.BlockSpec((1, 1, nk, A_V, tk), lambda b, h, i: (b, h, 0, 0, 0)),
        ],
        out_specs=pl.BlockSpec((tq, A_V), lambda b, h, i: (b * nq + i, h)),
        out_shape=jax.ShapeDtypeStruct((B * S, A_HEADS * A_V), BF16),
        scratch_shapes=[pltpu.VMEM((ATT_SLOTS, tk, tq), F32)],
        compiler_params=_cparams(("parallel", "parallel", "arbitrary")),
        name="mla_flash",
    )(qt, k, vt)


def _mixer_out_kernel(hf_ref, hb_ref, om_ref, ga_ref, gb_ref, att_ref, x_ref, mg_ref,
                      wa_ref, wb_ref, wo_ref, o_ref):
    hs = hf_ref[...] + hb_ref[...]
    mg = mg_ref[...]
    parts = []
    for h in range(M_HEADS):
        sl = slice(h * M_HEAD_DIM, (h + 1) * M_HEAD_DIM)
        parts.append(_rms(hs[:, sl], mg[:, sl]))
    hn = jnp.concatenate(parts, axis=1) * _sigmoid(om_ref[...].astype(F32))
    y_a = _dot(hn.astype(BF16), wa_ref[...])
    y_b = _dot(att_ref[...], wb_ref[...])
    merged = _sigmoid(ga_ref[...].astype(F32)) * y_a + _sigmoid(gb_ref[...].astype(F32)) * y_b
    o_ref[...] = x_ref[...] + _dot(merged.astype(BF16), wo_ref[...])


def _mixer_out(hf, hb, z, att, x, w):
    T = x.shape[0]
    tm = min(512, T)
    row = lambda i: (i, 0)
    const = lambda i: (0, 0)
    wspec = pl.BlockSpec((D_MODEL, D_MODEL), const)
    return pl.pallas_call(
        _mixer_out_kernel,
        grid=(T // tm,),
        in_specs=[
            pl.BlockSpec((tm, M_WIDTH), row),
            pl.BlockSpec((tm, M_WIDTH), row),
            pl.BlockSpec((tm, M_WIDTH), lambda i: (i, 3)),
            pl.BlockSpec((tm, D_MODEL), lambda i: (i, 4)),
            pl.BlockSpec((tm, D_MODEL), lambda i: (i, 5)),
            pl.BlockSpec((tm, D_MODEL), row),
            pl.BlockSpec((tm, D_MODEL), row),
            pl.BlockSpec((1, M_WIDTH), const),
            wspec, wspec, wspec,
        ],
        out_specs=pl.BlockSpec((tm, D_MODEL), row),
        out_shape=jax.ShapeDtypeStruct((T, D_MODEL), F32),
        compiler_params=_cparams(("parallel",)),
        name="mixer_out",
    )(hf, hb, z, z, z, att, x, w["mh_norm_g"], w["w_br_a"], w["w_br_b"], w["w_out"])


def _mem_kernel(m_ref, g_ref, wk_ref, wv_ref, k_ref, v_ref):
    mn = _rms(m_ref[...], g_ref[...]).astype(BF16)
    k_ref[...] = _dot(mn, wk_ref[...]).astype(BF16)
    v_ref[...] = _dot(mn, wv_ref[...]).astype(BF16)


def _mem_proj(mem, w):
    R = mem.shape[0]
    tm = 256
    row = lambda i: (i, 0)
    const = lambda i: (0, 0)
    wspec = pl.BlockSpec((D_MODEL, D_MODEL), const)
    return pl.pallas_call(
        _mem_kernel,
        grid=(R // tm,),
        in_specs=[pl.BlockSpec((tm, D_MODEL), row), pl.BlockSpec((1, D_MODEL), const), wspec, wspec],
        out_specs=[pl.BlockSpec((tm, D_MODEL), row)] * 2,
        out_shape=[jax.ShapeDtypeStruct((R, D_MODEL), BF16)] * 2,
        compiler_params=_cparams(("parallel",)),
        name="mem_proj",
    )(mem, w["norm_mem_g"], w["w_xk"], w["w_xv"])


def _cross_router_kernel(x_ref, kx_ref, vx_ref, gx_ref, gf_ref, wq_ref, wo_ref, wr_ref, wrt_ref,
                         br_ref, brt_ref, x2_ref, xn_ref, aff_ref, afft_ref):
    x1 = x_ref[...]
    xn = _rms(x1, gx_ref[...]).astype(BF16)
    q = _dot(xn, wq_ref[...])
    sls = [slice(h * X_HEAD_DIM, (h + 1) * X_HEAD_DIM) for h in range(X_HEADS)]
    ss = [_dot_nt(q[:, sl].astype(BF16), kx_ref[:, sl]) * (X_HEAD_DIM ** -0.5) for sl in sls]
    es = [jnp.exp(s - jnp.max(s, axis=1, keepdims=True)) for s in ss]
    ps = [e / jnp.sum(e, axis=1, keepdims=True) for e in es]
    o = jnp.concatenate([_dot(p.astype(BF16), vx_ref[:, sl]) for p, sl in zip(ps, sls)], axis=1)
    x2 = x1 + _dot(o.astype(BF16), wo_ref[...])
    x2_ref[...] = x2
    xf = _rms(x2, gf_ref[...])
    hi = xf.astype(BF16)
    xn_ref[...] = hi
    lo = (xf - hi.astype(F32)).astype(BF16)
    wr = wr_ref[...]
    wr_hi = wr.astype(BF16)
    wr_lo = (wr - wr_hi.astype(F32)).astype(BF16)
    logit = _dot(hi, wr_hi) + _dot(lo, wr_hi) + _dot(hi, wr_lo) + br_ref[...]
    e = jnp.exp(logit - jnp.max(logit, axis=1, keepdims=True))
    aff_ref[...] = e / jnp.sum(e, axis=1, keepdims=True)
    wrt = wrt_ref[...]
    wrt_hi = wrt.astype(BF16)
    wrt_lo = (wrt - wrt_hi.astype(F32)).astype(BF16)
    logit_t = _dot_nt(wrt_hi, hi) + _dot_nt(wrt_hi, lo) + _dot_nt(wrt_lo, hi) + brt_ref[...]
    et = jnp.exp(logit_t - jnp.max(logit_t, axis=0, keepdims=True))
    afft_ref[...] = et / jnp.sum(et, axis=0, keepdims=True)


def _cross_router(x1, kx, vx, S, n_mem, w):
    T = x1.shape[0]
    tm = min(512, S)
    nseq = S // tm
    row = lambda i: (i, 0)
    const = lambda i: (0, 0)
    wspec = pl.BlockSpec((D_MODEL, D_MODEL), const)
    memspec = pl.BlockSpec((n_mem, D_MODEL), lambda i: (i // nseq, 0))
    return pl.pallas_call(
        _cross_router_kernel,
        grid=(T // tm,),
        in_specs=[
            pl.BlockSpec((tm, D_MODEL), row), memspec, memspec,
            pl.BlockSpec((1, D_MODEL), const), pl.BlockSpec((1, D_MODEL), const),
            wspec, wspec,
            pl.BlockSpec((D_MODEL, N_EXPERTS), const), pl.BlockSpec((N_EXPERTS, D_MODEL), const),
            pl.BlockSpec((1, N_EXPERTS), const), pl.BlockSpec((N_EXPERTS, 1), const),
        ],
        out_specs=[
            pl.BlockSpec((tm, D_MODEL), row),
            pl.BlockSpec((tm, D_MODEL), row),
            pl.BlockSpec((tm, N_EXPERTS), row),
            pl.BlockSpec((N_EXPERTS, tm), lambda i: (0, i)),
        ],
        out_shape=[
            jax.ShapeDtypeStruct((T, D_MODEL), F32),
            jax.ShapeDtypeStruct((T, D_MODEL), BF16),
            jax.ShapeDtypeStruct((T, N_EXPERTS), F32),
            jax.ShapeDtypeStruct((N_EXPERTS, T), F32),
        ],
        compiler_params=_cparams(("parallel",)),
        name="cross_router",
    )(x1, kx, vx, w["norm_x_g"], w["norm_ffn_g"], w["w_xq"], w["w_xo"], w["w_router"], w["w_router_t"],
      w["b_router"], w["b_router_t"])


def _excl_cumsum(mask_f, strict_lane, strict_blk):
    nb = mask_f.shape[0]
    within = _dot(mask_f.astype(BF16), strict_lane)
    tot = jnp.sum(mask_f, axis=1, keepdims=True)
    bstart = _dot(strict_blk, jnp.broadcast_to(tot, (nb, LANES)).astype(BF16))
    return within + bstart, bstart


def _select_kernel(aff_ref, pos_ref, bst_ref, *, cap):
    a = aff_ref[0]
    nb = a.shape[0]
    bits = pltpu.bitcast(a, I32)

    def enough(cand):
        return jnp.sum(jnp.where(bits >= cand, 1.0, 0.0), axis=(0, 1), keepdims=True) >= cap

    def radix(i, prefix):
        hi = jnp.left_shift(jnp.int32(1), 29 - 2 * i)
        lo = jnp.left_shift(jnp.int32(1), 28 - 2 * i)
        return jnp.where(enough(prefix | hi | lo), prefix | hi | lo,
                         jnp.where(enough(prefix | hi), prefix | hi,
                                   jnp.where(enough(prefix | lo), prefix | lo, prefix)))

    top = jnp.full((1, 1), 1 << 30, I32)
    thr = lax.fori_loop(0, 15, radix, jnp.where(enough(top), top, 0))
    gt = bits > thr
    eq = bits == thr
    need = cap - jnp.sum(jnp.where(gt, 1.0, 0.0), axis=(0, 1), keepdims=True)
    li = lax.broadcasted_iota(I32, (LANES, LANES), 0)
    lj = lax.broadcasted_iota(I32, (LANES, LANES), 1)
    strict_lane = jnp.where(li < lj, 1.0, 0.0).astype(BF16)
    bi = lax.broadcasted_iota(I32, (nb, nb), 0)
    bj = lax.broadcasted_iota(I32, (nb, nb), 1)
    strict_blk = jnp.where(bj < bi, 1.0, 0.0).astype(BF16)
    rank, _ = _excl_cumsum(jnp.where(eq, 1.0, 0.0), strict_lane, strict_blk)
    sel = gt | (eq & (rank < need))
    pos, bstart = _excl_cumsum(jnp.where(sel, 1.0, 0.0), strict_lane, strict_blk)
    pos_ref[0] = jnp.where(sel, pos.astype(I32), -1)
    col = jnp.broadcast_to(bstart[:, 0:1], (nb, nb))
    bst_ref[0] = jnp.sum(jnp.where(bi == bj, col, 0.0), axis=0, keepdims=True).astype(I32)


def _select(aff_t, cap):
    E, T = aff_t.shape
    nb = T // LANES
    kern = functools.partial(_select_kernel, cap=cap)
    return pl.pallas_call(
        kern,
        grid=(E,),
        in_specs=[pl.BlockSpec((1, nb, LANES), lambda e: (e, 0, 0))],
        out_specs=[
            pl.BlockSpec((1, nb, LANES), lambda e: (e, 0, 0)),
            pl.BlockSpec((1, 1, nb), lambda e: (e, 0, 0)),
        ],
        out_shape=[
            jax.ShapeDtypeStruct((E, nb, LANES), I32),
            jax.ShapeDtypeStruct((E, 1, nb), I32),
        ],
        compiler_params=_cparams(("parallel",)),
        name="ec_select",
    )(aff_t.reshape(E, nb, LANES))


def _cells(bst, cap, T, tu, tc):
    E = bst.shape[0]
    ns = cap // tc
    start = bst[:, ::tu // LANES]
    end = jnp.concatenate([start[:, 1:], jnp.full((E, 1), cap, I32)], axis=1)
    cnt = end - start
    s_lo = jnp.minimum(start // tc, ns - 1)
    s_hi = jnp.where(cnt > 0, (end - 1) // tc, s_lo)
    return jnp.where(cnt > 0, s_hi - s_lo + 1, 0), s_lo


def _take(x, idx):
    hit = idx[..., :, None] == jnp.arange(x.shape[-1], dtype=I32)
    return jnp.sum(jnp.where(hit, x[..., None, :], 0), axis=-1)


def _enumerate_cells(nc_flat, slo_flat, steps):
    off_end = jnp.cumsum(nc_flat, axis=-1)
    off = off_end - nc_flat
    total = off_end[..., -1:]
    k = jnp.arange(steps, dtype=I32)
    kk = jnp.minimum(k, total - 1)
    grp = jnp.sum((off_end[..., None, :] <= kk[..., :, None]).astype(I32), axis=-1)
    s = _take(slo_flat - off, grp) + kk
    return grp, s, (k < total).astype(I32)


def _gather_steps(T, cap, tu, tc):
    return T // tu + MOE_GATHER_GROUP * (cap // tc)


def _gather_tables(bst, cap, T, tu, tc):
    E = bst.shape[0]
    G = MOE_GATHER_GROUP
    nu = T // tu
    steps = _gather_steps(T, cap, tu, tc)
    ncell, s_lo = _cells(bst, cap, T, tu, tc)
    own = nu + cap // tc
    _, o_s, o_valid = _enumerate_cells(ncell, s_lo, own)
    prev_s = jnp.concatenate([jnp.full((E, 1), -1, I32), o_s[:, :-1]], axis=1)
    o_first = o_valid * (o_s != prev_s).astype(I32)
    off = jnp.cumsum(ncell, axis=1) - ncell
    n_u = jnp.max(ncell.reshape(E // G, G, nu), axis=1)
    t_u, t_k, t_valid = _enumerate_cells(n_u, jnp.zeros_like(n_u), steps)
    u_e = jnp.repeat(t_u, G, axis=0)
    k_e = jnp.repeat(t_k, G, axis=0)
    nc_e = _take(ncell, u_e)
    off_e = _take(off, u_e)
    has = (k_e < nc_e).astype(I32) * jnp.repeat(t_valid, G, axis=0)
    cid = jnp.clip(off_e + jnp.minimum(k_e, nc_e - 1), 0, own - 1)
    s_e = _take(o_s, cid)
    first_e = has * _take(o_first, cid)
    per_expert = lambda a: a.reshape(E // G, G, steps).transpose(0, 2, 1).reshape(-1)
    return t_u.reshape(-1), per_expert(s_e), per_expert(has), per_expert(first_e)


def _combine_tables(bst, cap, T, tu, tc):
    E = bst.shape[0]
    nu = T // tu
    ns = cap // tc
    G = MOE_COMBINE_GROUP
    ncell, s_lo = _cells(bst, cap, T, tu, tc)
    ncell_c = ncell.at[0].set(jnp.maximum(ncell[0], 1))
    pc = E * (nu + ns)
    c_grp, c_s, _ = _enumerate_cells(ncell_c.T.reshape(-1), s_lo.T.reshape(-1), pc)
    c_e = c_grp % E
    cells_u = jnp.sum(ncell_c, axis=0)
    cell_off = jnp.cumsum(cells_u) - cells_u
    groups_u = (cells_u + G - 1) // G
    grp_end = jnp.cumsum(groups_u)
    n_steps = pc // G + nu
    k = jnp.arange(n_steps, dtype=I32)
    kk = jnp.minimum(k, grp_end[-1] - 1)
    t_u = jnp.sum((grp_end[None, :] <= kk[:, None]).astype(I32), axis=1)
    j = kk - _take(grp_end - groups_u, t_u)
    t_valid = (k < grp_end[-1]).astype(I32)
    local = j[:, None] * G + jnp.arange(G, dtype=I32)[None, :]
    cells_t = _take(cells_u, t_u)[:, None]
    cell_ok = (local < cells_t).astype(I32) * t_valid[:, None]
    cid = jnp.minimum(_take(cell_off, t_u)[:, None] + jnp.minimum(local, cells_t - 1), pc - 1).reshape(-1)
    t_first = t_valid * (j == 0).astype(I32)
    t_last = t_valid * (j == _take(groups_u, t_u) - 1).astype(I32)
    return (t_u, _take(c_e, cid), _take(c_s, cid), cell_ok.reshape(-1), t_valid, t_first, t_last)


def _gather_kernel(u_tab, s_tab, has_tab, first_tab, pos_ref, x_ref, *o_refs, steps, tc):
    G = MOE_GATHER_GROUP
    base = (pl.program_id(0) * steps + pl.program_id(1)) * G
    tu = pos_ref.shape[-1]
    for g, o_ref in enumerate(o_refs):
        slot = lax.broadcasted_iota(I32, (tc, tu), 0) + s_tab[base + g] * tc
        onehot = jnp.where(pos_ref[g] == slot, 1.0, 0.0).astype(BF16)

        @pl.when(first_tab[base + g] == 1)
        def _():
            o_ref[0] = _dot(onehot, x_ref[...]).astype(BF16)

        @pl.when((has_tab[base + g] == 1) & (first_tab[base + g] == 0))
        def _():
            o_ref[0] = o_ref[0] + _dot(onehot, x_ref[...]).astype(BF16)


def _moe_gather(xn, pos_row, tabs, cap, tu, tc):
    T = xn.shape[0]
    E = N_EXPERTS
    G = MOE_GATHER_GROUP
    steps = _gather_steps(T, cap, tu, tc)
    kern = functools.partial(_gather_kernel, steps=steps, tc=tc)

    def out_spec(g):
        return pl.BlockSpec((1, tc, D_MODEL), lambda p, k, u, s, *_: (p, s[(p * steps + k) * G + g], 0))

    grid_spec = pltpu.PrefetchScalarGridSpec(
        num_scalar_prefetch=4,
        grid=(E // G, steps),
        in_specs=[
            pl.BlockSpec((G, 1, tu), lambda p, k, u, *_: (p, 0, u[p * steps + k])),
            pl.BlockSpec((tu, D_MODEL), lambda p, k, u, *_: (u[p * steps + k], 0)),
        ],
        out_specs=[out_spec(g) for g in range(G)],
    )
    return pl.pallas_call(
        kern,
        grid_spec=grid_spec,
        out_shape=[jax.ShapeDtypeStruct((E // G, cap, D_MODEL), BF16)] * G,
        compiler_params=_cparams(("parallel", "arbitrary")),
        name="moe_gather",
    )(*tabs, pos_row, xn)


def _ffn_kernel(*refs):
    G = MOE_GATHER_GROUP
    x_refs = refs[:G]
    wg_ref, wu_ref, wd_ref, o_ref, wg_b, wu_b, wd_b = refs[G:]
    e = pl.program_id(0)

    @pl.when(pl.program_id(1) == 0)
    def _():
        wg_b[...] = wg_ref[0].astype(BF16)
        wu_b[...] = wu_ref[0].astype(BF16)
        wd_b[...] = wd_ref[0].astype(BF16)

    x = x_refs[0][0]
    for g in range(1, G):
        x = jnp.where(e % G == g, x_refs[g][0], x)
    gate = _dot(x, wg_b[...])
    up = _dot(x, wu_b[...])
    h = (gate * _sigmoid(gate) * up).astype(BF16)
    o_ref[0] = _dot(h, wd_b[...]).astype(BF16)


def _moe_ffn(xes, w):
    G = MOE_GATHER_GROUP
    _, cap, _ = xes[0].shape
    E = N_EXPERTS
    tf = min(512, cap)
    wspec = pl.BlockSpec((1, D_MODEL, E_FF), lambda e, i: (e, 0, 0))

    def x_spec(g):
        return pl.BlockSpec((1, tf, D_MODEL), lambda e, i: (e // G, jnp.where(e % G == g, i, 0), 0))

    return pl.pallas_call(
        _ffn_kernel,
        grid=(E, cap // tf),
        in_specs=[x_spec(g) for g in range(G)]
        + [wspec, wspec, pl.BlockSpec((1, E_FF, D_MODEL), lambda e, i: (e, 0, 0))],
        out_specs=pl.BlockSpec((1, tf, D_MODEL), lambda e, i: (e, i, 0)),
        out_shape=jax.ShapeDtypeStruct((E, cap, D_MODEL), BF16),
        scratch_shapes=[pltpu.VMEM((D_MODEL, E_FF), BF16), pltpu.VMEM((D_MODEL, E_FF), BF16),
                        pltpu.VMEM((E_FF, D_MODEL), BF16)],
        compiler_params=_cparams(("parallel", "arbitrary")),
        name="moe_ffn",
    )(*xes, w["w_e_gate"], w["w_e_up"], w["w_e_down"])


def _combine_kernel(u_tab, e_tab, s_tab, ok_tab, valid_tab, first_tab, last_tab, pos_ref, aff_ref, *rest, tc):
    G = MOE_COMBINE_GROUP
    ye_refs = rest[:G]
    x_ref, g_ref, o_ref, acc = rest[G:]
    step = pl.program_id(0)
    tu = pos_ref.shape[0]

    @pl.when(first_tab[step] == 1)
    def _():
        acc[...] = jnp.zeros(acc.shape, F32)

    @pl.when(valid_tab[step] == 1)
    def _():
        pos = pos_ref[...].astype(F32)
        aff = aff_ref[...]
        lane_e = lax.broadcasted_iota(I32, pos.shape, 1)
        lane_r = lax.broadcasted_iota(I32, (tu, tc), 1).astype(F32)
        total = None
        for g0 in range(0, G, 2):
            hits = []
            for g in (g0, g0 + 1):
                c = step * G + g
                mine = lane_e == e_tab[c]
                slot = jnp.sum(jnp.where(mine, pos, 0.0), axis=1, keepdims=True)
                gate = jnp.sum(jnp.where(mine, aff, 0.0), axis=1, keepdims=True)
                base = jnp.where(ok_tab[c] == 1, s_tab[c] * tc, -2 * tc).astype(F32)
                hits.append(jnp.where(slot - base == lane_r, gate, 0.0).astype(BF16))
            part = _dot(jnp.concatenate(hits, axis=1),
                        jnp.concatenate([ye_refs[g0][0], ye_refs[g0 + 1][0]], axis=0))
            total = part if total is None else total + part
        acc[...] = acc[...] + total

    @pl.when(last_tab[step] == 1)
    def _():
        o_ref[...] = _rms(x_ref[...] + acc[...], g_ref[...])


def _moe_combine(ye, pos_col, aff, x2, tabs, final_g, cap, tu, tc):
    T = x2.shape[0]
    E = N_EXPERTS
    G = MOE_COMBINE_GROUP
    steps = E * (T // tu + cap // tc) // G + T // tu
    kern = functools.partial(_combine_kernel, tc=tc)

    def ye_spec(g):
        return pl.BlockSpec((1, tc, D_MODEL), lambda k, u, e, s, *_: (e[k * G + g], s[k * G + g], 0))

    tile = pl.BlockSpec((tu, D_MODEL), lambda k, u, *_: (u[k], 0))
    per_expert = pl.BlockSpec((tu, N_EXPERTS), lambda k, u, *_: (u[k], 0))
    grid_spec = pltpu.PrefetchScalarGridSpec(
        num_scalar_prefetch=7,
        grid=(steps,),
        in_specs=[per_expert, per_expert]
        + [ye_spec(g) for g in range(G)]
        + [tile, pl.BlockSpec((1, D_MODEL), lambda k, *_: (0, 0))],
        out_specs=tile,
        scratch_shapes=[pltpu.VMEM((tu, D_MODEL), F32)],
    )
    return pl.pallas_call(
        kern,
        grid_spec=grid_spec,
        out_shape=jax.ShapeDtypeStruct((T, D_MODEL), F32),
        compiler_params=_cparams(("arbitrary",)),
        name="moe_combine",
    )(*tabs, pos_col, aff, *([ye] * G), x2, final_g)


def _rope_tables(S):
    pos = jnp.arange(S, dtype=F32)
    inv = ROPE_BASE ** (-jnp.arange(0, A_ROPE, 2, dtype=F32) / A_ROPE)
    ang = pos[:, None] * inv[None, :]
    pad = jnp.zeros((S, LANES - A_ROPE), F32)
    cos = jnp.concatenate([jnp.cos(ang), jnp.cos(ang), pad], axis=1)
    sin = jnp.concatenate([jnp.sin(ang), jnp.sin(ang), pad], axis=1)
    return cos, sin


def _rotate_half_cols(w):
    half = A_ROPE // 2
    return jnp.concatenate([-w[..., half:], w[..., :half]], axis=-1)


def _prep_weights(norm_mix_g, w_in, b_gates, conv_w, conv_b, mh_norm_g, g_cq, g_ckv, w_uq, w_ukv,
                  w_br_a, w_br_b, w_out, norm_x_g, norm_mem_g, w_xq, w_xk, w_xv, w_xo, norm_ffn_g,
                  w_router, b_router, w_e_gate, w_e_up, w_e_down, final_norm_g):
    l = 0
    wi = w_in[l]
    o = 0
    cols = {}
    for name, n in (("qm", M_WIDTH), ("km", M_WIDTH), ("vm", M_WIDTH), ("om", M_WIDTH), ("gates", 4 * M_HEADS),
                    ("cq", A_Q_RANK), ("ckv", A_KV_RANK), ("kr", A_ROPE), ("ga", D_MODEL), ("gb", D_MODEL)):
        cols[name] = wi[:, o:o + n]
        o += n
    zpad = jnp.zeros((D_MODEL, LANES - A_ROPE), F32)
    w_kr = jnp.concatenate([cols["kr"], zpad, _rotate_half_cols(cols["kr"]), zpad], axis=1)
    uq = w_uq[l].reshape(A_Q_RANK, A_HEADS, A_NOPE + A_ROPE)
    uq_rope = uq[:, :, A_NOPE:]
    hpad = jnp.zeros((A_Q_RANK, A_HEADS, LANES - A_ROPE), F32)
    w_qa = jnp.concatenate([uq, hpad], axis=2).reshape(A_Q_RANK, A_HEADS * A_QK_PAD)
    w_qr = jnp.concatenate([_rotate_half_cols(uq_rope), hpad], axis=2).reshape(A_Q_RANK, A_HEADS * LANES)
    ukv = w_ukv[l].reshape(A_KV_RANK, A_HEADS, A_NOPE + A_V)
    row = lambda v: v.reshape(1, -1).astype(F32)
    return {
        "norm_mix_g": row(norm_mix_g[l]),
        "w_big": jnp.concatenate([cols[n] for n in ("qm", "km", "vm", "om", "ga", "gb")], axis=1).astype(BF16),
        "w_c": jnp.concatenate([cols["cq"], cols["ckv"]], axis=1).astype(BF16),
        "w_kr": w_kr.astype(BF16),
        "w_g": cols["gates"].astype(BF16),
        "w_gt": cols["gates"].T.astype(BF16),
        "b_g": row(b_gates[l]),
        "b_gt": b_gates[l].reshape(-1, 1).astype(F32),
        "conv_w": conv_w[l],
        "conv_b": row(conv_b[l]),
        "mh_norm_g": row(mh_norm_g[l]),
        "g_cq": row(g_cq[l]),
        "g_ckv": row(g_ckv[l]),
        "w_qat": w_qa.T.astype(BF16),
        "w_qrt": w_qr.T.astype(BF16),
        "w_uk": ukv[:, :, :A_NOPE].reshape(A_KV_RANK, A_HEADS * A_NOPE).astype(BF16),
        "w_uvt": ukv[:, :, A_NOPE:].reshape(A_KV_RANK, A_HEADS * A_V).T.astype(BF16),
        "w_br_a": w_br_a[l].astype(BF16),
        "w_br_b": w_br_b[l].astype(BF16),
        "w_out": w_out[l].astype(BF16),
        "norm_x_g": row(norm_x_g[l]),
        "norm_mem_g": row(norm_mem_g[l]),
        "w_xq": w_xq[l].astype(BF16),
        "w_xk": w_xk[l].astype(BF16),
        "w_xv": w_xv[l].astype(BF16),
        "w_xo": w_xo[l].astype(BF16),
        "norm_ffn_g": row(norm_ffn_g[l]),
        "w_router": w_router[l],
        "w_router_t": w_router[l].T,
        "b_router": row(b_router[l]),
        "b_router_t": b_router[l].reshape(-1, 1).astype(F32),
        "w_e_gate": w_e_gate[l],
        "w_e_up": w_e_up[l],
        "w_e_down": w_e_down[l],
        "final_norm_g": row(final_norm_g),
    }


def _trunk(x, mem, w):
    B, S, _ = x.shape
    T = B * S
    n_mem = mem.shape[1]
    w = dict(w)
    w["cos"], w["sin"] = _rope_tables(S)
    w["cos_t"], w["sin_t"] = w["cos"].T, w["sin"].T
    x2d = x.reshape(T, D_MODEL)

    z, c, kr, gate, gatet = _inproj(x2d, S, w)
    qm = _conv(z, S, w, col0=0, scale=M_HEAD_DIM ** -0.5, transpose=False)
    kmt = _conv(z, S, w, col0=1, scale=1.0, transpose=True)
    hf, hb = _mlstm(qm, kmt, z, gate, gatet, B, S)
    qc, kc, vc = _mla_proj(c, kr, B, S, w)
    att = _flash(qc, kc, vc)
    x1 = _mixer_out(hf, hb, z, att, x2d, w)

    kx, vx = _mem_proj(mem.reshape(B * n_mem, D_MODEL), w)
    x2, xn, aff, aff_t = _cross_router(x1, kx, vx, S, n_mem, w)

    cap = max(1, EC_FACTOR * T // N_EXPERTS)
    pos, bst = _select(aff_t, cap)
    tu = min(MOE_TOKEN_TILE, T)
    tc = min(MOE_SLOT_TILE, cap)
    tuc = min(MOE_COMBINE_TOKEN_TILE, T)
    tcc = min(MOE_COMBINE_SLOT_TILE, cap)
    bst = bst.reshape(N_EXPERTS, -1)
    xe = _moe_gather(xn, pos.reshape(N_EXPERTS, 1, T), _gather_tables(bst, cap, T, tu, tc), cap, tu, tc)
    ye = _moe_ffn(xe, w)
    y = _moe_combine(ye, pos.reshape(N_EXPERTS, T).T, aff, x2, _combine_tables(bst, cap, T, tuc, tcc),
                     w["final_norm_g"], cap, tuc, tcc)
    return y.reshape(B, S, D_MODEL)


def kernel(x_prompt, x_sample, mem_prompt, mem_sample, norm_mix_g, w_in, b_gates, conv_w, conv_b, mh_norm_g, g_cq, g_ckv, w_uq, w_ukv, w_br_a, w_br_b, w_out, norm_x_g, norm_mem_g, w_xq, w_xk, w_xv, w_xo, norm_ffn_g, w_router, b_router, w_e_gate, w_e_up, w_e_down, final_norm_g):
    w = _prep_weights(norm_mix_g, w_in, b_gates, conv_w, conv_b, mh_norm_g, g_cq, g_ckv, w_uq, w_ukv,
                      w_br_a, w_br_b, w_out, norm_x_g, norm_mem_g, w_xq, w_xk, w_xv, w_xo, norm_ffn_g,
                      w_router, b_router, w_e_gate, w_e_up, w_e_down, final_norm_g)
    return (_trunk(x_prompt, mem_prompt, w), _trunk(x_sample, mem_sample, w))
```
